```python
import math
import jax, jax.numpy as jnp
from jax import lax
import numpy as np

D_MODEL = 1024
BATCH = 4
SEQ = 4096
DEPTH = 2
DEC_BATCH = 128
DEC_SEQ = 1
PAST_LEN = 2048
PAGE_SIZE = 128

HEAD_DIM = 64
N_NSA_HEADS = (D_MODEL // 2) // HEAD_DIM
N_NSA_KV = N_NSA_HEADS // 4
NSA_GROUP = N_NSA_HEADS // N_NSA_KV
N_GDN_HEADS = (D_MODEL // 2) // HEAD_DIM
NSA_WIDTH = N_NSA_HEADS * HEAD_DIM
GDN_WIDTH = N_GDN_HEADS * HEAD_DIM
MIX_WIDTH = NSA_WIDTH + GDN_WIDTH
KV_WIDTH = N_NSA_KV * HEAD_DIM
GDN_QKV = 3 * GDN_WIDTH
CMP_BLOCK = 32
SLC_BLOCK = 64
TOP_N = 16
N_LOCAL = 2
WINDOW = 512
Q_BLOCK = 128
GDN_CHUNK = 64
GDN_CONV = 4
FFN_CONV = 3
D_FF = ((8 * D_MODEL // 3 + 127) // 128) * 128
PLE_DIM = 256
LN_EPS = 1e-5
RMS_EPS = 1e-6
DN_ALPHA = (2 * DEPTH) ** 0.25
DN_BETA = (8 * DEPTH) ** -0.25
NEG = -1e30

C_Q = 0
C_KV = C_Q + NSA_WIDTH
C_WIN = C_KV + 4 * KV_WIDTH
C_GATE = C_WIN + 2 * KV_WIDTH
C_GQKV = C_GATE + 3 * N_NSA_HEADS
C_GA = C_GQKV + GDN_QKV
C_GB = C_GA + N_GDN_HEADS
C_GZ = C_GB + N_GDN_HEADS
IN_WIDTH = C_GZ + GDN_WIDTH
V_COLS = ((C_KV + KV_WIDTH, C_KV + 2 * KV_WIDTH), (C_KV + 3 * KV_WIDTH, C_WIN),
          (C_WIN + KV_WIDTH, C_GATE), (C_GQKV + 2 * GDN_WIDTH, C_GA))

kernel_name = "nsa_gdn_hybrid_decoder_step"


def alibi_slopes():
    return 2.0 ** (-8.0 * jnp.arange(1, N_NSA_HEADS + 1, dtype=jnp.float32) / N_NSA_HEADS)


def layer_norm(x, g, b):
    xf = x.astype(jnp.float32)
    mu = xf.mean(-1, keepdims=True)
    var = jnp.square(xf - mu).mean(-1, keepdims=True)
    return ((xf - mu) * lax.rsqrt(var + LN_EPS) * g.astype(jnp.float32) + b.astype(jnp.float32)).astype(x.dtype)


def causal_dwconv(x, buf, w):
    xp = jnp.concatenate([buf.astype(x.dtype), x], axis=1)
    y = lax.conv_general_dilated(xp, w[:, None, :].astype(x.dtype), window_strides=(1,), padding='VALID',
                                 dimension_numbers=('NWC', 'WIO', 'NWC'), feature_group_count=x.shape[-1])
    return y, xp[:, -(w.shape[0] - 1):]


def compress_blocks(rows, pe, phi):
    B, Lp, Hk, D = rows.shape
    blk = rows.reshape(B, Lp // CMP_BLOCK, CMP_BLOCK, Hk, D) + pe[None, None, :, None, :]
    return jnp.einsum('bnhd,de->bnhe', blk.mean(axis=2), phi)


def nsa_block(q, q_pos, kc, vc, ks_blk, vs_blk):
    B, Tq, H, D = q.shape
    Nc, Ns = kc.shape[1], ks_blk.shape[2]
    qg = q.reshape(B, Tq, N_NSA_KV, NSA_GROUP, D)
    sl = alibi_slopes().reshape(N_NSA_KV, NSA_GROUP)
    scale = D ** -0.5
    qpf = q_pos.astype(jnp.float32)
    c_end = jnp.arange(Nc) * CMP_BLOCK + (CMP_BLOCK - 1)
    c_mid = jnp.arange(Nc, dtype=jnp.float32) * CMP_BLOCK + (CMP_BLOCK - 1) / 2
    c_mask = c_end[None, :] <= q_pos[:, None]
    s = jnp.einsum('btngd,bcnd->bngtc', qg, kc).astype(jnp.float32) * scale
    s = s - sl[None, :, :, None, None] * (qpf[:, None] - c_mid[None, :])[None, None, None]
    p = jax.nn.softmax(jnp.where(c_mask, s, NEG), axis=-1) * c_mask
    o_cmp = jnp.einsum('bngtc,bcnd->btngd', p.astype(vc.dtype), vc)
    imp = p.sum(axis=2).reshape(B, N_NSA_KV, Tq, Ns, SLC_BLOCK // CMP_BLOCK).sum(-1)
    blk = jnp.arange(Ns)
    cur = q_pos // SLC_BLOCK
    future = blk[None, :] > cur[:, None]
    forced = (blk[None, :] == 0) | (((cur[:, None] - blk[None, :]) < N_LOCAL) & ~future)
    score = jnp.where(future, -jnp.inf, jnp.where(forced, jnp.inf, imp))
    k_eff = min(TOP_N, Ns)
    _, idx = lax.top_k(score, k_eff)
    b_i = jnp.arange(B)[:, None, None, None]
    h_i = jnp.arange(N_NSA_KV)[None, :, None, None]
    kg = ks_blk[b_i, h_i, idx]
    vg = vs_blk[b_i, h_i, idx]
    tok = idx[..., None] * SLC_BLOCK + jnp.arange(SLC_BLOCK)
    dist = q_pos[None, None, :, None, None] - tok
    valid = dist >= 0
    s2 = jnp.einsum('btngd,bntksd->bngtks', qg, kg).astype(jnp.float32) * scale
    s2 = s2 - sl[None, :, :, None, None, None] * dist.astype(jnp.float32)[:, :, None]
    s2 = jnp.where(valid[:, :, None], s2, NEG).reshape(B, N_NSA_KV, NSA_GROUP, Tq, k_eff * SLC_BLOCK)
    p2 = jax.nn.softmax(s2, axis=-1).reshape(B, N_NSA_KV, NSA_GROUP, Tq, k_eff, SLC_BLOCK)
    o_slc = jnp.einsum('bngtks,bntksd->btngd', p2.astype(vg.dtype), vg)
    return o_cmp.reshape(B, Tq, H, D), o_slc.reshape(B, Tq, H, D)


def window_attend(q, q_pos, k, v, k_pos):
    B, NB, Tb, H, D = q.shape
    qg = q.reshape(B, NB, Tb, N_NSA_KV, NSA_GROUP, D)
    sl = alibi_slopes().reshape(N_NSA_KV, NSA_GROUP)
    s = jnp.einsum('bjtngd,bjsnd->bjngts', qg, k).astype(jnp.float32) * (D ** -0.5)
    dist = q_pos[:, :, None] - k_pos[:, None, :]
    mask = (dist >= 0) & (dist < WINDOW) & (k_pos[:, None, :] >= 0)
    s = s - sl[None, None, :, :, None, None] * dist.astype(jnp.float32)[None, :, None, None]
    p = jax.nn.softmax(jnp.where(mask[None, :, None, None], s, NEG), axis=-1)
    o = jnp.einsum('bjngts,bjsnd->bjtngd', p.astype(v.dtype), v)
    return o.reshape(B, NB, Tb, H, D)


def gated_delta_chunked(q, k, v, g, beta, S0):
    B, T, H, D = q.shape
    C = GDN_CHUNK
    Tp = -(-T // C) * C

    def prep(a):
        a = jnp.pad(a, [(0, 0), (0, Tp - T)] + [(0, 0)] * (a.ndim - 2))
        a = a.reshape((B, Tp // C, C) + a.shape[2:])
        return jnp.moveaxis(jnp.moveaxis(a, 3, 2), 1, 0)

    q, k, v, g, beta = prep(q), prep(k), prep(v), prep(g), prep(beta)
    gc = jnp.cumsum(g, axis=-1)
    ar = jnp.arange(C)
    diff = gc[..., :, None] - gc[..., None, :]
    dec_strict = jnp.exp(jnp.where(ar[:, None] > ar[None, :], diff, -jnp.inf))
    dec_incl = jnp.exp(jnp.where(ar[:, None] >= ar[None, :], diff, -jnp.inf))
    kb = k * beta[..., None]
    A = jnp.einsum('nbhid,nbhjd->nbhij', kb, k) * dec_strict + jnp.eye(C, dtype=jnp.float32)
    rhs = jnp.concatenate([v * beta[..., None], kb * jnp.exp(gc)[..., None]], axis=-1)
    X = lax.linalg.triangular_solve(A, rhs, left_side=True, lower=True)
    val, kcd = X[..., :D], X[..., D:]
    inner = jnp.einsum('nbhid,nbhjd->nbhij', q, k) * dec_incl
    qg = q * jnp.exp(gc)[..., None]
    glast = gc[..., -1]
    kend = k * jnp.exp(glast[..., None] - gc)[..., None]

    def step(S, xs):
        val_c, kcd_c, inner_c, qg_c, kend_c, gl = xs
        vn = val_c - jnp.einsum('bhcd,bhde->bhce', kcd_c, S)
        o = jnp.einsum('bhcd,bhde->bhce', qg_c, S) + jnp.einsum('bhij,bhje->bhie', inner_c, vn)
        S = S * jnp.exp(gl)[..., None, None] + jnp.einsum('bhcd,bhce->bhde', kend_c, vn)
        return S, o

    S, o = lax.scan(step, S0, (val, kcd, inner, qg, kend, glast))
    o = jnp.moveaxis(jnp.moveaxis(o, 0, 1), 2, 3).reshape(B, Tp, H, D)[:, :T]
    return o, S


def l2norm(x):
    return x * lax.rsqrt(jnp.sum(x * x, axis=-1, keepdims=True) + RMS_EPS)


def trunk_layer(x, p, kv_past, win_buf, gdn_state, gdn_buf, ffn_buf,
                w_in, nsa_pe, nsa_phi, gdn_conv_w, gdn_A_log, gdn_dt_bias, gdn_norm_w, w_out,
                ln_g, ln_b, ffn_w_up, ffn_conv_w, ffn_w_down, ple_w_proj, ple_w_gate):
    B, T, _ = x.shape
    H, Hk, D = N_NSA_HEADS, N_NSA_KV, HEAD_DIM
    f32 = jnp.float32
    past = 0 if kv_past is None else kv_past.shape[1]
    q_pos = past + jnp.arange(T, dtype=jnp.int32)
    h = x @ w_in
    q = h[..., C_Q:C_KV].reshape(B, T, H, D)
    kv_new = h[..., C_KV:C_WIN].reshape(B, T, 4, Hk, D)
    win_new = h[..., C_WIN:C_GATE].reshape(B, T, 2, Hk, D)
    gates = jax.nn.sigmoid(h[..., C_GATE:C_GQKV]).reshape(B, T, H, 3)
    kv_all = kv_new if kv_past is None else jnp.concatenate([kv_past.astype(kv_new.dtype), kv_new], axis=1)
    L = kv_all.shape[1]
    Lp = -(-L // SLC_BLOCK) * SLC_BLOCK
    kv_all = jnp.pad(kv_all, ((0, 0), (0, Lp - L), (0, 0), (0, 0), (0, 0)))
    kc = compress_blocks(kv_all[:, :, 0], nsa_pe[0], nsa_phi[0])
    vc = compress_blocks(kv_all[:, :, 1], nsa_pe[1], nsa_phi[1])
    sel = kv_all[:, :, 2:4].reshape(B, Lp // SLC_BLOCK, SLC_BLOCK, 2, Hk, D).transpose(3, 0, 4, 1, 2, 5)
    ks_blk, vs_blk = sel[0], sel[1]
    if T > Q_BLOCK and T % Q_BLOCK == 0:
        nb = T // Q_BLOCK
        o_cmp, o_slc = lax.map(lambda a: nsa_block(a[0], a[1], kc, vc, ks_blk, vs_blk),
                               (q.reshape(B, nb, Q_BLOCK, H, D).swapaxes(0, 1), q_pos.reshape(nb, Q_BLOCK)))
        o_cmp = o_cmp.swapaxes(0, 1).reshape(B, T, H, D)
        o_slc = o_slc.swapaxes(0, 1).reshape(B, T, H, D)
    else:
        o_cmp, o_slc = nsa_block(q, q_pos, kc, vc, ks_blk, vs_blk)
    if win_buf is None:
        nb = T // Q_BLOCK
        nw = WINDOW // Q_BLOCK + 1
        wp = jnp.pad(win_new, ((0, 0), (WINDOW, 0), (0, 0), (0, 0), (0, 0))).reshape(B, nb + nw - 1, Q_BLOCK, 2, Hk, D)
        wk = jnp.stack([wp[:, i:i + nb] for i in range(nw)], axis=2).reshape(B, nb, nw * Q_BLOCK, 2, Hk, D)
        k_pos = jnp.arange(nb)[:, None] * Q_BLOCK - WINDOW + jnp.arange(nw * Q_BLOCK)[None, :]
        o_win = window_attend(q.reshape(B, nb, Q_BLOCK, H, D), q_pos.reshape(nb, Q_BLOCK),
                              wk[:, :, :, 0], wk[:, :, :, 1], k_pos)
        new_win = win_new[:, -min(WINDOW, T):]
    else:
        wb = win_buf.shape[1]
        wk = jnp.concatenate([win_buf.astype(win_new.dtype), win_new], axis=1)
        k_pos = past - wb + jnp.arange(wb + T)
        o_win = window_attend(q[:, None], q_pos[None], wk[:, None, :, 0], wk[:, None, :, 1], k_pos[None])
        new_win = wk[:, -wb:]
    o_win = o_win.reshape(B, T, H, D)
    o_nsa = gates[..., 0:1] * o_cmp + gates[..., 1:2] * o_slc + gates[..., 2:3] * o_win
    if gdn_buf is None:
        gdn_buf = jnp.zeros((B, GDN_CONV - 1, GDN_QKV), x.dtype)
        gdn_state = jnp.zeros((B, N_GDN_HEADS, HEAD_DIM, HEAD_DIM), f32)
    qkv, new_gbuf = causal_dwconv(h[..., C_GQKV:C_GA], gdn_buf, gdn_conv_w)
    qkv = jax.nn.silu(qkv.astype(f32)).reshape(B, T, 3, N_GDN_HEADS, HEAD_DIM)
    gq = l2norm(qkv[:, :, 0]) * (HEAD_DIM ** -0.5)
    gk = l2norm(qkv[:, :, 1])
    gv = qkv[:, :, 2]
    g = -jnp.exp(gdn_A_log.astype(f32)) * jax.nn.softplus(h[..., C_GA:C_GB].astype(f32) + gdn_dt_bias.astype(f32))
    beta = jax.nn.sigmoid(h[..., C_GB:C_GZ].astype(f32))
    o_g, S = gated_delta_chunked(gq, gk, gv, g, beta, gdn_state.astype(f32))
    z = h[..., C_GZ:IN_WIDTH].astype(f32).reshape(B, T, N_GDN_HEADS, HEAD_DIM)
    o_g = o_g * lax.rsqrt(jnp.mean(o_g * o_g, axis=-1, keepdims=True) + RMS_EPS) * gdn_norm_w.astype(f32) * jax.nn.silu(z)
    mix = jnp.concatenate([o_nsa.reshape(B, T, NSA_WIDTH), o_g.reshape(B, T, GDN_WIDTH).astype(x.dtype)], axis=-1)
    x = layer_norm(DN_ALPHA * x + mix @ w_out, ln_g[0], ln_b[0])
    if ffn_buf is None:
        ffn_buf = jnp.zeros((B, FFN_CONV - 1, D_FF), x.dtype)
    up = x @ ffn_w_up
    hg, new_fbuf = causal_dwconv(up[..., :D_FF], ffn_buf, ffn_conv_w)
    ffn = (jax.nn.gelu(hg, approximate=False) * up[..., D_FF:]) @ ffn_w_down
    x = layer_norm(DN_ALPHA * x + ffn, ln_g[1], ln_b[1])
    ple = jax.nn.sigmoid(x @ ple_w_gate) * (p.astype(x.dtype) @ ple_w_proj)
    x = layer_norm(DN_ALPHA * x + ple, ln_g[2], ln_b[2])
    return x, (kv_new, new_win, S, new_gbuf, new_fbuf)


def setup_inputs(seed: int = 0) -> dict:
    key = jax.random.key(seed)
    ks = jax.random.split(key, 26)
    f32 = jnp.float32

    def nrm(k, shape, s=1.0):
        return jax.random.normal(k, shape, f32) * s

    n_pages = PAST_LEN // PAGE_SIZE
    n_used = DEC_BATCH * n_pages
    n_pool = (5 * n_used + 3) // 4
    wb = min(WINDOW, PAST_LEN)
    col_scale = jnp.ones((IN_WIDTH,), f32)
    for lo, hi in V_COLS:
        col_scale = col_scale.at[lo:hi].set(DN_BETA)
    page_table = jax.random.permutation(ks[5], n_pool)[:n_used].reshape(DEC_BATCH, n_pages).astype(jnp.int32)
    dt = jnp.exp(jax.random.uniform(ks[12], (DEPTH, N_GDN_HEADS), f32, math.log(1e-3), math.log(1e-1)))
    return {
        "x_prompt": nrm(ks[0], (BATCH, SEQ, D_MODEL)),
        "x_sample": nrm(ks[1], (DEC_BATCH, DEC_SEQ, D_MODEL)),
        "cache_nsa_kv": nrm(ks[2], (DEPTH, n_pool, PAGE_SIZE, 4, N_NSA_KV, HEAD_DIM)),
        "state_nsa_win": nrm(ks[3], (DEPTH, DEC_BATCH, wb, 2, N_NSA_KV, HEAD_DIM)),
        "state_gdn": nrm(ks[4], (DEPTH, DEC_BATCH, N_GDN_HEADS, HEAD_DIM, HEAD_DIM), 0.1),
        "state_gdn_conv": nrm(ks[6], (DEPTH, DEC_BATCH, GDN_CONV - 1, GDN_QKV)),
        "state_ffn_conv": nrm(ks[7], (DEPTH, DEC_BATCH, FFN_CONV - 1, D_FF), 0.5),
        "page_table": page_table,
        "p_prompt": nrm(ks[8], (DEPTH, BATCH, SEQ, PLE_DIM)),
        "p_sample": nrm(ks[9], (DEPTH, DEC_BATCH, DEC_SEQ, PLE_DIM)),
        "w_in": nrm(ks[10], (DEPTH, D_MODEL, IN_WIDTH), D_MODEL ** -0.5) * col_scale,
        "nsa_pe": nrm(ks[11], (DEPTH, 2, CMP_BLOCK, HEAD_DIM), 0.02),
        "nsa_phi": nrm(ks[13], (DEPTH, 2, HEAD_DIM, HEAD_DIM), (CMP_BLOCK / HEAD_DIM) ** 0.5),
        "gdn_conv_w": nrm(ks[14], (DEPTH, GDN_CONV, GDN_QKV), GDN_CONV ** -0.5),
        "gdn_A_log": jnp.log(jax.random.uniform(ks[15], (DEPTH, N_GDN_HEADS), f32, 1.0, 16.0)),
        "gdn_dt_bias": dt + jnp.log(-jnp.expm1(-dt)),
        "gdn_norm_w": 1.0 + nrm(ks[16], (DEPTH, HEAD_DIM), 0.02),
        "w_out": nrm(ks[17], (DEPTH, MIX_WIDTH, D_MODEL), MIX_WIDTH ** -0.5 * DN_BETA),
        "ln_g": 1.0 + nrm(ks[18], (DEPTH, 3, D_MODEL), 0.02),
        "ln_b": nrm(ks[19], (DEPTH, 3, D_MODEL), 0.02),
        "ffn_w_up": nrm(ks[20], (DEPTH, D_MODEL, 2 * D_FF), D_MODEL ** -0.5 * DN_BETA),
        "ffn_conv_w": nrm(ks[21], (DEPTH, FFN_CONV, D_FF), FFN_CONV ** -0.5),
        "ffn_w_down": nrm(ks[22], (DEPTH, D_FF, D_MODEL), D_FF ** -0.5 * DN_BETA),
        "ple_w_proj": nrm(ks[23], (DEPTH, PLE_DIM, D_MODEL), PLE_DIM ** -0.5 * DN_BETA),
        "ple_w_gate": nrm(ks[24], (DEPTH, D_MODEL, D_MODEL), D_MODEL ** -0.5),
    }


def reference(x_prompt, x_sample, cache_nsa_kv, state_nsa_win, state_gdn, state_gdn_conv, state_ffn_conv,
              page_table, p_prompt, p_sample, w_in, nsa_pe, nsa_phi, gdn_conv_w, gdn_A_log, gdn_dt_bias,
              gdn_norm_w, w_out, ln_g, ln_b, ffn_w_up, ffn_conv_w, ffn_w_down, ple_w_proj, ple_w_gate):
    dec_b, n_pages = page_table.shape
    past_len = n_pages * cache_nsa_kv.shape[2]
    xp, xs = x_prompt, x_sample
    st_p, st_s = [], []
    for l in range(DEPTH):
        prm = (w_in[l], nsa_pe[l], nsa_phi[l], gdn_conv_w[l], gdn_A_log[l], gdn_dt_bias[l], gdn_norm_w[l],
               w_out[l], ln_g[l], ln_b[l], ffn_w_up[l], ffn_conv_w[l], ffn_w_down[l], ple_w_proj[l], ple_w_gate[l])
        xp, sp = trunk_layer(xp, p_prompt[l], None, None, None, None, None, *prm)
        kv_past = cache_nsa_kv[l][page_table].reshape(dec_b, past_len, 4, N_NSA_KV, HEAD_DIM)
        xs, ss = trunk_layer(xs, p_sample[l], kv_past, state_nsa_win[l], state_gdn[l], state_gdn_conv[l],
                             state_ffn_conv[l], *prm)
        st_p.append(sp)
        st_s.append(ss)
    kv_rows_prompt = jnp.stack([s[0] for s in st_p])
    kv_rows_sample = jnp.stack([s[0] for s in st_s])
    win_prompt = jnp.stack([s[1] for s in st_p])
    win_sample = jnp.stack([s[1] for s in st_s])
    gdn_state_prompt = jnp.stack([s[2] for s in st_p])
    gdn_state_sample = jnp.stack([s[2] for s in st_s])
    gdn_conv_prompt = jnp.stack([s[3] for s in st_p])
    gdn_conv_sample = jnp.stack([s[3] for s in st_s])
    ffn_conv_prompt = jnp.stack([s[4] for s in st_p])
    ffn_conv_sample = jnp.stack([s[4] for s in st_s])
    return (xp, xs, kv_rows_prompt, kv_rows_sample, win_prompt, win_sample, gdn_state_prompt, gdn_state_sample,
            gdn_conv_prompt, gdn_conv_sample, ffn_conv_prompt, ffn_conv_sample)
```

```python
import functools

import jax
import jax.numpy as jnp
from jax import lax
from jax.experimental import pallas as pl
from jax.experimental.pallas import tpu as pltpu

F32 = jnp.float32
BF16 = jnp.bfloat16

D_MODEL = 1024
DEPTH = 2
HEAD_DIM = 64
N_HEADS = 8
N_KV = 2
GROUP = 4
N_GDN = 8
KV_W = N_KV * HEAD_DIM
GDN_W = N_GDN * HEAD_DIM
GDN_QKV = 3 * GDN_W
CMP_BLOCK = 32
SLC_BLOCK = 64
TOP_N = 16
N_LOCAL = 2
WINDOW = 512
Q_BLOCK = 128
GDN_CHUNK = 64
GDN_CONV = 4
FFN_CONV = 3
D_FF = 2816
PLE_DIM = 256
LN_EPS = 1e-5
RMS_EPS = 1e-6
DN_ALPHA = (2 * DEPTH) ** 0.25
NEG = -1e30
BIG = 1e30

C_KV = 512
C_WIN = 1024
C_GATE = 1280
C_GQKV = 1304
C_GA = 2840
C_GB = 2848
C_GZ = 2856
IN_WIDTH = 3368
SM_W = 128
SM_GA = 24
SM_GB = 32

LANES = 128
SUBLANES = 8
VMEM_CAP = 56 * 1024 * 1024
FF_CHUNK = 256
SLC_TILE = 512

_NT = (((1,), (1,)), ((), ()))
_TN = (((0,), (0,)), ((), ()))


def _dot(a, b):
    return jnp.dot(a.astype(BF16), b.astype(BF16), preferred_element_type=F32)


def _dot_nt(a, b):
    return lax.dot_general(a.astype(BF16), b.astype(BF16), _NT, preferred_element_type=F32)


def _dot_tn(a, b):
    return lax.dot_general(a.astype(BF16), b.astype(BF16), _TN, preferred_element_type=F32)


def _dot_f32(a, b):
    return jnp.dot(a, b, precision=lax.Precision.HIGHEST, preferred_element_type=F32)


def _split3(a):
    hi = a.astype(BF16)
    r = a - hi.astype(F32)
    mid = r.astype(BF16)
    lo = (r - mid.astype(F32)).astype(BF16)
    return hi, mid, lo


def _dot_sel(a, sel):
    hi, mid, lo = _split3(a)
    sel = sel.astype(BF16)
    d = lambda u: jnp.dot(u, sel, preferred_element_type=F32)
    return d(hi) + d(mid) + d(lo)


def _dot_tn_sel(sel, a):
    hi, mid, lo = _split3(a)
    sel = sel.astype(BF16)
    d = lambda u: lax.dot_general(sel, u, _TN, preferred_element_type=F32)
    return d(hi) + d(mid) + d(lo)


def _sigmoid(x):
    return 1.0 / (1.0 + jnp.exp(-x))


def _silu(x):
    return x * _sigmoid(x)


def _softplus(x):
    return jnp.maximum(x, 0.0) + jnp.log1p(jnp.exp(-jnp.abs(x)))


def _layer_norm(x, g, b):
    mu = jnp.mean(x, axis=-1, keepdims=True)
    xc = x - mu
    var = jnp.mean(xc * xc, axis=-1, keepdims=True)
    return xc * lax.rsqrt(var + LN_EPS) * g + b


def _gelu(x):
    return 0.5 * x * (1.0 + lax.erf(x * (0.5 ** 0.5)))


def _head_indicator(width, heads):
    r = lax.broadcasted_iota(jnp.int32, (width, heads), 0) // HEAD_DIM
    c = lax.broadcasted_iota(jnp.int32, (width, heads), 1)
    return jnp.where(r == c, 1.0, 0.0).astype(BF16)


def _head_indicator_t(heads, width):
    r = lax.broadcasted_iota(jnp.int32, (heads, width), 0)
    c = lax.broadcasted_iota(jnp.int32, (heads, width), 1) // HEAD_DIM
    return jnp.where(r == c, 1.0, 0.0).astype(BF16)


def _params(sem, est_bytes):
    limit = int(min(max(2 * est_bytes, 32 * 1024 * 1024), VMEM_CAP))
    return pltpu.CompilerParams(dimension_semantics=sem, vmem_limit_bytes=limit)


def _resident(shape):
    nd = len(shape)
    return pl.BlockSpec(shape, lambda *_: (0,) * nd, pipeline_mode=pl.Buffered(1))


_PROJ_WIDTHS = (512, 512, 256, GDN_QKV, GDN_W, SM_W)


def _proj_in_body(x_ref, w_ref, *out_refs):
    xb = x_ref[...].astype(BF16)
    off = 0
    for ref in out_refs:
        n = ref.shape[-1]
        ref[...] = jnp.dot(xb, w_ref[:, off:off + n], preferred_element_type=F32)
        off += n


def _proj_in(x2, w_cat, tm):
    m = x2.shape[0]
    tot = sum(_PROJ_WIDTHS)
    est = 2 * tm * D_MODEL * 4 + D_MODEL * tot * 2 + 2 * tm * tot * 4
    return pl.pallas_call(
        _proj_in_body,
        grid=(m // tm,),
        in_specs=[pl.BlockSpec((tm, D_MODEL), lambda i: (i, 0)), _resident((D_MODEL, tot))],
        out_specs=[pl.BlockSpec((tm, n), lambda i: (i, 0)) for n in _PROJ_WIDTHS],
        out_shape=[jax.ShapeDtypeStruct((m, n), F32) for n in _PROJ_WIDTHS],
        compiler_params=_params(("parallel",), est),
        name="proj_in",
    )(x2, w_cat)


def _compress_rows(rows, pe, phi):
    nblk = rows.shape[0] // CMP_BLOCK
    mean = rows.reshape(nblk, CMP_BLOCK, KV_W).sum(axis=1) * (1.0 / CMP_BLOCK)
    pem = jnp.mean(pe, axis=0, keepdims=True)
    outs = []
    for n in range(N_KV):
        outs.append(_dot_f32(mean[:, n * HEAD_DIM:(n + 1) * HEAD_DIM] + pem, phi))
    return jnp.concatenate(outs, axis=1)


def _compress_body(kv_ref, pe_ref, phi_ref, kc_ref, vc_ref):
    kc_ref[...] = _compress_rows(kv_ref[:, 0:KV_W], pe_ref[0], phi_ref[0])
    vc_ref[...] = _compress_rows(kv_ref[:, KV_W:2 * KV_W], pe_ref[1], phi_ref[1])


def _compress(kv3, pe, phi):
    b, t, _ = kv3.shape
    nc = t // CMP_BLOCK
    est = 2 * t * 256 * 4 + 4 * nc * KV_W * 4
    return pl.pallas_call(
        _compress_body,
        grid=(b,),
        in_specs=[pl.BlockSpec((None, t, 2 * KV_W), lambda i: (i, 0, 0)),
                  _resident((2, CMP_BLOCK, HEAD_DIM)), _resident((2, HEAD_DIM, HEAD_DIM))],
        out_specs=[pl.BlockSpec((None, nc, KV_W), lambda i: (i, 0, 0))] * 2,
        out_shape=[jax.ShapeDtypeStruct((b, nc, KV_W), F32)] * 2,
        compiler_params=_params(("parallel",), est),
        name="nsa_compress",
    )(kv3, pe, phi)


def _softmax_rows(s):
    m = jnp.max(s, axis=-1, keepdims=True)
    e = jnp.exp(s - m)
    return e / jnp.sum(e, axis=-1, keepdims=True)


def _nsa_prompt_body(q_ref, slc_ref, kc_ref, vc_ref, win_ref, sm_ref, o_ref, imp_ref, *, seq):
    j = pl.program_id(1)
    nc = seq // CMP_BLOCK
    ns = seq // SLC_BLOCK
    tq = Q_BLOCK
    q = q_ref[...] * (HEAD_DIM ** -0.5)
    gates = _sigmoid(sm_ref[:, 0:3 * N_HEADS])
    qpos_c = j * tq + lax.broadcasted_iota(jnp.int32, (tq, 1), 0)
    qpos4_i = jnp.concatenate([qpos_c] * GROUP, axis=0)
    qpos4 = qpos4_i.astype(F32)
    qpos_r = j * tq + lax.broadcasted_iota(jnp.int32, (1, tq), 1)
    outs = [None] * N_HEADS
    for n in range(N_KV):
        hs = slice(n * HEAD_DIM, (n + 1) * HEAD_DIM)
        vs_ = slice(KV_W + n * HEAD_DIM, KV_W + (n + 1) * HEAD_DIM)
        qs = jnp.concatenate(
            [q[:, (GROUP * n + g) * HEAD_DIM:(GROUP * n + g + 1) * HEAD_DIM] for g in range(GROUP)],
            axis=0).astype(BF16)
        slope = jnp.concatenate(
            [jnp.full((tq, 1), 2.0 ** -(GROUP * n + g + 1), F32) for g in range(GROUP)], axis=0)

        cidx = lax.broadcasted_iota(jnp.int32, (1, nc), 1)
        c_mid = cidx.astype(F32) * CMP_BLOCK + (CMP_BLOCK - 1) / 2
        c_vis = (cidx * CMP_BLOCK + (CMP_BLOCK - 1)) <= qpos4_i
        s = _dot_nt(qs, kc_ref[:, hs]) - slope * (qpos4 - c_mid)
        p = jnp.where(c_vis, _softmax_rows(jnp.where(c_vis, s, NEG)), 0.0)
        o_cmp = _dot(p, vc_ref[:, hs])

        imp = p[0:tq] + p[tq:2 * tq] + p[2 * tq:3 * tq] + p[3 * tq:4 * tq]
        imp_ref[...] = imp.T
        impb = imp_ref[pl.ds(0, ns, stride=2), :] + imp_ref[pl.ds(1, ns, stride=2), :]
        blk = lax.broadcasted_iota(jnp.int32, (ns, tq), 0)
        cur = qpos_r // SLC_BLOCK
        future = blk > cur
        forced = (blk == 0) | (((cur - blk) < N_LOCAL) & jnp.logical_not(future))
        score = jnp.where(future, -BIG, jnp.where(forced, BIG, impb))
        rank = jnp.zeros((ns, tq), F32)
        for bp in range(ns):
            row = score[bp:bp + 1, :]
            rank = rank + jnp.where(blk > bp, jnp.where(row >= score, 1.0, 0.0), jnp.where(row > score, 1.0, 0.0))
        sel_t = jnp.where(rank < float(min(TOP_N, ns)), 1.0, 0.0)
        sel = sel_t.T.astype(BF16)
        sel4 = jnp.concatenate([sel] * GROUP, axis=0)

        def slc_tile(kt, carry):
            m, l, acc = carry
            k0 = pl.multiple_of(kt * SLC_TILE, SLC_TILE)
            kpos = k0 + lax.broadcasted_iota(jnp.int32, (1, SLC_TILE), 1)
            member = jnp.where(
                (k0 + lax.broadcasted_iota(jnp.int32, (ns, SLC_TILE), 1)) // SLC_BLOCK
                == lax.broadcasted_iota(jnp.int32, (ns, SLC_TILE), 0), 1.0, 0.0).astype(BF16)
            chosen = jnp.dot(sel4, member, preferred_element_type=F32)
            st = _dot_nt(qs, slc_ref[pl.ds(k0, SLC_TILE), hs]) - slope * (qpos4 - kpos.astype(F32))
            ok = (chosen > 0.5) & (kpos <= qpos4_i)
            st = jnp.where(ok, st, NEG)
            m2 = jnp.maximum(m, jnp.max(st, axis=-1, keepdims=True))
            a = jnp.exp(m - m2)
            e = jnp.exp(st - m2)
            l2 = a * l + jnp.sum(e, axis=-1, keepdims=True)
            acc2 = a * acc + _dot(e, slc_ref[pl.ds(k0, SLC_TILE), vs_])
            return m2, l2, acc2

        n_tiles = (j * tq + tq - 1) // SLC_TILE + 1
        init = (jnp.full((GROUP * tq, 1), NEG, F32), jnp.zeros((GROUP * tq, 1), F32),
                jnp.zeros((GROUP * tq, HEAD_DIM), F32))
        _, l_s, acc_s = lax.fori_loop(0, n_tiles, slc_tile, init)
        o_slc = acc_s / l_s

        nwb = WINDOW // tq + 1
        s_parts, v_parts = [], []
        for i in range(nwb):
            kb = j - (nwb - 1) + i
            r0 = pl.multiple_of(jnp.maximum(kb, 0) * tq, tq)
            kpos = kb * tq + lax.broadcasted_iota(jnp.int32, (1, tq), 1)
            dist = qpos4_i - kpos
            ok = (dist >= 0) & (dist < WINDOW) & (kpos >= 0)
            sw = _dot_nt(qs, win_ref[pl.ds(r0, tq), hs]) - slope * dist.astype(F32)
            s_parts.append(jnp.where(ok, sw, NEG))
            v_parts.append(win_ref[pl.ds(r0, tq), vs_])
        pw = _softmax_rows(jnp.concatenate(s_parts, axis=1))
        o_win = _dot(pw[:, 0:tq], v_parts[0])
        for i in range(1, nwb):
            o_win = o_win + _dot(pw[:, i * tq:(i + 1) * tq], v_parts[i])

        for g in range(GROUP):
            h = GROUP * n + g
            rows = slice(g * tq, (g + 1) * tq)
            outs[h] = (gates[:, 3 * h:3 * h + 1] * o_cmp[rows] + gates[:, 3 * h + 1:3 * h + 2] * o_slc[rows]
                       + gates[:, 3 * h + 2:3 * h + 3] * o_win[rows])
    o_ref[...] = jnp.concatenate(outs, axis=1)


def _nsa_prompt(q3, kv3, kc, vc, win3, sm3):
    b, t, _ = q3.shape
    assert t % SLC_TILE == 0 and t % Q_BLOCK == 0
    nc = t // CMP_BLOCK
    est = 2 * (2 * t * 256 * 4) + 4 * nc * KV_W * 4 + 8 * Q_BLOCK * 512 * 4 + 16 * 512 * 640 * 4
    return pl.pallas_call(
        functools.partial(_nsa_prompt_body, seq=t),
        grid=(b, t // Q_BLOCK),
        in_specs=[
            pl.BlockSpec((None, Q_BLOCK, 512), lambda i, j: (i, j, 0)),
            pl.BlockSpec((None, t, 2 * KV_W), lambda i, j: (i, 0, 1)),
            pl.BlockSpec((None, nc, KV_W), lambda i, j: (i, 0, 0)),
            pl.BlockSpec((None, nc, KV_W), lambda i, j: (i, 0, 0)),
            pl.BlockSpec((None, t, 2 * KV_W), lambda i, j: (i, 0, 0)),
            pl.BlockSpec((None, Q_BLOCK, SM_W), lambda i, j: (i, j, 0)),
        ],
        out_specs=pl.BlockSpec((None, Q_BLOCK, 512), lambda i, j: (i, j, 0)),
        out_shape=jax.ShapeDtypeStruct((b, t, 512), F32),
        scratch_shapes=[pltpu.VMEM((nc, Q_BLOCK), F32)],
        compiler_params=_params(("parallel", "arbitrary"), est),
        name="nsa_prompt",
    )(q3, kv3, kc, vc, win3, sm3)


def _solve_unit_lower(nmat, rhs):
    c = nmat.shape[0]
    ii = lax.broadcasted_iota(jnp.int32, (c, c), 0)
    jj = lax.broadcasted_iota(jnp.int32, (c, c), 1)
    tmat = jnp.where(ii == jj, 1.0, 0.0) - nmat
    pw = _dot(nmat, nmat)
    steps = c.bit_length() - 2
    for _ in range(steps - 1):
        both = _dot(jnp.concatenate([tmat, pw], axis=0), pw)
        tmat = tmat + both[0:c]
        pw = both[c:2 * c]
    tmat = tmat + _dot(tmat, pw)
    return _dot(tmat, rhs)


def _l2_scale(x, ind, ind_t):
    ssq = _dot_sel(x * x, ind)
    return _dot_sel(lax.rsqrt(ssq + RMS_EPS), ind_t)


def _gdn_prompt_body(gqkv_ref, gz_ref, sm_ref, cw_ref, alog_r_ref, dtb_r_ref, alog_c_ref, dtb_c_ref, nw_ref,
                     o_ref, s_ref, xbuf):
    t = pl.program_id(1)
    c = GDN_CHUNK
    pad = SUBLANES

    @pl.when(t == 0)
    def _():
        xbuf[0:pad, :] = jnp.zeros((pad, GDN_QKV), F32)
        s_ref[...] = jnp.zeros(s_ref.shape, F32)

    xbuf[pad:pad + c, :] = gqkv_ref[...]
    y = cw_ref[GDN_CONV - 1:GDN_CONV, :] * xbuf[pad:pad + c, :]
    for tap in range(GDN_CONV - 1):
        y = y + cw_ref[tap:tap + 1, :] * xbuf[pad - (GDN_CONV - 1) + tap:pad - (GDN_CONV - 1) + tap + c, :]
    xbuf[0:pad, :] = xbuf[c:c + pad, :]
    act = _silu(y)
    ind = _head_indicator(GDN_W, N_GDN)
    ind_t = _head_indicator_t(N_GDN, GDN_W)
    qraw = act[:, 0:GDN_W]
    kraw = act[:, GDN_W:2 * GDN_W]
    v = act[:, 2 * GDN_W:3 * GDN_W]
    qn = qraw * _l2_scale(qraw, ind, ind_t) * (HEAD_DIM ** -0.5)
    kn = kraw * _l2_scale(kraw, ind, ind_t)

    sm = sm_ref[...]
    sm_t = sm.T
    g_col = -jnp.exp(alog_r_ref[...]) * _softplus(sm[:, SM_GA:SM_GA + N_GDN] + dtb_r_ref[...])
    g_row = -jnp.exp(alog_c_ref[...]) * _softplus(sm_t[SM_GA:SM_GA + N_GDN, :] + dtb_c_ref[...])
    beta = _sigmoid(sm[:, SM_GB:SM_GB + N_GDN])
    ii = lax.broadcasted_iota(jnp.int32, (c, c), 0)
    jj = lax.broadcasted_iota(jnp.int32, (c, c), 1)
    gc_col = _dot_f32(jnp.where(ii >= jj, 1.0, 0.0), g_col)
    gc_row = _dot_f32(g_row, jnp.where(ii <= jj, 1.0, 0.0))
    egc = jnp.exp(gc_col)

    outs = []
    for h in range(N_GDN):
        hs = slice(h * HEAD_DIM, (h + 1) * HEAD_DIM)
        qh, kh, vh = qn[:, hs], kn[:, hs], v[:, hs]
        gcc = gc_col[:, h:h + 1]
        dec = jnp.exp(jnp.where(ii >= jj, gcc - gc_row[h:h + 1, :], NEG))
        bcol = beta[:, h:h + 1]
        kb = kh * bcol
        nmat = jnp.where(ii > jj, _dot_nt(kb, kh) * dec, 0.0)
        rhs = jnp.concatenate([vh * bcol, kb * egc[:, h:h + 1]], axis=1)
        x = _solve_unit_lower(nmat, rhs)
        val, kcd = x[:, 0:HEAD_DIM], x[:, HEAD_DIM:2 * HEAD_DIM]
        inner = _dot_nt(qh, kh) * dec
        gl = gc_row[h:h + 1, c - 1:c]
        s_h = s_ref[h]
        vn = val - _dot(kcd, s_h)
        outs.append(_dot(qh * egc[:, h:h + 1], s_h) + _dot(inner, vn))
        s_ref[h] = s_h * jnp.exp(gl) + _dot_tn(kh * jnp.exp(gl - gcc), vn)
    o = jnp.concatenate(outs, axis=1)
    rms = _dot_sel(lax.rsqrt(_dot_sel(o * o, ind) * (1.0 / HEAD_DIM) + RMS_EPS), ind_t)
    o_ref[...] = o * rms * nw_ref[...] * _silu(gz_ref[...])


def _gdn_prompt(gqkv3, gz3, sm3, cw, alog, dtb, nw_t):
    b, t, _ = gqkv3.shape
    assert t % GDN_CHUNK == 0
    c = GDN_CHUNK
    est = 4 * c * GDN_QKV * 4 + 64 * c * 512 * 4
    return pl.pallas_call(
        _gdn_prompt_body,
        grid=(b, t // c),
        in_specs=[
            pl.BlockSpec((None, c, GDN_QKV), lambda i, j: (i, j, 0)),
            pl.BlockSpec((None, c, GDN_W), lambda i, j: (i, j, 0)),
            pl.BlockSpec((None, c, SM_W), lambda i, j: (i, j, 0)),
            _resident((GDN_CONV, GDN_QKV)), _resident((1, N_GDN)), _resident((1, N_GDN)),
            _resident((N_GDN, 1)), _resident((N_GDN, 1)), _resident((1, GDN_W)),
        ],
        out_specs=[pl.BlockSpec((None, c, GDN_W), lambda i, j: (i, j, 0)),
                   pl.BlockSpec((None, N_GDN, HEAD_DIM, HEAD_DIM), lambda i, j: (i, 0, 0, 0))],
        out_shape=[jax.ShapeDtypeStruct((b, t, GDN_W), F32),
                   jax.ShapeDtypeStruct((b, N_GDN, HEAD_DIM, HEAD_DIM), F32)],
        scratch_shapes=[pltpu.VMEM((c + SUBLANES, GDN_QKV), F32)],
        compiler_params=_params(("parallel", "arbitrary"), est),
        name="gdn_prompt",
    )(gqkv3, gz3, sm3, cw, alog.reshape(1, N_GDN), dtb.reshape(1, N_GDN),
      alog.reshape(N_GDN, 1), dtb.reshape(N_GDN, 1), nw_t)


def _post_tail(x1, ffn, p, lng_ref, lnb_ref, wg_ref, wp_ref):
    x2 = _layer_norm(DN_ALPHA * x1 + ffn, lng_ref[1:2, :], lnb_ref[1:2, :])
    ple = _sigmoid(jnp.dot(x2.astype(BF16), wg_ref[...], preferred_element_type=F32)) \
        * jnp.dot(p.astype(BF16), wp_ref[...], preferred_element_type=F32)
    return _layer_norm(DN_ALPHA * x2 + ple, lng_ref[2:3, :], lnb_ref[2:3, :])


def _post_head(onsa_ref, og_ref, x_ref, wo_ref, lng_ref, lnb_ref):
    attn = jnp.dot(onsa_ref[...].astype(BF16), wo_ref[0:512, :], preferred_element_type=F32) \
        + jnp.dot(og_ref[...].astype(BF16), wo_ref[512:1024, :], preferred_element_type=F32)
    return _layer_norm(DN_ALPHA * x_ref[...] + attn, lng_ref[0:1, :], lnb_ref[0:1, :])


def _post_prompt_body(onsa_ref, og_ref, x_ref, p_ref, wo_ref, lng_ref, lnb_ref, wu_ref, cw_ref, wd_ref,
                      wp_ref, wg_ref, y_ref, fbuf_ref, upbuf):
    t = pl.program_id(1)
    tm = x_ref.shape[0]
    pad = SUBLANES

    @pl.when(t == 0)
    def _():
        upbuf[0:pad, :] = jnp.zeros((pad, D_FF), F32)

    x1 = _post_head(onsa_ref, og_ref, x_ref, wo_ref, lng_ref, lnb_ref)
    x1b = x1.astype(BF16)
    ffn = jnp.zeros((tm, D_MODEL), F32)
    for cc in range(D_FF // FF_CHUNK):
        cs = slice(cc * FF_CHUNK, (cc + 1) * FF_CHUNK)
        up_g = jnp.dot(x1b, wu_ref[:, cs], preferred_element_type=F32)
        up_v = jnp.dot(x1b, wu_ref[:, D_FF + cc * FF_CHUNK:D_FF + (cc + 1) * FF_CHUNK], preferred_element_type=F32)
        upbuf[pad:pad + tm, cs] = up_g
        hg = cw_ref[FFN_CONV - 1:FFN_CONV, cs] * up_g
        for tap in range(FFN_CONV - 1):
            o0 = pad - (FFN_CONV - 1) + tap
            hg = hg + cw_ref[tap:tap + 1, cs] * upbuf[o0:o0 + tm, cs]
        ffn = ffn + jnp.dot((_gelu(hg) * up_v).astype(BF16), wd_ref[cs, :], preferred_element_type=F32)
    fbuf_ref[...] = upbuf[pad + tm - (FFN_CONV - 1):pad + tm, :]
    upbuf[0:pad, :] = upbuf[tm:tm + pad, :]
    y_ref[...] = _post_tail(x1, ffn, p_ref[...], lng_ref, lnb_ref, wg_ref, wp_ref)


def _post_decode_body(onsa_ref, og_ref, x_ref, p_ref, fst_ref, wo_ref, lng_ref, lnb_ref, wu_ref, cw_ref, wd_ref,
                      wp_ref, wg_ref, y_ref, fnew_ref):
    tm = x_ref.shape[0]
    x1 = _post_head(onsa_ref, og_ref, x_ref, wo_ref, lng_ref, lnb_ref)
    x1b = x1.astype(BF16)
    ffn = jnp.zeros((tm, D_MODEL), F32)
    for cc in range(D_FF // FF_CHUNK):
        cs = slice(cc * FF_CHUNK, (cc + 1) * FF_CHUNK)
        cs1 = slice(D_FF + cc * FF_CHUNK, D_FF + (cc + 1) * FF_CHUNK)
        up_g = jnp.dot(x1b, wu_ref[:, cs], preferred_element_type=F32)
        up_v = jnp.dot(x1b, wu_ref[:, cs1], preferred_element_type=F32)
        hg = cw_ref[0:1, cs] * fst_ref[:, cs] + cw_ref[1:2, cs] * fst_ref[:, cs1] + cw_ref[2:3, cs] * up_g
        fnew_ref[:, cs] = fst_ref[:, cs1]
        fnew_ref[:, cs1] = up_g
        ffn = ffn + jnp.dot((_gelu(hg) * up_v).astype(BF16), wd_ref[cs, :], preferred_element_type=F32)
    y_ref[...] = _post_tail(x1, ffn, p_ref[...], lng_ref, lnb_ref, wg_ref, wp_ref)


def _post_weight_specs():
    return [_resident((D_MODEL, D_MODEL)), _resident((3, D_MODEL)), _resident((3, D_MODEL)),
            _resident((D_MODEL, 2 * D_FF)), _resident((FFN_CONV, D_FF)), _resident((D_FF, D_MODEL)),
            _resident((PLE_DIM, D_MODEL)), _resident((D_MODEL, D_MODEL))]


_POST_WEIGHT_BYTES = 2 * (2 * D_MODEL * D_MODEL + 3 * D_MODEL * D_FF + PLE_DIM * D_MODEL) + 4 * 9 * D_FF


def _post_prompt(onsa3, og3, x3, p3, p_base, wts, tm):
    b, t, _ = x3.shape
    row = lambda n: pl.BlockSpec((None, tm, n), lambda i, j: (i, j, 0))
    p_spec = pl.BlockSpec((None, tm, PLE_DIM), lambda i, j: (i + p_base, j, 0))
    est = _POST_WEIGHT_BYTES + 2 * tm * (512 + 512 + 1024 + 256 + 1024) * 4 + (tm + 8) * D_FF * 4 + 12 * tm * 1024 * 4
    return pl.pallas_call(
        _post_prompt_body,
        grid=(b, t // tm),
        in_specs=[row(512), row(512), row(D_MODEL), p_spec] + _post_weight_specs(),
        out_specs=[row(D_MODEL), pl.BlockSpec((None, FFN_CONV - 1, D_FF), lambda i, j: (i, 0, 0))],
        out_shape=[jax.ShapeDtypeStruct((b, t, D_MODEL), F32),
                   jax.ShapeDtypeStruct((b, FFN_CONV - 1, D_FF), F32)],
        scratch_shapes=[pltpu.VMEM((tm + SUBLANES, D_FF), F32)],
        compiler_params=_params(("parallel", "arbitrary"), est),
        name="post_prompt",
    )(onsa3, og3, x3, p3, *wts)


def _post_decode(onsa2, og2, x2, p2, fst2, wts):
    m = x2.shape[0]
    full = lambda n: pl.BlockSpec((m, n), lambda i: (0, 0))
    est = _POST_WEIGHT_BYTES + 2 * m * (512 + 512 + 1024 + 256 + 1024 + 4 * D_FF) * 4 + 12 * m * 1024 * 4
    return pl.pallas_call(
        _post_decode_body,
        grid=(1,),
        in_specs=[full(512), full(512), full(D_MODEL), full(PLE_DIM), full(2 * D_FF)] + _post_weight_specs(),
        out_specs=[full(D_MODEL), full(2 * D_FF)],
        out_shape=[jax.ShapeDtypeStruct((m, D_MODEL), F32), jax.ShapeDtypeStruct((m, 2 * D_FF), F32)],
        compiler_params=_params(("arbitrary",), est),
        name="post_decode",
    )(onsa2, og2, x2, p2, fst2, *wts)


def _nsa_decode_body(pt_ref, *refs, n_pages, page):
    page_refs = refs[:n_pages]
    (q_ref, kvn_ref, winn_ref, sm_ref, wbuf_ref, pe_ref, phi_ref, o_ref, wout_ref) = refs[n_pages:]
    del pt_ref
    past = n_pages * page
    ncv = past // CMP_BLOCK
    nsb = past // SLC_BLOCK
    assert ncv <= LANES and ncv % 2 == 0
    q = q_ref[...] * (HEAD_DIM ** -0.5)
    zero = jnp.zeros((1, HEAD_DIM), F32)
    rows = []
    for h in range(N_HEADS):
        piece = q[:, h * HEAD_DIM:(h + 1) * HEAD_DIM]
        rows.append(jnp.concatenate([piece, zero] if h < GROUP else [zero, piece], axis=1))
    qbd = jnp.concatenate(rows, axis=0)
    qbd_b = qbd.astype(BF16)
    hrow = lax.broadcasted_iota(jnp.int32, (N_HEADS, 1), 0)
    slope = jnp.concatenate([jnp.full((1, 1), 2.0 ** -(h + 1), F32) for h in range(N_HEADS)], axis=0)

    blocks_per_page = page // CMP_BLOCK
    mk, mv = [], []
    for pr in page_refs:
        mk.append(pr[:, 0:KV_W].reshape(blocks_per_page, CMP_BLOCK, KV_W).sum(axis=1))
        mv.append(pr[:, KV_W:2 * KV_W].reshape(blocks_per_page, CMP_BLOCK, KV_W).sum(axis=1))
    mk = jnp.concatenate(mk, axis=0) * (1.0 / CMP_BLOCK)
    mv = jnp.concatenate(mv, axis=0) * (1.0 / CMP_BLOCK)
    pek = jnp.mean(pe_ref[0], axis=0, keepdims=True)
    pev = jnp.mean(pe_ref[1], axis=0, keepdims=True)
    kc = jnp.concatenate([_dot_f32(mk[:, n * HEAD_DIM:(n + 1) * HEAD_DIM] + pek, phi_ref[0]) for n in range(N_KV)], axis=1)
    vc = jnp.concatenate([_dot_f32(mv[:, n * HEAD_DIM:(n + 1) * HEAD_DIM] + pev, phi_ref[1]) for n in range(N_KV)], axis=1)

    cidx = lax.broadcasted_iota(jnp.int32, (1, ncv), 1)
    c_mid = cidx.astype(F32) * CMP_BLOCK + (CMP_BLOCK - 1) / 2
    s = _dot_nt(qbd_b, kc) - slope * (float(past) - c_mid)
    p = _softmax_rows(s)
    o_cmp = _dot(p, vc)

    p_l = p if ncv == LANES else jnp.concatenate([p, jnp.zeros((N_HEADS, LANES - ncv), F32)], axis=1)
    pair = p_l + pltpu.roll(p_l, LANES - 1, 1)
    pooled = jnp.concatenate([jnp.sum(pair[n * GROUP:(n + 1) * GROUP], axis=0, keepdims=True) for n in range(N_KV)],
                             axis=0)
    lane = lax.broadcasted_iota(jnp.int32, (N_KV, LANES), 1)
    blk = lane // 2
    is_blk = ((lane % 2) == 0) & (lane < ncv)
    is_cur = lane == ncv
    forced = is_cur | (is_blk & ((blk == 0) | ((nsb - blk) < N_LOCAL)))
    score = jnp.where(forced, BIG, jnp.where(is_blk, pooled, -BIG))
    rank = jnp.zeros((N_KV, LANES), F32)
    cand = [2 * b for b in range(nsb)] + [ncv]
    for lc in cand:
        col = score[:, lc:lc + 1]
        rank = rank + jnp.where(lane > lc, jnp.where(col >= score, 1.0, 0.0), jnp.where(col > score, 1.0, 0.0))
    sel = jnp.where((rank < float(min(TOP_N, nsb + 1))) & (is_blk | is_cur), 1.0, 0.0)
    sel8 = jnp.concatenate([jnp.broadcast_to(sel[n:n + 1], (GROUP, LANES)) for n in range(N_KV)], axis=0).astype(BF16)

    kvn = kvn_ref[...]
    s_new = jnp.sum(qbd * kvn[:, 2 * KV_W:3 * KV_W], axis=-1, keepdims=True)
    s_parts = []
    for i, pr in enumerate(page_refs):
        tpos = i * page + lax.broadcasted_iota(jnp.int32, (1, page), 1)
        member = jnp.where((2 * ((i * page + lax.broadcasted_iota(jnp.int32, (LANES, page), 1)) // SLC_BLOCK))
                           == lax.broadcasted_iota(jnp.int32, (LANES, page), 0), 1.0, 0.0).astype(BF16)
        chosen = jnp.dot(sel8, member, preferred_element_type=F32)
        st = _dot_nt(qbd_b, pr[:, 2 * KV_W:3 * KV_W]) - slope * (float(past) - tpos.astype(F32))
        s_parts.append(jnp.where(chosen > 0.5, st, NEG))
    s_all = jnp.concatenate(s_parts, axis=1)
    m = jnp.maximum(jnp.max(s_all, axis=-1, keepdims=True), s_new)
    e = jnp.exp(s_all - m)
    e_new = jnp.exp(s_new - m)
    acc = e_new * kvn[:, 3 * KV_W:4 * KV_W]
    for i, pr in enumerate(page_refs):
        acc = acc + _dot(e[:, i * page:(i + 1) * page], pr[:, 3 * KV_W:4 * KV_W])
    o_slc = acc / (jnp.sum(e, axis=-1, keepdims=True) + e_new)

    wb = wbuf_ref.shape[0]
    winn = winn_ref[...]
    rpos = lax.broadcasted_iota(jnp.int32, (1, wb), 1)
    dist = wb - rpos
    sw = _dot_nt(qbd_b, wbuf_ref[:, 0:KV_W]) - slope * dist.astype(F32)
    sw = jnp.where(dist < WINDOW, sw, NEG)
    sw_new = jnp.sum(qbd * winn[:, 0:KV_W], axis=-1, keepdims=True)
    mw = jnp.maximum(jnp.max(sw, axis=-1, keepdims=True), sw_new)
    ew = jnp.exp(sw - mw)
    ew_new = jnp.exp(sw_new - mw)
    o_win = (_dot(ew, wbuf_ref[:, KV_W:2 * KV_W]) + ew_new * winn[:, KV_W:2 * KV_W]) \
        / (jnp.sum(ew, axis=-1, keepdims=True) + ew_new)
    wout_ref[0:wb - 1, :] = wbuf_ref[1:wb, :]
    wout_ref[wb - 1:wb, :] = winn

    gates = _sigmoid(sm_ref[:, 0:3 * N_HEADS])
    glane = lax.broadcasted_iota(jnp.int32, (N_HEADS, 3 * N_HEADS), 1)
    gsel = lambda r: jnp.sum(jnp.where(glane == 3 * hrow + r, gates, 0.0), axis=-1, keepdims=True)
    o8 = gsel(0) * o_cmp + gsel(1) * o_slc + gsel(2) * o_win
    pieces = []
    for h in range(N_HEADS):
        n = h // GROUP
        pieces.append(o8[h:h + 1, n * HEAD_DIM:(n + 1) * HEAD_DIM])
    o_ref[...] = jnp.concatenate(pieces, axis=1)


def _nsa_decode(page_table, cache3, page_base, q3, kvn3, winn3, sm3, wbuf3, seq_base, pe, phi):
    nb, n_pages = page_table.shape
    page = cache3.shape[1]
    wb = wbuf3.shape[1]
    assert wb == WINDOW and n_pages * page >= wb
    page_specs = [pl.BlockSpec((None, page, 4 * KV_W),
                               functools.partial(lambda i, pt, k: (pt[i, k] + page_base, 0, 0), k=k))
                  for k in range(n_pages)]
    one = lambda n: pl.BlockSpec((None, 1, n), lambda i, pt: (i, 0, 0))
    grid_spec = pltpu.PrefetchScalarGridSpec(
        num_scalar_prefetch=1,
        grid=(nb,),
        in_specs=page_specs + [one(512), one(512), one(256), one(SM_W),
                               pl.BlockSpec((None, wb, 2 * KV_W), lambda i, pt: (i + seq_base, 0, 0)),
                               pl.BlockSpec((2, CMP_BLOCK, HEAD_DIM), lambda i, pt: (0, 0, 0)),
                               pl.BlockSpec((2, HEAD_DIM, HEAD_DIM), lambda i, pt: (0, 0, 0))],
        out_specs=[one(512), pl.BlockSpec((None, wb, 2 * KV_W), lambda i, pt: (i, 0, 0))],
    )
    est = 2 * n_pages * page * 512 * 4 + 4 * wb * 256 * 4 + 4 * 1024 * 1024
    return pl.pallas_call(
        functools.partial(_nsa_decode_body, n_pages=n_pages, page=page),
        grid_spec=grid_spec,
        out_shape=[jax.ShapeDtypeStruct((nb, 1, 512), F32), jax.ShapeDtypeStruct((nb, wb, 2 * KV_W), F32)],
        compiler_params=_params(("arbitrary",), est),
        name="nsa_decode",
    )(page_table, *([cache3] * n_pages), q3, kvn3, winn3, sm3, wbuf3, pe, phi)


def _gdn_dec_prep_body(gqkv_ref, cst_ref, sm_ref, cw_ref, alog_ref, dtb_ref, qn_ref, kn_ref, v_ref, db_ref, cnew_ref):
    w = GDN_QKV
    xn = gqkv_ref[...]
    y = cw_ref[GDN_CONV - 1:GDN_CONV, :] * xn
    for tap in range(GDN_CONV - 1):
        y = y + cw_ref[tap:tap + 1, :] * cst_ref[:, tap * w:(tap + 1) * w]
    for tap in range(GDN_CONV - 2):
        cnew_ref[:, tap * w:(tap + 1) * w] = cst_ref[:, (tap + 1) * w:(tap + 2) * w]
    cnew_ref[:, (GDN_CONV - 2) * w:(GDN_CONV - 1) * w] = xn
    act = _silu(y)
    ind = _head_indicator(GDN_W, N_GDN)
    ind_t = _head_indicator_t(N_GDN, GDN_W)
    qraw = act[:, 0:GDN_W]
    kraw = act[:, GDN_W:2 * GDN_W]
    qn_ref[...] = qraw * _l2_scale(qraw, ind, ind_t) * (HEAD_DIM ** -0.5)
    kn_ref[...] = kraw * _l2_scale(kraw, ind, ind_t)
    v_ref[...] = act[:, 2 * GDN_W:3 * GDN_W]
    sm = sm_ref[...]
    g = -jnp.exp(alog_ref[...]) * _softplus(sm[:, SM_GA:SM_GA + N_GDN] + dtb_ref[...])
    beta = _sigmoid(sm[:, SM_GB:SM_GB + N_GDN])
    m = sm.shape[0]
    db_ref[...] = jnp.concatenate([jnp.exp(g), beta, jnp.zeros((m, SM_W - 2 * N_GDN), F32)], axis=1)


def _gdn_dec_prep(gqkv2, cst2, sm2, cw, alog, dtb):
    m = gqkv2.shape[0]
    full = lambda n: pl.BlockSpec((m, n), lambda i: (0, 0))
    est = 2 * m * (8 * GDN_QKV + 4 * 512) * 4 + 8 * m * GDN_QKV * 4
    return pl.pallas_call(
        _gdn_dec_prep_body,
        grid=(1,),
        in_specs=[full(GDN_QKV), full((GDN_CONV - 1) * GDN_QKV), full(SM_W),
                  pl.BlockSpec((GDN_CONV, GDN_QKV), lambda i: (0, 0)),
                  pl.BlockSpec((1, N_GDN), lambda i: (0, 0)), pl.BlockSpec((1, N_GDN), lambda i: (0, 0))],
        out_specs=[full(GDN_W), full(GDN_W), full(GDN_W), full(SM_W), full((GDN_CONV - 1) * GDN_QKV)],
        out_shape=[jax.ShapeDtypeStruct((m, GDN_W), F32)] * 3 + [jax.ShapeDtypeStruct((m, SM_W), F32),
                   jax.ShapeDtypeStruct((m, (GDN_CONV - 1) * GDN_QKV), F32)],
        compiler_params=_params(("arbitrary",), est),
        name="gdn_decode_prep",
    )(gqkv2, cst2, sm2, cw, alog.reshape(1, N_GDN), dtb.reshape(1, N_GDN))


def _gdn_dec_step_body(qn_ref, kn_ref, v_ref, db_ref, gz_ref, s_ref, nw_ref, o_ref, snew_ref):
    ind_t = _head_indicator_t(N_GDN, GDN_W)
    hrow = lax.broadcasted_iota(jnp.int32, (N_GDN, SM_W), 0)
    lane = lax.broadcasted_iota(jnp.int32, (N_GDN, SM_W), 1)
    db = db_ref[...]
    a_col = jnp.sum(jnp.where(lane == hrow, db, 0.0), axis=-1, keepdims=True)
    b_col = jnp.sum(jnp.where(lane == hrow + N_GDN, db, 0.0), axis=-1, keepdims=True)
    ind_f = ind_t.astype(F32)
    k_bd = ind_f * kn_ref[...]
    q_bd = ind_f * qn_ref[...]
    v = v_ref[...]
    v8 = jnp.concatenate([v[:, h * HEAD_DIM:(h + 1) * HEAD_DIM] for h in range(N_GDN)], axis=0)
    s = s_ref[...]
    hi, mid, lo = _split3(s)
    sk = _dot(k_bd, hi) + _dot(k_bd, mid) + _dot(k_bd, lo)
    vn = b_col * (v8 - a_col * sk)
    a_rows = jnp.broadcast_to(a_col, (N_GDN, HEAD_DIM))
    s_new = _dot_tn_sel(ind_t, a_rows) * s + _dot_tn(k_bd, vn)
    snew_ref[...] = s_new
    o8 = _dot(q_bd, s_new)
    o8 = o8 * lax.rsqrt(jnp.mean(o8 * o8, axis=-1, keepdims=True) + RMS_EPS) * nw_ref[...]
    o = jnp.concatenate([o8[h:h + 1, :] for h in range(N_GDN)], axis=1)
    o_ref[...] = o * _silu(gz_ref[...])


def _gdn_dec_step(qn3, kn3, v3, db3, gz3, s3, seq_base, nw):
    nb = qn3.shape[0]
    one = lambda n: pl.BlockSpec((None, 1, n), lambda i: (i, 0, 0))
    st = pl.BlockSpec((None, GDN_W, HEAD_DIM), lambda i: (i, 0, 0))
    est = 4 * GDN_W * HEAD_DIM * 4 + 4 * 1024 * 1024
    return pl.pallas_call(
        _gdn_dec_step_body,
        grid=(nb,),
        in_specs=[one(GDN_W), one(GDN_W), one(GDN_W), one(SM_W), one(GDN_W),
                  pl.BlockSpec((None, GDN_W, HEAD_DIM), lambda i: (i + seq_base, 0, 0)),
                  pl.BlockSpec((1, HEAD_DIM), lambda i: (0, 0))],
        out_specs=[one(GDN_W), st],
        out_shape=[jax.ShapeDtypeStruct((nb, 1, GDN_W), F32), jax.ShapeDtypeStruct((nb, GDN_W, HEAD_DIM), F32)],
        compiler_params=_params(("parallel",), est),
        name="gdn_decode_step",
    )(qn3, kn3, v3, db3, gz3, s3, nw)


def _prep_w_in(w):
    small = jnp.concatenate([w[:, C_GATE:C_GQKV], w[:, C_GA:C_GB], w[:, C_GB:C_GZ],
                             jnp.zeros((D_MODEL, SM_W - (C_GQKV - C_GATE) - 2 * N_GDN), w.dtype)], axis=1)
    return jnp.concatenate([w[:, 0:C_KV], w[:, C_KV:C_WIN], w[:, C_WIN:C_GATE], w[:, C_GQKV:C_GA],
                            w[:, C_GZ:IN_WIDTH], small], axis=1).astype(BF16)


def kernel(x_prompt, x_sample, cache_nsa_kv, state_nsa_win, state_gdn, state_gdn_conv, state_ffn_conv, page_table, p_prompt, p_sample, w_in, nsa_pe, nsa_phi, gdn_conv_w, gdn_A_log, gdn_dt_bias, gdn_norm_w, w_out, ln_g, ln_b, ffn_w_up, ffn_conv_w, ffn_w_down, ple_w_proj, ple_w_gate):
    b, t, d = x_prompt.shape
    nb, ts, _ = x_sample.shape
    assert d == D_MODEL and ts == 1 and t % Q_BLOCK == 0 and t >= WINDOW
    depth = w_in.shape[0]
    n_pool, page = cache_nsa_kv.shape[1], cache_nsa_kv.shape[2]
    wb = state_nsa_win.shape[2]
    tm_p = 256
    xp, xs = x_prompt, x_sample.reshape(nb, d)
    cache_all = cache_nsa_kv.reshape(depth * n_pool, page, 4 * KV_W)
    wbuf_all = state_nsa_win.reshape(depth * nb, wb, 2 * KV_W)
    gstate_all = state_gdn.reshape(depth * nb, GDN_W, HEAD_DIM)
    p_all = p_prompt.reshape(depth * b, t, PLE_DIM)
    st_p, st_s = [], []
    for l in range(depth):
        w_cat = _prep_w_in(w_in[l])
        post_w = (w_out[l].astype(BF16), ln_g[l], ln_b[l], ffn_w_up[l].astype(BF16), ffn_conv_w[l],
                  ffn_w_down[l].astype(BF16), ple_w_proj[l].astype(BF16), ple_w_gate[l].astype(BF16))
        nw_t = jnp.tile(gdn_norm_w[l].reshape(1, HEAD_DIM), (1, N_GDN))

        q, kv, win, gqkv, gz, sm = _proj_in(xp.reshape(b * t, d), w_cat, 512)
        kv3, win3 = kv.reshape(b, t, 512), win.reshape(b, t, 256)
        sm3, gqkv3 = sm.reshape(b, t, SM_W), gqkv.reshape(b, t, GDN_QKV)
        kc, vc = _compress(kv3, nsa_pe[l], nsa_phi[l])
        o_nsa = _nsa_prompt(q.reshape(b, t, 512), kv3, kc, vc, win3, sm3)
        o_g, s_p = _gdn_prompt(gqkv3, gz.reshape(b, t, GDN_W), sm3, gdn_conv_w[l], gdn_A_log[l], gdn_dt_bias[l], nw_t)
        xp, fbuf_p = _post_prompt(o_nsa, o_g, xp, p_all, l * b, post_w, tm_p)
        st_p.append((kv3.reshape(b, t, 4, N_KV, HEAD_DIM), win3[:, t - wb:].reshape(b, wb, 2, N_KV, HEAD_DIM),
                     s_p, gqkv3[:, t - (GDN_CONV - 1):], fbuf_p))

        q, kv, win, gqkv, gz, sm = _proj_in(xs, w_cat, nb)
        o_nsa, win_new = _nsa_decode(
            page_table, cache_all, l * n_pool, q.reshape(nb, 1, 512), kv.reshape(nb, 1, 512),
            win.reshape(nb, 1, 256), sm.reshape(nb, 1, SM_W), wbuf_all, l * nb, nsa_pe[l], nsa_phi[l])
        qn, kn, v, db, cnew = _gdn_dec_prep(gqkv, state_gdn_conv[l].reshape(nb, (GDN_CONV - 1) * GDN_QKV), sm,
                                            gdn_conv_w[l], gdn_A_log[l], gdn_dt_bias[l])
        o_g, s_s = _gdn_dec_step(qn.reshape(nb, 1, GDN_W), kn.reshape(nb, 1, GDN_W), v.reshape(nb, 1, GDN_W),
                                 db.reshape(nb, 1, SM_W), gz.reshape(nb, 1, GDN_W),
                                 gstate_all, l * nb, gdn_norm_w[l].reshape(1, HEAD_DIM))
        xs, fnew = _post_decode(o_nsa.reshape(nb, 512), o_g.reshape(nb, GDN_W), xs, p_sample[l].reshape(nb, PLE_DIM),
                                state_ffn_conv[l].reshape(nb, (FFN_CONV - 1) * D_FF), post_w)
        st_s.append((kv.reshape(nb, 1, 4, N_KV, HEAD_DIM), win_new.reshape(nb, wb, 2, N_KV, HEAD_DIM),
                     s_s.reshape(nb, N_GDN, HEAD_DIM, HEAD_DIM), cnew.reshape(nb, GDN_CONV - 1, GDN_QKV),
                     fnew.reshape(nb, FFN_CONV - 1, D_FF)))

    outs = [xp, xs.reshape(nb, 1, d)]
    for k in range(5):
        outs.append(jnp.stack([s[k] for s in st_p]))
        outs.append(jnp.stack([s[k] for s in st_s]))
    return tuple(outs)
```

```python
import functools

import jax
import jax.numpy as jnp
from jax import lax
from jax.experimental import pallas as pl
from jax.experimental.pallas import tpu as pltpu

F32 = jnp.float32
BF16 = jnp.bfloat16

D_MODEL = 1024
DEPTH = 2
HEAD_DIM = 64
N_HEADS = 8
N_KV = 2
GROUP = 4
N_GDN = 8
KV_W = N_KV * HEAD_DIM
GDN_W = N_GDN * HEAD_DIM
GDN_QKV = 3 * GDN_W
CMP_BLOCK = 32
SLC_BLOCK = 64
TOP_N = 16
N_LOCAL = 2
WINDOW = 512
Q_BLOCK = 128
GDN_CHUNK = 64
GDN_CONV = 4
FFN_CONV = 3
D_FF = 2816
PLE_DIM = 256
LN_EPS = 1e-5
RMS_EPS = 1e-6
DN_ALPHA = (2 * DEPTH) ** 0.25
NEG = -1e30
BIG = 1e30

C_KV = 512
C_WIN = 1024
C_GATE = 1280
C_GQKV = 1304
C_GA = 2840
C_GB = 2848
C_GZ = 2856
IN_WIDTH = 3368
SM_W = 128
SM_GA = 24
SM_GB = 32

LANES = 128
SUBLANES = 8
VMEM_CAP = 56 * 1024 * 1024
FF_CHUNK = 256
SLC_TILE = 512

_NT = (((1,), (1,)), ((), ()))
_TN = (((0,), (0,)), ((), ()))


def _dot(a, b):
    return jnp.dot(a.astype(BF16), b.astype(BF16), preferred_element_type=F32)


def _dot_nt(a, b):
    return lax.dot_general(a.astype(BF16), b.astype(BF16), _NT, preferred_element_type=F32)


def _dot_tn(a, b):
    return lax.dot_general(a.astype(BF16), b.astype(BF16), _TN, preferred_element_type=F32)


def _dot_f32(a, b):
    return jnp.dot(a, b, precision=lax.Precision.HIGHEST, preferred_element_type=F32)


def _split3(a):
    hi = a.astype(BF16)
    r = a - hi.astype(F32)
    mid = r.astype(BF16)
    lo = (r - mid.astype(F32)).astype(BF16)
    return hi, mid, lo


def _dot_sel(a, sel):
    hi, mid, lo = _split3(a)
    sel = sel.astype(BF16)
    d = lambda u: jnp.dot(u, sel, preferred_element_type=F32)
    return d(hi) + d(mid) + d(lo)


def _dot_tn_sel(sel, a):
    hi, mid, lo = _split3(a)
    sel = sel.astype(BF16)
    d = lambda u: lax.dot_general(sel, u, _TN, preferred_element_type=F32)
    return d(hi) + d(mid) + d(lo)


def _sigmoid(x):
    return 1.0 / (1.0 + jnp.exp(-x))


def _silu(x):
    return x * _sigmoid(x)


def _softplus(x):
    return jnp.maximum(x, 0.0) + jnp.log1p(jnp.exp(-jnp.abs(x)))


def _layer_norm(x, g, b):
    mu = jnp.mean(x, axis=-1, keepdims=True)
    xc = x - mu
    var = jnp.mean(xc * xc, axis=-1, keepdims=True)
    return xc * lax.rsqrt(var + LN_EPS) * g + b


def _gelu(x):
    return 0.5 * x * (1.0 + lax.erf(x * (0.5 ** 0.5)))


def _head_indicator(width, heads):
    r = lax.broadcasted_iota(jnp.int32, (width, heads), 0) // HEAD_DIM
    c = lax.broadcasted_iota(jnp.int32, (width, heads), 1)
    return jnp.where(r == c, 1.0, 0.0).astype(BF16)


def _head_indicator_t(heads, width):
    r = lax.broadcasted_iota(jnp.int32, (heads, width), 0)
    c = lax.broadcasted_iota(jnp.int32, (heads, width), 1) // HEAD_DIM
    return jnp.where(r == c, 1.0, 0.0).astype(BF16)


def _params(sem, est_bytes):
    limit = int(min(max(2 * est_bytes, 32 * 1024 * 1024), VMEM_CAP))
    return pltpu.CompilerParams(dimension_semantics=sem, vmem_limit_bytes=limit)


def _resident(shape):
    nd = len(shape)
    return pl.BlockSpec(shape, lambda *_: (0,) * nd, pipeline_mode=pl.Buffered(1))


_PROJ_WIDTHS = (512, 512, 256, GDN_QKV, GDN_W, SM_W)


def _proj_in_body(x_ref, w_ref, *out_refs):
    xb = x_ref[...].astype(BF16)
    off = 0
    for ref in out_refs:
        n = ref.shape[-1]
        ref[...] = jnp.dot(xb, w_ref[:, off:off + n], preferred_element_type=F32)
        off += n


def _proj_in(x2, w_cat, tm):
    m = x2.shape[0]
    tot = sum(_PROJ_WIDTHS)
    est = 2 * tm * D_MODEL * 4 + D_MODEL * tot * 2 + 2 * tm * tot * 4
    return pl.pallas_call(
        _proj_in_body,
        grid=(m // tm,),
        in_specs=[pl.BlockSpec((tm, D_MODEL), lambda i: (i, 0)), _resident((D_MODEL, tot))],
        out_specs=[pl.BlockSpec((tm, n), lambda i: (i, 0)) for n in _PROJ_WIDTHS],
        out_shape=[jax.ShapeDtypeStruct((m, n), F32) for n in _PROJ_WIDTHS],
        compiler_params=_params(("parallel",), est),
        name="proj_in",
    )(x2, w_cat)


def _compress_rows(rows, pe, phi):
    nblk = rows.shape[0] // CMP_BLOCK
    mean = rows.reshape(nblk, CMP_BLOCK, KV_W).sum(axis=1) * (1.0 / CMP_BLOCK)
    pem = jnp.mean(pe, axis=0, keepdims=True)
    outs = []
    for n in range(N_KV):
        outs.append(_dot_f32(mean[:, n * HEAD_DIM:(n + 1) * HEAD_DIM] + pem, phi))
    return jnp.concatenate(outs, axis=1)


def _compress_body(kv_ref, pe_ref, phi_ref, kc_ref, vc_ref):
    kc_ref[...] = _compress_rows(kv_ref[:, 0:KV_W], pe_ref[0], phi_ref[0])
    vc_ref[...] = _compress_rows(kv_ref[:, KV_W:2 * KV_W], pe_ref[1], phi_ref[1])


def _compress(kv3, pe, phi):
    b, t, _ = kv3.shape
    nc = t // CMP_BLOCK
    est = 2 * t * 256 * 4 + 4 * nc * KV_W * 4
    return pl.pallas_call(
        _compress_body,
        grid=(b,),
        in_specs=[pl.BlockSpec((None, t, 2 * KV_W), lambda i: (i, 0, 0)),
                  _resident((2, CMP_BLOCK, HEAD_DIM)), _resident((2, HEAD_DIM, HEAD_DIM))],
        out_specs=[pl.BlockSpec((None, nc, KV_W), lambda i: (i, 0, 0))] * 2,
        out_shape=[jax.ShapeDtypeStruct((b, nc, KV_W), F32)] * 2,
        compiler_params=_params(("parallel",), est),
        name="nsa_compress",
    )(kv3, pe, phi)


def _softmax_rows(s):
    m = jnp.max(s, axis=-1, keepdims=True)
    e = jnp.exp(s - m)
    return e / jnp.sum(e, axis=-1, keepdims=True)


def _nsa_prompt_body(q_ref, slc_ref, kc_ref, vc_ref, win_ref, sm_ref, o_ref, imp_ref, *, seq):
    j = pl.program_id(1)
    nc = seq // CMP_BLOCK
    ns = seq // SLC_BLOCK
    tq = Q_BLOCK
    q = q_ref[...] * (HEAD_DIM ** -0.5)
    gates = _sigmoid(sm_ref[:, 0:3 * N_HEADS])
    qpos_c = j * tq + lax.broadcasted_iota(jnp.int32, (tq, 1), 0)
    qpos4_i = jnp.concatenate([qpos_c] * GROUP, axis=0)
    qpos4 = qpos4_i.astype(F32)
    qpos_r = j * tq + lax.broadcasted_iota(jnp.int32, (1, tq), 1)
    outs = [None] * N_HEADS
    for n in range(N_KV):
        hs = slice(n * HEAD_DIM, (n + 1) * HEAD_DIM)
        vs_ = slice(KV_W + n * HEAD_DIM, KV_W + (n + 1) * HEAD_DIM)
        qs = jnp.concatenate(
            [q[:, (GROUP * n + g) * HEAD_DIM:(GROUP * n + g + 1) * HEAD_DIM] for g in range(GROUP)],
            axis=0).astype(BF16)
        slope = jnp.concatenate(
            [jnp.full((tq, 1), 2.0 ** -(GROUP * n + g + 1), F32) for g in range(GROUP)], axis=0)

        cidx = lax.broadcasted_iota(jnp.int32, (1, nc), 1)
        c_mid = cidx.astype(F32) * CMP_BLOCK + (CMP_BLOCK - 1) / 2
        c_vis = (cidx * CMP_BLOCK + (CMP_BLOCK - 1)) <= qpos4_i
        s = _dot_nt(qs, kc_ref[:, hs]) - slope * (qpos4 - c_mid)
        p = jnp.where(c_vis, _softmax_rows(jnp.where(c_vis, s, NEG)), 0.0)
        o_cmp = _dot(p, vc_ref[:, hs])

        imp = p[0:tq] + p[tq:2 * tq] + p[2 * tq:3 * tq] + p[3 * tq:4 * tq]
        imp_ref[...] = imp.T
        impb = imp_ref[pl.ds(0, ns, stride=2), :] + imp_ref[pl.ds(1, ns, stride=2), :]
        blk = lax.broadcasted_iota(jnp.int32, (ns, tq), 0)
        cur = qpos_r // SLC_BLOCK
        future = blk > cur
        forced = (blk == 0) | (((cur - blk) < N_LOCAL) & jnp.logical_not(future))
        score = jnp.where(future, -BIG, jnp.where(forced, BIG, impb))
        rank = jnp.zeros((ns, tq), F32)
        for bp in range(ns):
            row = score[bp:bp + 1, :]
            rank = rank + jnp.where(blk > bp, jnp.where(row >= score, 1.0, 0.0), jnp.where(row > score, 1.0, 0.0))
        sel_t = jnp.where(rank < float(min(TOP_N, ns)), 1.0, 0.0)
        sel = sel_t.T.astype(BF16)
        sel4 = jnp.concatenate([sel] * GROUP, axis=0)

        def slc_tile(kt, carry):
            m, l, acc = carry
            k0 = pl.multiple_of(kt * SLC_TILE, SLC_TILE)
            kpos = k0 + lax.broadcasted_iota(jnp.int32, (1, SLC_TILE), 1)
            member = jnp.where(
                (k0 + lax.broadcasted_iota(jnp.int32, (ns, SLC_TILE), 1)) // SLC_BLOCK
                == lax.broadcasted_iota(jnp.int32, (ns, SLC_TILE), 0), 1.0, 0.0).astype(BF16)
            chosen = jnp.dot(sel4, member, preferred_element_type=F32)
            st = _dot_nt(qs, slc_ref[pl.ds(k0, SLC_TILE), hs]) - slope * (qpos4 - kpos.astype(F32))
            ok = (chosen > 0.5) & (kpos <= qpos4_i)
            st = jnp.where(ok, st, NEG)
            m2 = jnp.maximum(m, jnp.max(st, axis=-1, keepdims=True))
            a = jnp.exp(m - m2)
            e = jnp.exp(st - m2)
            l2 = a * l + jnp.sum(e, axis=-1, keepdims=True)
            acc2 = a * acc + _dot(e, slc_ref[pl.ds(k0, SLC_TILE), vs_])
            return m2, l2, acc2

        n_tiles = (j * tq + tq - 1) // SLC_TILE + 1
        init = (jnp.full((GROUP * tq, 1), NEG, F32), jnp.zeros((GROUP * tq, 1), F32),
                jnp.zeros((GROUP * tq, HEAD_DIM), F32))
        _, l_s, acc_s = lax.fori_loop(0, n_tiles, slc_tile, init)
        o_slc = acc_s / l_s

        nwb = WINDOW // tq + 1
        s_parts, v_parts = [], []
        for i in range(nwb):
            kb = j - (nwb - 1) + i
            r0 = pl.multiple_of(jnp.maximum(kb, 0) * tq, tq)
            kpos = kb * tq + lax.broadcasted_iota(jnp.int32, (1, tq), 1)
            dist = qpos4_i - kpos
            ok = (dist >= 0) & (dist < WINDOW) & (kpos >= 0)
            sw = _dot_nt(qs, win_ref[pl.ds(r0, tq), hs]) - slope * dist.astype(F32)
            s_parts.append(jnp.where(ok, sw, NEG))
            v_parts.append(win_ref[pl.ds(r0, tq), vs_])
        pw = _softmax_rows(jnp.concatenate(s_parts, axis=1))
        o_win = _dot(pw[:, 0:tq], v_parts[0])
        for i in range(1, nwb):
            o_win = o_win + _dot(pw[:, i * tq:(i + 1) * tq], v_parts[i])

        for g in range(GROUP):
            h = GROUP * n + g
            rows = slice(g * tq, (g + 1) * tq)
            outs[h] = (gates[:, 3 * h:3 * h + 1] * o_cmp[rows] + gates[:, 3 * h + 1:3 * h + 2] * o_slc[rows]
                       + gates[:, 3 * h + 2:3 * h + 3] * o_win[rows])
    o_ref[...] = jnp.concatenate(outs, axis=1)


def _nsa_prompt(q3, kv3, kc, vc, win3, sm3):
    b, t, _ = q3.shape
    assert t % SLC_TILE == 0 and t % Q_BLOCK == 0
    nc = t // CMP_BLOCK
    est = 2 * (2 * t * 256 * 4) + 4 * nc * KV_W * 4 + 8 * Q_BLOCK * 512 * 4 + 16 * 512 * 640 * 4
    return pl.pallas_call(
        functools.partial(_nsa_prompt_body, seq=t),
        grid=(b, t // Q_BLOCK),
        in_specs=[
            pl.BlockSpec((None, Q_BLOCK, 512), lambda i, j: (i, j, 0)),
            pl.BlockSpec((None, t, 2 * KV_W), lambda i, j: (i, 0, 1)),
            pl.BlockSpec((None, nc, KV_W), lambda i, j: (i, 0, 0)),
            pl.BlockSpec((None, nc, KV_W), lambda i, j: (i, 0, 0)),
            pl.BlockSpec((None, t, 2 * KV_W), lambda i, j: (i, 0, 0)),
            pl.BlockSpec((None, Q_BLOCK, SM_W), lambda i, j: (i, j, 0)),
        ],
        out_specs=pl.BlockSpec((None, Q_BLOCK, 512), lambda i, j: (i, j, 0)),
        out_shape=jax.ShapeDtypeStruct((b, t, 512), F32),
        scratch_shapes=[pltpu.VMEM((nc, Q_BLOCK), F32)],
        compiler_params=_params(("parallel", "arbitrary"), est),
        name="nsa_prompt",
    )(q3, kv3, kc, vc, win3, sm3)


_BNN = (((2,), (1,)), ((0,), (0,)))
_BNT = (((2,), (2,)), ((0,), (0,)))
_BTN = (((1,), (1,)), ((0,), (0,)))


def _bdot(a, b, dims=_BNN):
    return lax.dot_general(a.astype(BF16), b.astype(BF16), dims, preferred_element_type=F32)


def _solve_unit_lower(nmat, rhs):
    c = nmat.shape[1]
    ii = lax.broadcasted_iota(jnp.int32, (c, c), 0)
    jj = lax.broadcasted_iota(jnp.int32, (c, c), 1)
    tmat = jnp.where(ii == jj, 1.0, 0.0) - nmat
    pw = _bdot(nmat, nmat)
    steps = c.bit_length() - 2
    for _ in range(steps - 1):
        both = _bdot(jnp.concatenate([tmat, pw], axis=1), pw)
        tmat = tmat + both[:, 0:c]
        pw = both[:, c:2 * c]
    tmat = tmat + _bdot(tmat, pw)
    return _bdot(tmat, rhs)


def _l2_scale(x, ind, ind_t):
    ssq = _dot_sel(x * x, ind)
    return _dot_sel(lax.rsqrt(ssq + RMS_EPS), ind_t)


def _gdn_prompt_body(gqkv_ref, gz_ref, sm_ref, cw_ref, alog_r_ref, dtb_r_ref, alog_c_ref, dtb_c_ref, nw_ref,
                     o_ref, s_ref, xbuf):
    t = pl.program_id(1)
    c = GDN_CHUNK
    nt = gqkv_ref.shape[0]
    nchunk = nt // c
    pad = SUBLANES

    @pl.when(t == 0)
    def _():
        xbuf[0:pad, :] = jnp.zeros((pad, GDN_QKV), F32)
        s_ref[...] = jnp.zeros(s_ref.shape, F32)

    xbuf[pad:pad + nt, :] = gqkv_ref[...]
    y = cw_ref[GDN_CONV - 1:GDN_CONV, :] * xbuf[pad:pad + nt, :]
    for tap in range(GDN_CONV - 1):
        y = y + cw_ref[tap:tap + 1, :] * xbuf[pad - (GDN_CONV - 1) + tap:pad - (GDN_CONV - 1) + tap + nt, :]
    xbuf[0:pad, :] = xbuf[nt:nt + pad, :]
    act = _silu(y)
    ind = _head_indicator(GDN_W, N_GDN)
    ind_t = _head_indicator_t(N_GDN, GDN_W)
    qraw = act[:, 0:GDN_W]
    kraw = act[:, GDN_W:2 * GDN_W]
    v = act[:, 2 * GDN_W:3 * GDN_W]
    qn = qraw * _l2_scale(qraw, ind, ind_t) * (HEAD_DIM ** -0.5)
    kn = kraw * _l2_scale(kraw, ind, ind_t)

    sm = sm_ref[...]
    sm_t = sm.T
    g_col = -jnp.exp(alog_r_ref[...]) * _softplus(sm[:, SM_GA:SM_GA + N_GDN] + dtb_r_ref[...])
    g_row = -jnp.exp(alog_c_ref[...]) * _softplus(sm_t[SM_GA:SM_GA + N_GDN, :] + dtb_c_ref[...])
    beta = _sigmoid(sm[:, SM_GB:SM_GB + N_GDN])
    ti = lax.broadcasted_iota(jnp.int32, (nt, nt), 0)
    tj = lax.broadcasted_iota(jnp.int32, (nt, nt), 1)
    same = (ti // c) == (tj // c)
    gc_col = _dot_f32(jnp.where(same & (ti >= tj), 1.0, 0.0), g_col)
    gc_row = _dot_f32(g_row, jnp.where(same & (ti <= tj), 1.0, 0.0))

    items = [(g, h) for g in range(nchunk) for h in range(N_GDN)]
    rows = lambda g: slice(g * c, (g + 1) * c)
    heads = lambda x: jnp.stack([x[rows(g), h * HEAD_DIM:(h + 1) * HEAD_DIM] for g, h in items], axis=0)
    cols = lambda x: jnp.stack([x[rows(g), h:h + 1] for g, h in items], axis=0)
    qb, kb, vb = heads(qn), heads(kn), heads(v)
    gcc = cols(gc_col)
    bcol = cols(beta)
    gcr = jnp.stack([gc_row[h:h + 1, rows(g)] for g, h in items], axis=0)
    gl = jnp.stack([gc_row[h:h + 1, (g + 1) * c - 1:(g + 1) * c] for g, h in items], axis=0)
    ii = lax.broadcasted_iota(jnp.int32, (c, c), 0)
    jj = lax.broadcasted_iota(jnp.int32, (c, c), 1)
    dec = jnp.exp(jnp.where(ii >= jj, gcc - gcr, NEG))
    egc = jnp.exp(gcc)
    kbeta = kb * bcol
    nmat = jnp.where(ii > jj, _bdot(kbeta, kb, _BNT) * dec, 0.0)
    x = _solve_unit_lower(nmat, jnp.concatenate([vb * bcol, kbeta * egc], axis=2))
    val, kcd = x[:, :, 0:HEAD_DIM], x[:, :, HEAD_DIM:2 * HEAD_DIM]
    inner = _bdot(qb, kb, _BNT) * dec
    qg = qb * egc
    kend = kb * jnp.exp(gl - gcc)
    egl = jnp.exp(gl)

    s = s_ref[...]
    outs = []
    for g in range(nchunk):
        it = slice(g * N_GDN, (g + 1) * N_GDN)
        vn = val[it] - _bdot(kcd[it], s)
        og = _bdot(qg[it], s) + _bdot(inner[it], vn)
        s = s * egl[it] + _bdot(kend[it], vn, _BTN)
        outs.append(jnp.concatenate([og[h] for h in range(N_GDN)], axis=1))
    s_ref[...] = s
    o = jnp.concatenate(outs, axis=0)
    rms = _dot_sel(lax.rsqrt(_dot_sel(o * o, ind) * (1.0 / HEAD_DIM) + RMS_EPS), ind_t)
    o_ref[...] = o * rms * nw_ref[...] * _silu(gz_ref[...])


GDN_TOK = 4 * GDN_CHUNK


def _gdn_prompt(gqkv3, gz3, sm3, cw, alog, dtb, nw_t):
    b, t, _ = gqkv3.shape
    assert t % GDN_TOK == 0
    c = GDN_TOK
    est = 4 * c * GDN_QKV * 4 + 64 * c * 512 * 4
    return pl.pallas_call(
        _gdn_prompt_body,
        grid=(b, t // c),
        in_specs=[
            pl.BlockSpec((None, c, GDN_QKV), lambda i, j: (i, j, 0)),
            pl.BlockSpec((None, c, GDN_W), lambda i, j: (i, j, 0)),
            pl.BlockSpec((None, c, SM_W), lambda i, j: (i, j, 0)),
            _resident((GDN_CONV, GDN_QKV)), _resident((1, N_GDN)), _resident((1, N_GDN)),
            _resident((N_GDN, 1)), _resident((N_GDN, 1)), _resident((1, GDN_W)),
        ],
        out_specs=[pl.BlockSpec((None, c, GDN_W), lambda i, j: (i, j, 0)),
                   pl.BlockSpec((None, N_GDN, HEAD_DIM, HEAD_DIM), lambda i, j: (i, 0, 0, 0))],
        out_shape=[jax.ShapeDtypeStruct((b, t, GDN_W), F32),
                   jax.ShapeDtypeStruct((b, N_GDN, HEAD_DIM, HEAD_DIM), F32)],
        scratch_shapes=[pltpu.VMEM((c + SUBLANES, GDN_QKV), F32)],
        compiler_params=_params(("parallel", "arbitrary"), est),
        name="gdn_prompt",
    )(gqkv3, gz3, sm3, cw, alog.reshape(1, N_GDN), dtb.reshape(1, N_GDN),
      alog.reshape(N_GDN, 1), dtb.reshape(N_GDN, 1), nw_t)


def _post_tail(x1, ffn, p, lng_ref, lnb_ref, wg_ref, wp_ref):
    x2 = _layer_norm(DN_ALPHA * x1 + ffn, lng_ref[1:2, :], lnb_ref[1:2, :])
    ple = _sigmoid(jnp.dot(x2.astype(BF16), wg_ref[...], preferred_element_type=F32)) \
        * jnp.dot(p.astype(BF16), wp_ref[...], preferred_element_type=F32)
    return _layer_norm(DN_ALPHA * x2 + ple, lng_ref[2:3, :], lnb_ref[2:3, :])


def _post_head(onsa_ref, og_ref, x_ref, wo_ref, lng_ref, lnb_ref):
    attn = jnp.dot(onsa_ref[...].astype(BF16), wo_ref[0:512, :], preferred_element_type=F32) \
        + jnp.dot(og_ref[...].astype(BF16), wo_ref[512:1024, :], preferred_element_type=F32)
    return _layer_norm(DN_ALPHA * x_ref[...] + attn, lng_ref[0:1, :], lnb_ref[0:1, :])


def _post_prompt_body(onsa_ref, og_ref, x_ref, p_ref, wo_ref, lng_ref, lnb_ref, wu_ref, cw_ref, wd_ref,
                      wp_ref, wg_ref, y_ref, fbuf_ref, upbuf):
    t = pl.program_id(1)
    tm = x_ref.shape[0]
    pad = SUBLANES

    @pl.when(t == 0)
    def _():
        upbuf[0:pad, :] = jnp.zeros((pad, D_FF), F32)

    x1 = _post_head(onsa_ref, og_ref, x_ref, wo_ref, lng_ref, lnb_ref)
    x1b = x1.astype(BF16)
    ffn = jnp.zeros((tm, D_MODEL), F32)
    for cc in range(D_FF // FF_CHUNK):
        cs = slice(cc * FF_CHUNK, (cc + 1) * FF_CHUNK)
        up_g = jnp.dot(x1b, wu_ref[:, cs], preferred_element_type=F32)
        up_v = jnp.dot(x1b, wu_ref[:, D_FF + cc * FF_CHUNK:D_FF + (cc + 1) * FF_CHUNK], preferred_element_type=F32)
        upbuf[pad:pad + tm, cs] = up_g
        hg = cw_ref[FFN_CONV - 1:FFN_CONV, cs] * up_g
        for tap in range(FFN_CONV - 1):
            o0 = pad - (FFN_CONV - 1) + tap
            hg = hg + cw_ref[tap:tap + 1, cs] * upbuf[o0:o0 + tm, cs]
        ffn = ffn + jnp.dot((_gelu(hg) * up_v).astype(BF16), wd_ref[cs, :], preferred_element_type=F32)
    fbuf_ref[...] = upbuf[pad + tm - (FFN_CONV - 1):pad + tm, :]
    upbuf[0:pad, :] = upbuf[tm:tm + pad, :]
    y_ref[...] = _post_tail(x1, ffn, p_ref[...], lng_ref, lnb_ref, wg_ref, wp_ref)


def _post_decode_body(onsa_ref, og_ref, x_ref, p_ref, fst_ref, wo_ref, lng_ref, lnb_ref, wu_ref, cw_ref, wd_ref,
                      wp_ref, wg_ref, y_ref, fnew_ref):
    tm = x_ref.shape[0]
    x1 = _post_head(onsa_ref, og_ref, x_ref, wo_ref, lng_ref, lnb_ref)
    x1b = x1.astype(BF16)
    ffn = jnp.zeros((tm, D_MODEL), F32)
    for cc in range(D_FF // FF_CHUNK):
        cs = slice(cc * FF_CHUNK, (cc + 1) * FF_CHUNK)
        cs1 = slice(D_FF + cc * FF_CHUNK, D_FF + (cc + 1) * FF_CHUNK)
        up_g = jnp.dot(x1b, wu_ref[:, cs], preferred_element_type=F32)
        up_v = jnp.dot(x1b, wu_ref[:, cs1], preferred_element_type=F32)
        hg = cw_ref[0:1, cs] * fst_ref[:, cs] + cw_ref[1:2, cs] * fst_ref[:, cs1] + cw_ref[2:3, cs] * up_g
        fnew_ref[:, cs] = fst_ref[:, cs1]
        fnew_ref[:, cs1] = up_g
        ffn = ffn + jnp.dot((_gelu(hg) * up_v).astype(BF16), wd_ref[cs, :], preferred_element_type=F32)
    y_ref[...] = _post_tail(x1, ffn, p_ref[...], lng_ref, lnb_ref, wg_ref, wp_ref)


def _post_weight_specs():
    return [_resident((D_MODEL, D_MODEL)), _resident((3, D_MODEL)), _resident((3, D_MODEL)),
            _resident((D_MODEL, 2 * D_FF)), _resident((FFN_CONV, D_FF)), _resident((D_FF, D_MODEL)),
            _resident((PLE_DIM, D_MODEL)), _resident((D_MODEL, D_MODEL))]


_POST_WEIGHT_BYTES = 2 * (2 * D_MODEL * D_MODEL + 3 * D_MODEL * D_FF + PLE_DIM * D_MODEL) + 4 * 9 * D_FF


def _post_prompt(onsa3, og3, x3, p3, p_base, wts, tm):
    b, t, _ = x3.shape
    row = lambda n: pl.BlockSpec((None, tm, n), lambda i, j: (i, j, 0))
    p_spec = pl.BlockSpec((None, tm, PLE_DIM), lambda i, j: (i + p_base, j, 0))
    est = _POST_WEIGHT_BYTES + 2 * tm * (512 + 512 + 1024 + 256 + 1024) * 4 + (tm + 8) * D_FF * 4 + 12 * tm * 1024 * 4
    return pl.pallas_call(
        _post_prompt_body,
        grid=(b, t // tm),
        in_specs=[row(512), row(512), row(D_MODEL), p_spec] + _post_weight_specs(),
        out_specs=[row(D_MODEL), pl.BlockSpec((None, FFN_CONV - 1, D_FF), lambda i, j: (i, 0, 0))],
        out_shape=[jax.ShapeDtypeStruct((b, t, D_MODEL), F32),
                   jax.ShapeDtypeStruct((b, FFN_CONV - 1, D_FF), F32)],
        scratch_shapes=[pltpu.VMEM((tm + SUBLANES, D_FF), F32)],
        compiler_params=_params(("parallel", "arbitrary"), est),
        name="post_prompt",
    )(onsa3, og3, x3, p3, *wts)


def _post_decode(onsa2, og2, x2, p2, fst2, wts):
    m = x2.shape[0]
    full = lambda n: pl.BlockSpec((m, n), lambda i: (0, 0))
    est = _POST_WEIGHT_BYTES + 2 * m * (512 + 512 + 1024 + 256 + 1024 + 4 * D_FF) * 4 + 12 * m * 1024 * 4
    return pl.pallas_call(
        _post_decode_body,
        grid=(1,),
        in_specs=[full(512), full(512), full(D_MODEL), full(PLE_DIM), full(2 * D_FF)] + _post_weight_specs(),
        out_specs=[full(D_MODEL), full(2 * D_FF)],
        out_shape=[jax.ShapeDtypeStruct((m, D_MODEL), F32), jax.ShapeDtypeStruct((m, 2 * D_FF), F32)],
        compiler_params=_params(("arbitrary",), est),
        name="post_decode",
    )(onsa2, og2, x2, p2, fst2, *wts)


def _nsa_decode_body(pt_ref, *refs, n_pages, page, chained):
    page_refs = refs[:n_pages]
    (q_ref, kvn_ref, winn_ref, sm_ref, wbuf_ref, pe_ref, phi_ref) = refs[n_pages:n_pages + 7]
    o_ref, wout_ref = refs[n_pages + 7 + (1 if chained else 0):]
    del pt_ref
    past = n_pages * page
    ncv = past // CMP_BLOCK
    nsb = past // SLC_BLOCK
    assert ncv <= LANES and ncv % 2 == 0
    q = q_ref[...] * (HEAD_DIM ** -0.5)
    zero = jnp.zeros((1, HEAD_DIM), F32)
    rows = []
    for h in range(N_HEADS):
        piece = q[:, h * HEAD_DIM:(h + 1) * HEAD_DIM]
        rows.append(jnp.concatenate([piece, zero] if h < GROUP else [zero, piece], axis=1))
    qbd = jnp.concatenate(rows, axis=0)
    qbd_b = qbd.astype(BF16)
    hrow = lax.broadcasted_iota(jnp.int32, (N_HEADS, 1), 0)
    slope = jnp.concatenate([jnp.full((1, 1), 2.0 ** -(h + 1), F32) for h in range(N_HEADS)], axis=0)

    cmp_rows = jnp.concatenate([pr[0:2 * KV_W, :] for pr in page_refs], axis=1)
    pool = jnp.where(lax.broadcasted_iota(jnp.int32, (past, ncv), 0) // CMP_BLOCK
                     == lax.broadcasted_iota(jnp.int32, (past, ncv), 1), 1.0, 0.0)
    mean_t = _dot_sel(cmp_rows, pool) * (1.0 / CMP_BLOCK)
    kc_t, vc_t = [], []
    for which, dst in ((0, kc_t), (1, vc_t)):
        pe_col = jnp.sum(pe_ref[which].T, axis=1, keepdims=True) * (1.0 / CMP_BLOCK)
        for n in range(N_KV):
            blk = mean_t[which * KV_W + n * HEAD_DIM:which * KV_W + (n + 1) * HEAD_DIM, :] + pe_col
            dst.append(lax.dot_general(phi_ref[which], blk, _TN, precision=lax.Precision.HIGHEST,
                                       preferred_element_type=F32))
    kc_t = jnp.concatenate(kc_t, axis=0)
    vc_t = jnp.concatenate(vc_t, axis=0)

    cidx = lax.broadcasted_iota(jnp.int32, (1, ncv), 1)
    c_mid = cidx.astype(F32) * CMP_BLOCK + (CMP_BLOCK - 1) / 2
    s = _dot(qbd_b, kc_t) - slope * (float(past) - c_mid)
    p = _softmax_rows(s)
    o_cmp = _dot_nt(p, vc_t)

    p_l = p if ncv == LANES else jnp.concatenate([p, jnp.zeros((N_HEADS, LANES - ncv), F32)], axis=1)
    pair = p_l + pltpu.roll(p_l, LANES - 1, 1)
    pooled = jnp.concatenate([jnp.sum(pair[n * GROUP:(n + 1) * GROUP], axis=0, keepdims=True) for n in range(N_KV)],
                             axis=0)
    lane = lax.broadcasted_iota(jnp.int32, (N_KV, LANES), 1)
    blk = lane // 2
    is_blk = ((lane % 2) == 0) & (lane < ncv)
    is_cur = lane == ncv
    forced = is_cur | (is_blk & ((blk == 0) | ((nsb - blk) < N_LOCAL)))
    score = jnp.where(forced, BIG, jnp.where(is_blk, pooled, -BIG))
    rank = jnp.zeros((N_KV, LANES), F32)
    cand = [2 * b for b in range(nsb)] + [ncv]
    for lc in cand:
        col = score[:, lc:lc + 1]
        rank = rank + jnp.where(lane > lc, jnp.where(col >= score, 1.0, 0.0), jnp.where(col > score, 1.0, 0.0))
    sel = jnp.where((rank < float(min(TOP_N, nsb + 1))) & (is_blk | is_cur), 1.0, 0.0)
    sel8 = jnp.concatenate([jnp.broadcast_to(sel[n:n + 1], (GROUP, LANES)) for n in range(N_KV)], axis=0).astype(BF16)

    kvn = kvn_ref[...]
    s_new = jnp.sum(qbd * kvn[:, 2 * KV_W:3 * KV_W], axis=-1, keepdims=True)
    s_parts = []
    for i, pr in enumerate(page_refs):
        tpos = i * page + lax.broadcasted_iota(jnp.int32, (1, page), 1)
        member = jnp.where((2 * ((i * page + lax.broadcasted_iota(jnp.int32, (LANES, page), 1)) // SLC_BLOCK))
                           == lax.broadcasted_iota(jnp.int32, (LANES, page), 0), 1.0, 0.0).astype(BF16)
        chosen = jnp.dot(sel8, member, preferred_element_type=F32)
        st = _dot(qbd_b, pr[2 * KV_W:3 * KV_W, :]) - slope * (float(past) - tpos.astype(F32))
        s_parts.append(jnp.where(chosen > 0.5, st, NEG))
    s_all = jnp.concatenate(s_parts, axis=1)
    m = jnp.maximum(jnp.max(s_all, axis=-1, keepdims=True), s_new)
    e = jnp.exp(s_all - m)
    e_new = jnp.exp(s_new - m)
    acc = e_new * kvn[:, 3 * KV_W:4 * KV_W]
    for i, pr in enumerate(page_refs):
        acc = acc + _dot_nt(e[:, i * page:(i + 1) * page], pr[3 * KV_W:4 * KV_W, :])
    o_slc = acc / (jnp.sum(e, axis=-1, keepdims=True) + e_new)

    wb = wbuf_ref.shape[1]
    winn = winn_ref[...]
    rpos = lax.broadcasted_iota(jnp.int32, (1, wb), 1)
    dist = wb - rpos
    sw = _dot(qbd_b, wbuf_ref[0:KV_W, :]) - slope * dist.astype(F32)
    sw = jnp.where(dist < WINDOW, sw, NEG)
    sw_new = jnp.sum(qbd * winn[:, 0:KV_W], axis=-1, keepdims=True)
    mw = jnp.maximum(jnp.max(sw, axis=-1, keepdims=True), sw_new)
    ew = jnp.exp(sw - mw)
    ew_new = jnp.exp(sw_new - mw)
    o_win = (_dot_nt(ew, wbuf_ref[KV_W:2 * KV_W, :]) + ew_new * winn[:, KV_W:2 * KV_W]) \
        / (jnp.sum(ew, axis=-1, keepdims=True) + ew_new)
    winn_col = jnp.broadcast_to(winn, (SUBLANES, 2 * KV_W)).T[:, 0:1]
    shifted = pltpu.roll(wbuf_ref[...], wb - 1, 1)
    wout_ref[...] = jnp.where(lax.broadcasted_iota(jnp.int32, (2 * KV_W, wb), 1) == wb - 1, winn_col, shifted)

    gates = _sigmoid(sm_ref[:, 0:3 * N_HEADS])
    glane = lax.broadcasted_iota(jnp.int32, (N_HEADS, 3 * N_HEADS), 1)
    gsel = lambda r: jnp.sum(jnp.where(glane == 3 * hrow + r, gates, 0.0), axis=-1, keepdims=True)
    o8 = gsel(0) * o_cmp + gsel(1) * o_slc + gsel(2) * o_win
    pieces = []
    for h in range(N_HEADS):
        n = h // GROUP
        pieces.append(o8[h:h + 1, n * HEAD_DIM:(n + 1) * HEAD_DIM])
    o_ref[...] = jnp.concatenate(pieces, axis=1)


def _nsa_decode(page_table, cache3, page_base, q3, kvn3, winn3, sm3, wbuf3, seq_base, pe, phi, wout_prev):
    nb, n_pages = page_table.shape
    page = cache3.shape[2]
    wb = wbuf3.shape[2]
    assert wb == WINDOW and n_pages * page >= wb
    chained = wout_prev is not None
    page_specs = [pl.BlockSpec((None, 4 * KV_W, page),
                               functools.partial(lambda i, pt, k: (pt[i, k] + page_base, 0, 0), k=k))
                  for k in range(n_pages)]
    one = lambda n: pl.BlockSpec((None, 1, n), lambda i, pt: (i, 0, 0))
    win_spec = pl.BlockSpec((None, 2 * KV_W, wb), lambda i, pt: (i + seq_base, 0, 0))
    in_specs = page_specs + [one(512), one(512), one(256), one(SM_W), win_spec,
                             pl.BlockSpec((2, CMP_BLOCK, HEAD_DIM), lambda i, pt: (0, 0, 0)),
                             pl.BlockSpec((2, HEAD_DIM, HEAD_DIM), lambda i, pt: (0, 0, 0))]
    args = [page_table] + [cache3] * n_pages + [q3, kvn3, winn3, sm3, wbuf3, pe, phi]
    aliases = {}
    if chained:
        in_specs.append(pl.BlockSpec(memory_space=pl.ANY))
        args.append(wout_prev)
        aliases = {len(args) - 1: 1}
    grid_spec = pltpu.PrefetchScalarGridSpec(
        num_scalar_prefetch=1, grid=(nb,), in_specs=in_specs, out_specs=[one(512), win_spec])
    est = 2 * n_pages * page * 512 * 4 + 4 * wb * 256 * 4 + 8 * 1024 * 1024
    return pl.pallas_call(
        functools.partial(_nsa_decode_body, n_pages=n_pages, page=page, chained=chained),
        grid_spec=grid_spec,
        out_shape=[jax.ShapeDtypeStruct((nb, 1, 512), F32), jax.ShapeDtypeStruct(wbuf3.shape, F32)],
        input_output_aliases=aliases,
        compiler_params=_params(("arbitrary",), est),
        name="nsa_decode",
    )(*args)


def _gdn_dec_prep_body(gqkv_ref, cst_ref, sm_ref, cw_ref, alog_ref, dtb_ref, qn_ref, kn_ref, v_ref, db_ref, cnew_ref):
    w = GDN_QKV
    xn = gqkv_ref[...]
    y = cw_ref[GDN_CONV - 1:GDN_CONV, :] * xn
    for tap in range(GDN_CONV - 1):
        y = y + cw_ref[tap:tap + 1, :] * cst_ref[:, tap * w:(tap + 1) * w]
    for tap in range(GDN_CONV - 2):
        cnew_ref[:, tap * w:(tap + 1) * w] = cst_ref[:, (tap + 1) * w:(tap + 2) * w]
    cnew_ref[:, (GDN_CONV - 2) * w:(GDN_CONV - 1) * w] = xn
    act = _silu(y)
    ind = _head_indicator(GDN_W, N_GDN)
    ind_t = _head_indicator_t(N_GDN, GDN_W)
    qraw = act[:, 0:GDN_W]
    kraw = act[:, GDN_W:2 * GDN_W]
    qn_ref[...] = qraw * _l2_scale(qraw, ind, ind_t) * (HEAD_DIM ** -0.5)
    kn_ref[...] = kraw * _l2_scale(kraw, ind, ind_t)
    v_ref[...] = act[:, 2 * GDN_W:3 * GDN_W]
    sm = sm_ref[...]
    g = -jnp.exp(alog_ref[...]) * _softplus(sm[:, SM_GA:SM_GA + N_GDN] + dtb_ref[...])
    beta = _sigmoid(sm[:, SM_GB:SM_GB + N_GDN])
    m = sm.shape[0]
    db_ref[...] = jnp.concatenate([jnp.exp(g), beta, jnp.zeros((m, SM_W - 2 * N_GDN), F32)], axis=1)


def _gdn_dec_prep(gqkv2, cst2, sm2, cw, alog, dtb):
    m = gqkv2.shape[0]
    full = lambda n: pl.BlockSpec((m, n), lambda i: (0, 0))
    est = 2 * m * (8 * GDN_QKV + 4 * 512) * 4 + 8 * m * GDN_QKV * 4
    return pl.pallas_call(
        _gdn_dec_prep_body,
        grid=(1,),
        in_specs=[full(GDN_QKV), full((GDN_CONV - 1) * GDN_QKV), full(SM_W),
                  pl.BlockSpec((GDN_CONV, GDN_QKV), lambda i: (0, 0)),
                  pl.BlockSpec((1, N_GDN), lambda i: (0, 0)), pl.BlockSpec((1, N_GDN), lambda i: (0, 0))],
        out_specs=[full(GDN_W), full(GDN_W), full(GDN_W), full(SM_W), full((GDN_CONV - 1) * GDN_QKV)],
        out_shape=[jax.ShapeDtypeStruct((m, GDN_W), F32)] * 3 + [jax.ShapeDtypeStruct((m, SM_W), F32),
                   jax.ShapeDtypeStruct((m, (GDN_CONV - 1) * GDN_QKV), F32)],
        compiler_params=_params(("arbitrary",), est),
        name="gdn_decode_prep",
    )(gqkv2, cst2, sm2, cw, alog.reshape(1, N_GDN), dtb.reshape(1, N_GDN))


def _gdn_dec_step_body(qn_ref, kn_ref, v_ref, db_ref, gz_ref, s_ref, nw_ref, o_ref, snew_ref):
    ind_t = _head_indicator_t(N_GDN, GDN_W)
    hrow = lax.broadcasted_iota(jnp.int32, (N_GDN, SM_W), 0)
    lane = lax.broadcasted_iota(jnp.int32, (N_GDN, SM_W), 1)
    db = db_ref[...]
    a_col = jnp.sum(jnp.where(lane == hrow, db, 0.0), axis=-1, keepdims=True)
    b_col = jnp.sum(jnp.where(lane == hrow + N_GDN, db, 0.0), axis=-1, keepdims=True)
    ind_f = ind_t.astype(F32)
    k_bd = ind_f * kn_ref[...]
    q_bd = ind_f * qn_ref[...]
    v = v_ref[...]
    v8 = jnp.concatenate([v[:, h * HEAD_DIM:(h + 1) * HEAD_DIM] for h in range(N_GDN)], axis=0)
    s = s_ref[...]
    hi, mid, lo = _split3(s)
    sk = _dot(k_bd, hi) + _dot(k_bd, mid) + _dot(k_bd, lo)
    vn = b_col * (v8 - a_col * sk)
    a_rows = jnp.broadcast_to(a_col, (N_GDN, HEAD_DIM))
    s_new = _dot_tn_sel(ind_t, a_rows) * s + _dot_tn(k_bd, vn)
    snew_ref[...] = s_new
    o8 = _dot(q_bd, s_new)
    o8 = o8 * lax.rsqrt(jnp.mean(o8 * o8, axis=-1, keepdims=True) + RMS_EPS) * nw_ref[...]
    o = jnp.concatenate([o8[h:h + 1, :] for h in range(N_GDN)], axis=1)
    o_ref[...] = o * _silu(gz_ref[...])


def _gdn_dec_step(qn3, kn3, v3, db3, gz3, s3, seq_base, nw):
    nb = qn3.shape[0]
    one = lambda n: pl.BlockSpec((None, 1, n), lambda i: (i, 0, 0))
    st = pl.BlockSpec((None, GDN_W, HEAD_DIM), lambda i: (i, 0, 0))
    est = 4 * GDN_W * HEAD_DIM * 4 + 4 * 1024 * 1024
    return pl.pallas_call(
        _gdn_dec_step_body,
        grid=(nb,),
        in_specs=[one(GDN_W), one(GDN_W), one(GDN_W), one(SM_W), one(GDN_W),
                  pl.BlockSpec((None, GDN_W, HEAD_DIM), lambda i: (i + seq_base, 0, 0)),
                  pl.BlockSpec((1, HEAD_DIM), lambda i: (0, 0))],
        out_specs=[one(GDN_W), st],
        out_shape=[jax.ShapeDtypeStruct((nb, 1, GDN_W), F32), jax.ShapeDtypeStruct((nb, GDN_W, HEAD_DIM), F32)],
        compiler_params=_params(("parallel",), est),
        name="gdn_decode_step",
    )(qn3, kn3, v3, db3, gz3, s3, nw)


def _prep_w_in(w):
    small = jnp.concatenate([w[:, C_GATE:C_GQKV], w[:, C_GA:C_GB], w[:, C_GB:C_GZ],
                             jnp.zeros((D_MODEL, SM_W - (C_GQKV - C_GATE) - 2 * N_GDN), w.dtype)], axis=1)
    return jnp.concatenate([w[:, 0:C_KV], w[:, C_KV:C_WIN], w[:, C_WIN:C_GATE], w[:, C_GQKV:C_GA],
                            w[:, C_GZ:IN_WIDTH], small], axis=1).astype(BF16)


def kernel(x_prompt, x_sample, cache_nsa_kv, state_nsa_win, state_gdn, state_gdn_conv, state_ffn_conv, page_table, p_prompt, p_sample, w_in, nsa_pe, nsa_phi, gdn_conv_w, gdn_A_log, gdn_dt_bias, gdn_norm_w, w_out, ln_g, ln_b, ffn_w_up, ffn_conv_w, ffn_w_down, ple_w_proj, ple_w_gate):
    b, t, d = x_prompt.shape
    nb, ts, _ = x_sample.shape
    assert d == D_MODEL and ts == 1 and t % Q_BLOCK == 0 and t >= WINDOW
    depth = w_in.shape[0]
    n_pool, page = cache_nsa_kv.shape[1], cache_nsa_kv.shape[2]
    wb = state_nsa_win.shape[2]
    tm_p = 256
    xp, xs = x_prompt, x_sample.reshape(nb, d)
    cache_all = cache_nsa_kv.transpose(0, 1, 3, 4, 5, 2).reshape(depth * n_pool, 4 * KV_W, page)
    wbuf_all = state_nsa_win.transpose(0, 1, 3, 4, 5, 2).reshape(depth * nb, 2 * KV_W, wb)
    gstate_all = state_gdn.reshape(depth * nb, GDN_W, HEAD_DIM)
    p_all = p_prompt.reshape(depth * b, t, PLE_DIM)
    win_all = None
    st_p, st_s = [], []
    for l in range(depth):
        w_cat = _prep_w_in(w_in[l])
        post_w = (w_out[l].astype(BF16), ln_g[l], ln_b[l], ffn_w_up[l].astype(BF16), ffn_conv_w[l],
                  ffn_w_down[l].astype(BF16), ple_w_proj[l].astype(BF16), ple_w_gate[l].astype(BF16))
        nw_t = jnp.tile(gdn_norm_w[l].reshape(1, HEAD_DIM), (1, N_GDN))

        q, kv, win, gqkv, gz, sm = _proj_in(xp.reshape(b * t, d), w_cat, 512)
        kv3, win3 = kv.reshape(b, t, 512), win.reshape(b, t, 256)
        sm3, gqkv3 = sm.reshape(b, t, SM_W), gqkv.reshape(b, t, GDN_QKV)
        kc, vc = _compress(kv3, nsa_pe[l], nsa_phi[l])
        o_nsa = _nsa_prompt(q.reshape(b, t, 512), kv3, kc, vc, win3, sm3)
        o_g, s_p = _gdn_prompt(gqkv3, gz.reshape(b, t, GDN_W), sm3, gdn_conv_w[l], gdn_A_log[l], gdn_dt_bias[l], nw_t)
        xp, fbuf_p = _post_prompt(o_nsa, o_g, xp, p_all, l * b, post_w, tm_p)
        st_p.append((kv3.reshape(b, t, 4, N_KV, HEAD_DIM), win3[:, t - wb:].reshape(b, wb, 2, N_KV, HEAD_DIM),
                     s_p, gqkv3[:, t - (GDN_CONV - 1):], fbuf_p))

        q, kv, win, gqkv, gz, sm = _proj_in(xs, w_cat, nb)
        o_nsa, win_all = _nsa_decode(
            page_table, cache_all, l * n_pool, q.reshape(nb, 1, 512), kv.reshape(nb, 1, 512),
            win.reshape(nb, 1, 256), sm.reshape(nb, 1, SM_W), wbuf_all, l * nb, nsa_pe[l], nsa_phi[l], win_all)
        qn, kn, v, db, cnew = _gdn_dec_prep(gqkv, state_gdn_conv[l].reshape(nb, (GDN_CONV - 1) * GDN_QKV), sm,
                                            gdn_conv_w[l], gdn_A_log[l], gdn_dt_bias[l])
        o_g, s_s = _gdn_dec_step(qn.reshape(nb, 1, GDN_W), kn.reshape(nb, 1, GDN_W), v.reshape(nb, 1, GDN_W),
                                 db.reshape(nb, 1, SM_W), gz.reshape(nb, 1, GDN_W),
                                 gstate_all, l * nb, gdn_norm_w[l].reshape(1, HEAD_DIM))
        xs, fnew = _post_decode(o_nsa.reshape(nb, 512), o_g.reshape(nb, GDN_W), xs, p_sample[l].reshape(nb, PLE_DIM),
                                state_ffn_conv[l].reshape(nb, (FFN_CONV - 1) * D_FF), post_w)
        st_s.append((kv.reshape(nb, 1, 4, N_KV, HEAD_DIM), None,
                     s_s.reshape(nb, N_GDN, HEAD_DIM, HEAD_DIM), cnew.reshape(nb, GDN_CONV - 1, GDN_QKV),
                     fnew.reshape(nb, FFN_CONV - 1, D_FF)))

    win_sample = win_all.reshape(depth, nb, 2, N_KV, HEAD_DIM, wb).transpose(0, 1, 5, 2, 3, 4)
    outs = [xp, xs.reshape(nb, 1, d)]
    for k in range(5):
        outs.append(jnp.stack([s[k] for s in st_p]))
        outs.append(win_sample if k == 1 else jnp.stack([s[k] for s in st_s]))
    return tuple(outs)
```

```python
import functools

import jax
import jax.numpy as jnp
from jax import lax
from jax.experimental import pallas as pl
from jax.experimental.pallas import tpu as pltpu

F32 = jnp.float32
BF16 = jnp.bfloat16

D_MODEL = 1024
DEPTH = 2
HEAD_DIM = 64
N_HEADS = 8
N_KV = 2
GROUP = 4
N_GDN = 8
KV_W = N_KV * HEAD_DIM
GDN_W = N_GDN * HEAD_DIM
GDN_QKV = 3 * GDN_W
CMP_BLOCK = 32
SLC_BLOCK = 64
TOP_N = 16
N_LOCAL = 2
WINDOW = 512
Q_BLOCK = 128
GDN_CHUNK = 64
GDN_CONV = 4
FFN_CONV = 3
D_FF = 2816
PLE_DIM = 256
LN_EPS = 1e-5
RMS_EPS = 1e-6
DN_ALPHA = (2 * DEPTH) ** 0.25
NEG = -1e30
BIG = 1e30

C_KV = 512
C_WIN = 1024
C_GATE = 1280
C_GQKV = 1304
C_GA = 2840
C_GB = 2848
C_GZ = 2856
IN_WIDTH = 3368
SM_W = 128
SM_GA = 24
SM_GB = 32

LANES = 128
SUBLANES = 8
VMEM_CAP = 56 * 1024 * 1024
FF_CHUNK = 256
SLC_TILE = 512

_NT = (((1,), (1,)), ((), ()))
_TN = (((0,), (0,)), ((), ()))


def _dot(a, b):
    return jnp.dot(a.astype(BF16), b.astype(BF16), preferred_element_type=F32)


def _dot_nt(a, b):
    return lax.dot_general(a.astype(BF16), b.astype(BF16), _NT, preferred_element_type=F32)


def _dot_tn(a, b):
    return lax.dot_general(a.astype(BF16), b.astype(BF16), _TN, preferred_element_type=F32)


def _dot_f32(a, b):
    return jnp.dot(a, b, precision=lax.Precision.HIGHEST, preferred_element_type=F32)


def _split3(a):
    hi = a.astype(BF16)
    r = a - hi.astype(F32)
    mid = r.astype(BF16)
    lo = (r - mid.astype(F32)).astype(BF16)
    return hi, mid, lo


def _dot_sel(a, sel):
    hi, mid, lo = _split3(a)
    sel = sel.astype(BF16)
    d = lambda u: jnp.dot(u, sel, preferred_element_type=F32)
    return d(hi) + d(mid) + d(lo)


def _dot_tn_sel(sel, a):
    hi, mid, lo = _split3(a)
    sel = sel.astype(BF16)
    d = lambda u: lax.dot_general(sel, u, _TN, preferred_element_type=F32)
    return d(hi) + d(mid) + d(lo)


def _sigmoid(x):
    return 1.0 / (1.0 + jnp.exp(-x))


def _silu(x):
    return x * _sigmoid(x)


def _softplus(x):
    return jnp.maximum(x, 0.0) + jnp.log1p(jnp.exp(-jnp.abs(x)))


def _layer_norm(x, g, b):
    mu = jnp.mean(x, axis=-1, keepdims=True)
    xc = x - mu
    var = jnp.mean(xc * xc, axis=-1, keepdims=True)
    return xc * lax.rsqrt(var + LN_EPS) * g + b


def _gelu(x):
    return 0.5 * x * (1.0 + lax.erf(x * (0.5 ** 0.5)))


def _head_indicator(width, heads):
    r = lax.broadcasted_iota(jnp.int32, (width, heads), 0) // HEAD_DIM
    c = lax.broadcasted_iota(jnp.int32, (width, heads), 1)
    return jnp.where(r == c, 1.0, 0.0).astype(BF16)


def _head_indicator_t(heads, width):
    r = lax.broadcasted_iota(jnp.int32, (heads, width), 0)
    c = lax.broadcasted_iota(jnp.int32, (heads, width), 1) // HEAD_DIM
    return jnp.where(r == c, 1.0, 0.0).astype(BF16)


def _params(sem, est_bytes):
    limit = int(min(max(2 * est_bytes, 32 * 1024 * 1024), VMEM_CAP))
    return pltpu.CompilerParams(dimension_semantics=sem, vmem_limit_bytes=limit)


def _resident(shape):
    nd = len(shape)
    return pl.BlockSpec(shape, lambda *_: (0,) * nd, pipeline_mode=pl.Buffered(1))


_PROJ_WIDTHS = (512, 512, 256, GDN_QKV, GDN_W, SM_W)


def _proj_in_body(x_ref, w_ref, *out_refs):
    xb = x_ref[...].astype(BF16)
    off = 0
    for ref in out_refs:
        n = ref.shape[-1]
        ref[...] = jnp.dot(xb, w_ref[:, off:off + n], preferred_element_type=F32)
        off += n


def _proj_in(x2, w_cat, tm):
    m = x2.shape[0]
    tot = sum(_PROJ_WIDTHS)
    est = 2 * tm * D_MODEL * 4 + D_MODEL * tot * 2 + 2 * tm * tot * 4
    return pl.pallas_call(
        _proj_in_body,
        grid=(m // tm,),
        in_specs=[pl.BlockSpec((tm, D_MODEL), lambda i: (i, 0)), _resident((D_MODEL, tot))],
        out_specs=[pl.BlockSpec((tm, n), lambda i: (i, 0)) for n in _PROJ_WIDTHS],
        out_shape=[jax.ShapeDtypeStruct((m, n), F32) for n in _PROJ_WIDTHS],
        compiler_params=_params(("parallel",), est),
        name="proj_in",
    )(x2, w_cat)


def _compress_rows(rows, pe, phi):
    nblk = rows.shape[0] // CMP_BLOCK
    mean = rows.reshape(nblk, CMP_BLOCK, KV_W).sum(axis=1) * (1.0 / CMP_BLOCK)
    pem = jnp.mean(pe, axis=0, keepdims=True)
    outs = []
    for n in range(N_KV):
        outs.append(_dot_f32(mean[:, n * HEAD_DIM:(n + 1) * HEAD_DIM] + pem, phi))
    return jnp.concatenate(outs, axis=1)


def _compress_body(kv_ref, pe_ref, phi_ref, kc_ref, vc_ref):
    kc_ref[...] = _compress_rows(kv_ref[:, 0:KV_W], pe_ref[0], phi_ref[0])
    vc_ref[...] = _compress_rows(kv_ref[:, KV_W:2 * KV_W], pe_ref[1], phi_ref[1])


def _compress(kv3, pe, phi):
    b, t, _ = kv3.shape
    nc = t // CMP_BLOCK
    est = 2 * t * 256 * 4 + 4 * nc * KV_W * 4
    return pl.pallas_call(
        _compress_body,
        grid=(b,),
        in_specs=[pl.BlockSpec((None, t, 2 * KV_W), lambda i: (i, 0, 0)),
                  _resident((2, CMP_BLOCK, HEAD_DIM)), _resident((2, HEAD_DIM, HEAD_DIM))],
        out_specs=[pl.BlockSpec((None, nc, KV_W), lambda i: (i, 0, 0))] * 2,
        out_shape=[jax.ShapeDtypeStruct((b, nc, KV_W), F32)] * 2,
        compiler_params=_params(("parallel",), est),
        name="nsa_compress",
    )(kv3, pe, phi)


def _softmax_rows(s):
    m = jnp.max(s, axis=-1, keepdims=True)
    e = jnp.exp(s - m)
    return e / jnp.sum(e, axis=-1, keepdims=True)


def _key_aux(npos, pos_of_row, with_blocks):
    pos = pos_of_row(lax.broadcasted_iota(jnp.int32, (npos, LANES), 0).astype(F32))
    lane = lax.broadcasted_iota(jnp.int32, (npos, LANES), 1)
    hi = jnp.floor(pos * (1.0 / SLC_BLOCK))
    aux = jnp.where(lane == AUX0, hi, jnp.where(lane == AUX0 + 1, pos - hi * SLC_BLOCK,
                    jnp.where((lane == AUX0 + 2) | (lane == AUX0 + 3), 1.0, 0.0)))
    if with_blocks:
        aux = jnp.where(lane < AUX0, jnp.where(hi == lane.astype(F32), NEG, 0.0), aux)
    return aux.astype(BF16)


AUX0 = 64


def _softmax_cols(s):
    m = jnp.max(s, axis=0, keepdims=True)
    e = jnp.exp(s - m)
    return e / jnp.sum(e, axis=0, keepdims=True)


def _nsa_prompt_body(q_ref, slc_ref, kc_ref, vc_ref, win_ref, sm_ref, kaux_ref, caux_ref, o_ref,
                     imp_ref, kcat, wcat, ccat, vts, vtw, vtc, *, seq):
    j = pl.program_id(1)
    nc = seq // CMP_BLOCK
    ns = seq // SLC_BLOCK
    tq = Q_BLOCK
    cols4 = GROUP * tq
    assert ns <= AUX0 and seq <= 256 * SLC_BLOCK

    @pl.when(j == 0)
    def _():
        for c0 in range(0, seq, SLC_TILE):
            rs = slice(c0, c0 + SLC_TILE)
            kcat[rs, 0:KV_W] = slc_ref[rs, 0:KV_W].astype(BF16)
            kcat[rs, KV_W:2 * KV_W] = kaux_ref[rs, :]
            wcat[rs, 0:KV_W] = win_ref[rs, 0:KV_W].astype(BF16)
            wcat[rs, KV_W:2 * KV_W] = kaux_ref[rs, :]
            vts[:, rs] = slc_ref[rs, KV_W:2 * KV_W].T.astype(BF16)
            vtw[:, rs] = win_ref[rs, KV_W:2 * KV_W].T.astype(BF16)
        ccat[:, 0:KV_W] = kc_ref[...].astype(BF16)
        ccat[:, KV_W:2 * KV_W] = caux_ref[...]
        vtc[...] = vc_ref[...].T.astype(BF16)

    q_t = (q_ref[...] * (HEAD_DIM ** -0.5)).T
    gates_t = _sigmoid(sm_ref[...].T[0:3 * N_HEADS, :])
    qpos_r = j * tq + lax.broadcasted_iota(jnp.int32, (1, tq), 1)
    qpos4 = jnp.concatenate([qpos_r] * GROUP, axis=1)
    qa = (qpos4 // SLC_BLOCK).astype(F32)
    qb = (qpos4 % SLC_BLOCK).astype(F32)
    arow = lax.broadcasted_iota(jnp.int32, (SUBLANES, cols4), 0)
    t_q = lax.broadcasted_iota(jnp.int32, (tq, cols4), 1) % tq
    t_k = lax.broadcasted_iota(jnp.int32, (tq, cols4), 0)
    zeros_h = jnp.zeros((HEAD_DIM, cols4), F32)
    outs = [None] * N_HEADS
    rhs_slcs, o_cmps, o_wins = [], [], []
    for n in range(N_KV):
        hs = slice(n * HEAD_DIM, (n + 1) * HEAD_DIM)
        qn_t = jnp.concatenate(
            [q_t[(GROUP * n + g) * HEAD_DIM:(GROUP * n + g + 1) * HEAD_DIM, :] for g in range(GROUP)], axis=1)
        slope = jnp.concatenate(
            [jnp.full((1, tq), 2.0 ** -(GROUP * n + g + 1), F32) for g in range(GROUP)], axis=1)
        q_top = jnp.concatenate([qn_t, zeros_h] if n == 0 else [zeros_h, qn_t], axis=0)
        q_pos = jnp.where(arow == 0, slope * SLC_BLOCK,
                          jnp.where(arow == 1, slope,
                                    jnp.where(arow == 2, -slope * SLC_BLOCK * qa,
                                              jnp.where(arow == 3, -slope * qb, 0.0))))
        q_tail = jnp.zeros((2 * KV_W - KV_W - AUX0 - SUBLANES, cols4), F32)

        def rhs(block_rows):
            return jnp.concatenate([q_top, block_rows, q_pos, q_tail], axis=0).astype(BF16)

        rhs_plain = rhs(jnp.zeros((AUX0, cols4), F32))

        c_end = lax.broadcasted_iota(jnp.int32, (nc, 1), 0) * CMP_BLOCK + (CMP_BLOCK - 1)
        c_vis = c_end <= qpos4
        s = jnp.dot(ccat[...], rhs_plain, preferred_element_type=F32)
        p = jnp.where(c_vis, _softmax_cols(jnp.where(c_vis, s, NEG)), 0.0)
        o_cmp = jnp.dot(vtc[hs, :], p.astype(BF16), preferred_element_type=F32)

        imp_ref[...] = p[:, 0:tq] + p[:, tq:2 * tq] + p[:, 2 * tq:3 * tq] + p[:, 3 * tq:4 * tq]
        impb = imp_ref[pl.ds(0, ns, stride=2), :] + imp_ref[pl.ds(1, ns, stride=2), :]
        blk = lax.broadcasted_iota(jnp.int32, (ns, tq), 0)
        cur = qpos_r // SLC_BLOCK
        future = blk > cur
        forced = (blk == 0) | (((cur - blk) < N_LOCAL) & jnp.logical_not(future))
        score = jnp.where(future, -BIG, jnp.where(forced, BIG, impb))
        rank = jnp.zeros((ns, tq), F32)
        for bp in range(ns):
            row = score[bp:bp + 1, :]
            rank = rank + jnp.where(blk > bp, jnp.where(row >= score, 1.0, 0.0), jnp.where(row > score, 1.0, 0.0))
        unsel_t = jnp.where(rank < float(min(TOP_N, ns)), 0.0, 1.0)
        if ns < AUX0:
            unsel_t = jnp.concatenate([unsel_t, jnp.zeros((AUX0 - ns, tq), F32)], axis=0)
        rhs_slcs.append(rhs(jnp.concatenate([unsel_t] * GROUP, axis=1)))

        nwb = WINDOW // tq + 1
        s_parts, starts = [], []
        for i in range(nwb):
            kb = j - (nwb - 1) + i
            r0 = pl.multiple_of(jnp.maximum(kb, 0) * tq, tq)
            sw = jnp.dot(wcat[pl.ds(r0, tq), :], rhs_plain, preferred_element_type=F32)
            if i == 0:
                sw = jnp.where(t_k > t_q, sw, NEG)
            if i == nwb - 1:
                sw = jnp.where(t_k <= t_q, sw, NEG)
            else:
                sw = jnp.where(kb >= 0, sw, NEG)
            s_parts.append(sw)
            starts.append(r0)
        pw = _softmax_cols(jnp.concatenate(s_parts, axis=0)).astype(BF16)
        o_win = jnp.dot(vtw[hs, pl.ds(starts[0], tq)], pw[0:tq], preferred_element_type=F32)
        for i in range(1, nwb):
            o_win = o_win + jnp.dot(vtw[hs, pl.ds(starts[i], tq)], pw[i * tq:(i + 1) * tq], preferred_element_type=F32)

        o_cmps.append(o_cmp)
        o_wins.append(o_win)

    def slc_step(k0, carry, diagonal):
        new = []
        for n in range(N_KV):
            m, l, acc = carry[n]
            st = jnp.dot(kcat[pl.ds(k0, SLC_TILE), :], rhs_slcs[n], preferred_element_type=F32)
            if diagonal:
                kpos = k0 + lax.broadcasted_iota(jnp.int32, (SLC_TILE, 1), 0)
                st = jnp.where(kpos <= qpos4, st, NEG)
            m2 = jnp.maximum(m, jnp.max(st, axis=0, keepdims=True))
            a = jnp.exp(m - m2)
            e = jnp.exp(st - m2)
            l2 = a * l + jnp.sum(e, axis=0, keepdims=True)
            v_rows = vts[n * HEAD_DIM:(n + 1) * HEAD_DIM, pl.ds(k0, SLC_TILE)]
            new.append((m2, l2, a * acc + jnp.dot(v_rows, e.astype(BF16), preferred_element_type=F32)))
        return tuple(new)

    n_full = (j * tq) // SLC_TILE
    start = (jnp.full((1, cols4), NEG, F32), jnp.zeros((1, cols4), F32), jnp.zeros((HEAD_DIM, cols4), F32))
    carry = lax.fori_loop(
        0, n_full, lambda kt, c: slc_step(pl.multiple_of(kt * SLC_TILE, SLC_TILE), c, False), (start,) * N_KV)
    carry = slc_step(pl.multiple_of(n_full * SLC_TILE, SLC_TILE), carry, True)

    for n in range(N_KV):
        _, l_s, acc_s = carry[n]
        o_slc = acc_s / l_s
        for g in range(GROUP):
            h = GROUP * n + g
            cs = slice(g * tq, (g + 1) * tq)
            outs[h] = (gates_t[3 * h:3 * h + 1, :] * o_cmps[n][:, cs] + gates_t[3 * h + 1:3 * h + 2, :] * o_slc[:, cs]
                       + gates_t[3 * h + 2:3 * h + 3, :] * o_wins[n][:, cs])
    o_ref[...] = jnp.concatenate(outs, axis=0).T


def _nsa_prompt(q3, kv3, kc, vc, win3, sm3):
    b, t, _ = q3.shape
    assert t % SLC_TILE == 0 and t % Q_BLOCK == 0
    nc = t // CMP_BLOCK
    est = 2 * (2 * t * 256 * 4) + 4 * nc * KV_W * 4 + 8 * Q_BLOCK * 512 * 4 + 16 * 512 * 640 * 4
    kaux = _key_aux(t, lambda r: r, True)
    caux = _key_aux(nc, lambda r: r * CMP_BLOCK + (CMP_BLOCK - 1) / 2, False)
    return pl.pallas_call(
        functools.partial(_nsa_prompt_body, seq=t),
        grid=(b, t // Q_BLOCK),
        in_specs=[
            pl.BlockSpec((None, Q_BLOCK, 512), lambda i, j: (i, j, 0)),
            pl.BlockSpec((None, t, 2 * KV_W), lambda i, j: (i, 0, 1)),
            pl.BlockSpec((None, nc, KV_W), lambda i, j: (i, 0, 0)),
            pl.BlockSpec((None, nc, KV_W), lambda i, j: (i, 0, 0)),
            pl.BlockSpec((None, t, 2 * KV_W), lambda i, j: (i, 0, 0)),
            pl.BlockSpec((None, Q_BLOCK, SM_W), lambda i, j: (i, j, 0)),
            pl.BlockSpec((t, LANES), lambda i, j: (0, 0)),
            pl.BlockSpec((nc, LANES), lambda i, j: (0, 0)),
        ],
        out_specs=pl.BlockSpec((None, Q_BLOCK, 512), lambda i, j: (i, j, 0)),
        out_shape=jax.ShapeDtypeStruct((b, t, 512), F32),
        scratch_shapes=[pltpu.VMEM((nc, Q_BLOCK), F32),
                        pltpu.VMEM((t, 2 * KV_W), BF16), pltpu.VMEM((t, 2 * KV_W), BF16), pltpu.VMEM((nc, 2 * KV_W), BF16),
                        pltpu.VMEM((KV_W, t), BF16), pltpu.VMEM((KV_W, t), BF16), pltpu.VMEM((KV_W, nc), BF16)],
        compiler_params=_params(("parallel", "arbitrary"), est),
        name="nsa_prompt",
    )(q3, kv3, kc, vc, win3, sm3, kaux, caux)


_BNN = (((2,), (1,)), ((0,), (0,)))
_BNT = (((2,), (2,)), ((0,), (0,)))
_BTN = (((1,), (1,)), ((0,), (0,)))


def _bdot(a, b, dims=_BNN):
    return lax.dot_general(a.astype(BF16), b.astype(BF16), dims, preferred_element_type=F32)


def _solve_unit_lower(nmat, rhs):
    c = nmat.shape[1]
    ii = lax.broadcasted_iota(jnp.int32, (c, c), 0)
    jj = lax.broadcasted_iota(jnp.int32, (c, c), 1)
    tmat = jnp.where(ii == jj, 1.0, 0.0) - nmat
    pw = _bdot(nmat, nmat)
    steps = c.bit_length() - 2
    for _ in range(steps - 1):
        both = _bdot(jnp.concatenate([tmat, pw], axis=1), pw)
        tmat = tmat + both[:, 0:c]
        pw = both[:, c:2 * c]
    tmat = tmat + _bdot(tmat, pw)
    return _bdot(tmat, rhs)


def _l2_scale(x, ind, ind_t):
    ssq = _dot_sel(x * x, ind)
    return _dot_sel(lax.rsqrt(ssq + RMS_EPS), ind_t)


def _gdn_prompt_body(gqkv_ref, gz_ref, sm_ref, cw_ref, alog_r_ref, dtb_r_ref, alog_c_ref, dtb_c_ref, nw_ref,
                     o_ref, s_ref, xbuf):
    t = pl.program_id(1)
    c = GDN_CHUNK
    nt = gqkv_ref.shape[0]
    nchunk = nt // c
    pad = SUBLANES

    @pl.when(t == 0)
    def _():
        xbuf[0:pad, :] = jnp.zeros((pad, GDN_QKV), F32)
        s_ref[...] = jnp.zeros(s_ref.shape, F32)

    xbuf[pad:pad + nt, :] = gqkv_ref[...]
    y = cw_ref[GDN_CONV - 1:GDN_CONV, :] * xbuf[pad:pad + nt, :]
    for tap in range(GDN_CONV - 1):
        y = y + cw_ref[tap:tap + 1, :] * xbuf[pad - (GDN_CONV - 1) + tap:pad - (GDN_CONV - 1) + tap + nt, :]
    xbuf[0:pad, :] = xbuf[nt:nt + pad, :]
    act = _silu(y)
    ind = _head_indicator(GDN_W, N_GDN)
    ind_t = _head_indicator_t(N_GDN, GDN_W)
    qraw = act[:, 0:GDN_W]
    kraw = act[:, GDN_W:2 * GDN_W]
    v = act[:, 2 * GDN_W:3 * GDN_W]
    qn = qraw * _l2_scale(qraw, ind, ind_t) * (HEAD_DIM ** -0.5)
    kn = kraw * _l2_scale(kraw, ind, ind_t)

    sm = sm_ref[...]
    sm_t = sm.T
    g_col = -jnp.exp(alog_r_ref[...]) * _softplus(sm[:, SM_GA:SM_GA + N_GDN] + dtb_r_ref[...])
    g_row = -jnp.exp(alog_c_ref[...]) * _softplus(sm_t[SM_GA:SM_GA + N_GDN, :] + dtb_c_ref[...])
    beta = _sigmoid(sm[:, SM_GB:SM_GB + N_GDN])
    ti = lax.broadcasted_iota(jnp.int32, (nt, nt), 0)
    tj = lax.broadcasted_iota(jnp.int32, (nt, nt), 1)
    same = (ti // c) == (tj // c)
    gc_col = _dot_f32(jnp.where(same & (ti >= tj), 1.0, 0.0), g_col)
    gc_row = _dot_f32(g_row, jnp.where(same & (ti <= tj), 1.0, 0.0))

    items = [(g, h) for g in range(nchunk) for h in range(N_GDN)]
    rows = lambda g: slice(g * c, (g + 1) * c)
    heads = lambda x: jnp.stack([x[rows(g), h * HEAD_DIM:(h + 1) * HEAD_DIM] for g, h in items], axis=0)
    cols = lambda x: jnp.stack([x[rows(g), h:h + 1] for g, h in items], axis=0)
    qb, kb, vb = heads(qn), heads(kn), heads(v)
    gcc = cols(gc_col)
    bcol = cols(beta)
    gcr = jnp.stack([gc_row[h:h + 1, rows(g)] for g, h in items], axis=0)
    gl = jnp.stack([gc_row[h:h + 1, (g + 1) * c - 1:(g + 1) * c] for g, h in items], axis=0)
    ii = lax.broadcasted_iota(jnp.int32, (c, c), 0)
    jj = lax.broadcasted_iota(jnp.int32, (c, c), 1)
    dec = jnp.exp(jnp.where(ii >= jj, gcc - gcr, NEG))
    egc = jnp.exp(gcc)
    kbeta = kb * bcol
    nmat = jnp.where(ii > jj, _bdot(kbeta, kb, _BNT) * dec, 0.0)
    x = _solve_unit_lower(nmat, jnp.concatenate([vb * bcol, kbeta * egc], axis=2))
    val, kcd = x[:, :, 0:HEAD_DIM], x[:, :, HEAD_DIM:2 * HEAD_DIM]
    inner = _bdot(qb, kb, _BNT) * dec
    qg = qb * egc
    kend = kb * jnp.exp(gl - gcc)
    egl = jnp.exp(gl)

    s = s_ref[...]
    outs = []
    for g in range(nchunk):
        it = slice(g * N_GDN, (g + 1) * N_GDN)
        vn = val[it] - _bdot(kcd[it], s)
        og = _bdot(qg[it], s) + _bdot(inner[it], vn)
        s = s * egl[it] + _bdot(kend[it], vn, _BTN)
        outs.append(jnp.concatenate([og[h] for h in range(N_GDN)], axis=1))
    s_ref[...] = s
    o = jnp.concatenate(outs, axis=0)
    rms = _dot_sel(lax.rsqrt(_dot_sel(o * o, ind) * (1.0 / HEAD_DIM) + RMS_EPS), ind_t)
    o_ref[...] = o * rms * nw_ref[...] * _silu(gz_ref[...])


GDN_TOK = 4 * GDN_CHUNK


def _gdn_prompt(gqkv3, gz3, sm3, cw, alog, dtb, nw_t):
    b, t, _ = gqkv3.shape
    assert t % GDN_TOK == 0
    c = GDN_TOK
    est = 4 * c * GDN_QKV * 4 + 64 * c * 512 * 4
    return pl.pallas_call(
        _gdn_prompt_body,
        grid=(b, t // c),
        in_specs=[
            pl.BlockSpec((None, c, GDN_QKV), lambda i, j: (i, j, 0)),
            pl.BlockSpec((None, c, GDN_W), lambda i, j: (i, j, 0)),
            pl.BlockSpec((None, c, SM_W), lambda i, j: (i, j, 0)),
            _resident((GDN_CONV, GDN_QKV)), _resident((1, N_GDN)), _resident((1, N_GDN)),
            _resident((N_GDN, 1)), _resident((N_GDN, 1)), _resident((1, GDN_W)),
        ],
        out_specs=[pl.BlockSpec((None, c, GDN_W), lambda i, j: (i, j, 0)),
                   pl.BlockSpec((None, N_GDN, HEAD_DIM, HEAD_DIM), lambda i, j: (i, 0, 0, 0))],
        out_shape=[jax.ShapeDtypeStruct((b, t, GDN_W), F32),
                   jax.ShapeDtypeStruct((b, N_GDN, HEAD_DIM, HEAD_DIM), F32)],
        scratch_shapes=[pltpu.VMEM((c + SUBLANES, GDN_QKV), F32)],
        compiler_params=_params(("parallel", "arbitrary"), est),
        name="gdn_prompt",
    )(gqkv3, gz3, sm3, cw, alog.reshape(1, N_GDN), dtb.reshape(1, N_GDN),
      alog.reshape(N_GDN, 1), dtb.reshape(N_GDN, 1), nw_t)


def _post_tail(x1, ffn, p, lng_ref, lnb_ref, wg_ref, wp_ref):
    x2 = _layer_norm(DN_ALPHA * x1 + ffn, lng_ref[1:2, :], lnb_ref[1:2, :])
    ple = _sigmoid(jnp.dot(x2.astype(BF16), wg_ref[...], preferred_element_type=F32)) \
        * jnp.dot(p.astype(BF16), wp_ref[...], preferred_element_type=F32)
    return _layer_norm(DN_ALPHA * x2 + ple, lng_ref[2:3, :], lnb_ref[2:3, :])


def _post_head(onsa_ref, og_ref, x_ref, wo_ref, lng_ref, lnb_ref):
    attn = jnp.dot(onsa_ref[...].astype(BF16), wo_ref[0:512, :], preferred_element_type=F32) \
        + jnp.dot(og_ref[...].astype(BF16), wo_ref[512:1024, :], preferred_element_type=F32)
    return _layer_norm(DN_ALPHA * x_ref[...] + attn, lng_ref[0:1, :], lnb_ref[0:1, :])


def _post_prompt_body(onsa_ref, og_ref, x_ref, p_ref, wo_ref, lng_ref, lnb_ref, wu_ref, cw_ref, wd_ref,
                      wp_ref, wg_ref, y_ref, fbuf_ref, upbuf):
    t = pl.program_id(1)
    tm = x_ref.shape[0]
    pad = SUBLANES

    @pl.when(t == 0)
    def _():
        upbuf[0:pad, :] = jnp.zeros((pad, D_FF), F32)

    x1 = _post_head(onsa_ref, og_ref, x_ref, wo_ref, lng_ref, lnb_ref)
    x1b = x1.astype(BF16)
    ffn = jnp.zeros((tm, D_MODEL), F32)
    for cc in range(D_FF // FF_CHUNK):
        cs = slice(cc * FF_CHUNK, (cc + 1) * FF_CHUNK)
        up_g = jnp.dot(x1b, wu_ref[:, cs], preferred_element_type=F32)
        up_v = jnp.dot(x1b, wu_ref[:, D_FF + cc * FF_CHUNK:D_FF + (cc + 1) * FF_CHUNK], preferred_element_type=F32)
        upbuf[pad:pad + tm, cs] = up_g
        hg = cw_ref[FFN_CONV - 1:FFN_CONV, cs] * up_g
        for tap in range(FFN_CONV - 1):
            o0 = pad - (FFN_CONV - 1) + tap
            hg = hg + cw_ref[tap:tap + 1, cs] * upbuf[o0:o0 + tm, cs]
        ffn = ffn + jnp.dot((_gelu(hg) * up_v).astype(BF16), wd_ref[cs, :], preferred_element_type=F32)
    fbuf_ref[...] = upbuf[pad + tm - (FFN_CONV - 1):pad + tm, :]
    upbuf[0:pad, :] = upbuf[tm:tm + pad, :]
    y_ref[...] = _post_tail(x1, ffn, p_ref[...], lng_ref, lnb_ref, wg_ref, wp_ref)


def _post_decode_body(onsa_ref, og_ref, x_ref, p_ref, fst_ref, wo_ref, lng_ref, lnb_ref, wu_ref, cw_ref, wd_ref,
                      wp_ref, wg_ref, y_ref, fnew_ref):
    tm = x_ref.shape[0]
    x1 = _post_head(onsa_ref, og_ref, x_ref, wo_ref, lng_ref, lnb_ref)
    x1b = x1.astype(BF16)
    ffn = jnp.zeros((tm, D_MODEL), F32)
    for cc in range(D_FF // FF_CHUNK):
        cs = slice(cc * FF_CHUNK, (cc + 1) * FF_CHUNK)
        cs1 = slice(D_FF + cc * FF_CHUNK, D_FF + (cc + 1) * FF_CHUNK)
        up_g = jnp.dot(x1b, wu_ref[:, cs], preferred_element_type=F32)
        up_v = jnp.dot(x1b, wu_ref[:, cs1], preferred_element_type=F32)
        hg = cw_ref[0:1, cs] * fst_ref[:, cs] + cw_ref[1:2, cs] * fst_ref[:, cs1] + cw_ref[2:3, cs] * up_g
        fnew_ref[:, cs] = fst_ref[:, cs1]
        fnew_ref[:, cs1] = up_g
        ffn = ffn + jnp.dot((_gelu(hg) * up_v).astype(BF16), wd_ref[cs, :], preferred_element_type=F32)
    y_ref[...] = _post_tail(x1, ffn, p_ref[...], lng_ref, lnb_ref, wg_ref, wp_ref)


def _post_weight_specs():
    return [_resident((D_MODEL, D_MODEL)), _resident((3, D_MODEL)), _resident((3, D_MODEL)),
            _resident((D_MODEL, 2 * D_FF)), _resident((FFN_CONV, D_FF)), _resident((D_FF, D_MODEL)),
            _resident((PLE_DIM, D_MODEL)), _resident((D_MODEL, D_MODEL))]


_POST_WEIGHT_BYTES = 2 * (2 * D_MODEL * D_MODEL + 3 * D_MODEL * D_FF + PLE_DIM * D_MODEL) + 4 * 9 * D_FF


def _post_prompt(onsa3, og3, x3, p3, p_base, wts, tm):
    b, t, _ = x3.shape
    row = lambda n: pl.BlockSpec((None, tm, n), lambda i, j: (i, j, 0))
    p_spec = pl.BlockSpec((None, tm, PLE_DIM), lambda i, j: (i + p_base, j, 0))
    est = _POST_WEIGHT_BYTES + 2 * tm * (512 + 512 + 1024 + 256 + 1024) * 4 + (tm + 8) * D_FF * 4 + 12 * tm * 1024 * 4
    return pl.pallas_call(
        _post_prompt_body,
        grid=(b, t // tm),
        in_specs=[row(512), row(512), row(D_MODEL), p_spec] + _post_weight_specs(),
        out_specs=[row(D_MODEL), pl.BlockSpec((None, FFN_CONV - 1, D_FF), lambda i, j: (i, 0, 0))],
        out_shape=[jax.ShapeDtypeStruct((b, t, D_MODEL), F32),
                   jax.ShapeDtypeStruct((b, FFN_CONV - 1, D_FF), F32)],
        scratch_shapes=[pltpu.VMEM((tm + SUBLANES, D_FF), F32)],
        compiler_params=_params(("parallel", "arbitrary"), est),
        name="post_prompt",
    )(onsa3, og3, x3, p3, *wts)


def _post_decode(onsa2, og2, x2, p2, fst2, wts):
    m = x2.shape[0]
    full = lambda n: pl.BlockSpec((m, n), lambda i: (0, 0))
    est = _POST_WEIGHT_BYTES + 2 * m * (512 + 512 + 1024 + 256 + 1024 + 4 * D_FF) * 4 + 12 * m * 1024 * 4
    return pl.pallas_call(
        _post_decode_body,
        grid=(1,),
        in_specs=[full(512), full(512), full(D_MODEL), full(PLE_DIM), full(2 * D_FF)] + _post_weight_specs(),
        out_specs=[full(D_MODEL), full(2 * D_FF)],
        out_shape=[jax.ShapeDtypeStruct((m, D_MODEL), F32), jax.ShapeDtypeStruct((m, 2 * D_FF), F32)],
        compiler_params=_params(("arbitrary",), est),
        name="post_decode",
    )(onsa2, og2, x2, p2, fst2, *wts)


def _nsa_decode_body(pt_ref, *refs, n_pages, page, chained):
    page_refs = refs[:n_pages]
    (q_ref, kvn_ref, winn_ref, sm_ref, wbuf_ref, pe_ref, phi_ref) = refs[n_pages:n_pages + 7]
    o_ref, wout_ref = refs[n_pages + 7 + (1 if chained else 0):]
    del pt_ref
    past = n_pages * page
    ncv = past // CMP_BLOCK
    nsb = past // SLC_BLOCK
    assert ncv <= LANES and ncv % 2 == 0
    q = q_ref[...] * (HEAD_DIM ** -0.5)
    zero = jnp.zeros((1, HEAD_DIM), F32)
    rows = []
    for h in range(N_HEADS):
        piece = q[:, h * HEAD_DIM:(h + 1) * HEAD_DIM]
        rows.append(jnp.concatenate([piece, zero] if h < GROUP else [zero, piece], axis=1))
    qbd = jnp.concatenate(rows, axis=0)
    qbd_b = qbd.astype(BF16)
    hrow = lax.broadcasted_iota(jnp.int32, (N_HEADS, 1), 0)
    slope = jnp.concatenate([jnp.full((1, 1), 2.0 ** -(h + 1), F32) for h in range(N_HEADS)], axis=0)

    zero_hh = jnp.zeros((HEAD_DIM, HEAD_DIM), F32)
    both_heads = lambda w: jnp.concatenate([jnp.concatenate([w, zero_hh], axis=1),
                                            jnp.concatenate([zero_hh, w], axis=1)], axis=0)
    pe_row = lambda which: jnp.concatenate([jnp.mean(pe_ref[which], axis=0, keepdims=True)] * N_KV, axis=1)
    pool = jnp.where(lax.broadcasted_iota(jnp.int32, (past, ncv), 0) // CMP_BLOCK
                     == lax.broadcasted_iota(jnp.int32, (past, ncv), 1), 1.0, 0.0)
    pool_t = jnp.where(lax.broadcasted_iota(jnp.int32, (ncv, past), 1) // CMP_BLOCK
                       == lax.broadcasted_iota(jnp.int32, (ncv, past), 0), 1.0, 0.0).astype(BF16)
    q_phi = lax.dot_general(qbd, both_heads(phi_ref[0]), _NT, precision=lax.Precision.HIGHEST,
                            preferred_element_type=F32)
    row_s = jnp.concatenate([_dot(q_phi, pr[0:KV_W, :]) for pr in page_refs], axis=1)
    cidx = lax.broadcasted_iota(jnp.int32, (1, ncv), 1)
    c_mid = cidx.astype(F32) * CMP_BLOCK + (CMP_BLOCK - 1) / 2
    s = _dot_sel(row_s, pool) * (1.0 / CMP_BLOCK) + jnp.sum(q_phi * pe_row(0), axis=-1, keepdims=True) \
        - slope * (float(past) - c_mid)
    p = _softmax_rows(s)
    p_rows = jnp.dot(p.astype(BF16), pool_t, preferred_element_type=F32) * (1.0 / CMP_BLOCK)
    u = jnp.sum(p, axis=-1, keepdims=True) * pe_row(1)
    for i, pr in enumerate(page_refs):
        u = u + _dot_nt(p_rows[:, i * page:(i + 1) * page], pr[KV_W:2 * KV_W, :])
    o_cmp = jnp.dot(u, both_heads(phi_ref[1]), precision=lax.Precision.HIGHEST, preferred_element_type=F32)

    p_l = p if ncv == LANES else jnp.concatenate([p, jnp.zeros((N_HEADS, LANES - ncv), F32)], axis=1)
    pair = p_l + pltpu.roll(p_l, LANES - 1, 1)
    pooled = jnp.concatenate([jnp.sum(pair[n * GROUP:(n + 1) * GROUP], axis=0, keepdims=True) for n in range(N_KV)],
                             axis=0)
    lane = lax.broadcasted_iota(jnp.int32, (N_KV, LANES), 1)
    blk = lane // 2
    is_blk = ((lane % 2) == 0) & (lane < ncv)
    is_cur = lane == ncv
    forced = is_cur | (is_blk & ((blk == 0) | ((nsb - blk) < N_LOCAL)))
    score = jnp.where(forced, BIG, jnp.where(is_blk, pooled, -BIG))
    rank = jnp.zeros((N_KV, LANES), F32)
    cand = [2 * b for b in range(nsb)] + [ncv]
    for lc in cand:
        col = score[:, lc:lc + 1]
        rank = rank + jnp.where(lane > lc, jnp.where(col >= score, 1.0, 0.0), jnp.where(col > score, 1.0, 0.0))
    sel = jnp.where((rank < float(min(TOP_N, nsb + 1))) & (is_blk | is_cur), 1.0, 0.0)
    sel8 = jnp.concatenate([jnp.broadcast_to(sel[n:n + 1], (GROUP, LANES)) for n in range(N_KV)], axis=0).astype(BF16)

    kvn = kvn_ref[...]
    s_new = jnp.sum(qbd * kvn[:, 2 * KV_W:3 * KV_W], axis=-1, keepdims=True)
    s_parts = []
    for i, pr in enumerate(page_refs):
        tpos = i * page + lax.broadcasted_iota(jnp.int32, (1, page), 1)
        member = jnp.where((2 * ((i * page + lax.broadcasted_iota(jnp.int32, (LANES, page), 1)) // SLC_BLOCK))
                           == lax.broadcasted_iota(jnp.int32, (LANES, page), 0), 1.0, 0.0).astype(BF16)
        chosen = jnp.dot(sel8, member, preferred_element_type=F32)
        st = _dot(qbd_b, pr[2 * KV_W:3 * KV_W, :]) - slope * (float(past) - tpos.astype(F32))
        s_parts.append(jnp.where(chosen > 0.5, st, NEG))
    s_all = jnp.concatenate(s_parts, axis=1)
    m = jnp.maximum(jnp.max(s_all, axis=-1, keepdims=True), s_new)
    e = jnp.exp(s_all - m)
    e_new = jnp.exp(s_new - m)
    acc = e_new * kvn[:, 3 * KV_W:4 * KV_W]
    for i, pr in enumerate(page_refs):
        acc = acc + _dot_nt(e[:, i * page:(i + 1) * page], pr[3 * KV_W:4 * KV_W, :])
    o_slc = acc / (jnp.sum(e, axis=-1, keepdims=True) + e_new)

    wb = wbuf_ref.shape[1]
    winn = winn_ref[...]
    rpos = lax.broadcasted_iota(jnp.int32, (1, wb), 1)
    dist = wb - rpos
    sw = _dot(qbd_b, wbuf_ref[0:KV_W, :]) - slope * dist.astype(F32)
    sw = jnp.where(dist < WINDOW, sw, NEG)
    sw_new = jnp.sum(qbd * winn[:, 0:KV_W], axis=-1, keepdims=True)
    mw = jnp.maximum(jnp.max(sw, axis=-1, keepdims=True), sw_new)
    ew = jnp.exp(sw - mw)
    ew_new = jnp.exp(sw_new - mw)
    o_win = (_dot_nt(ew, wbuf_ref[KV_W:2 * KV_W, :]) + ew_new * winn[:, KV_W:2 * KV_W]) \
        / (jnp.sum(ew, axis=-1, keepdims=True) + ew_new)
    winn_col = jnp.broadcast_to(winn, (SUBLANES, 2 * KV_W)).T[:, 0:1]
    shifted = pltpu.roll(wbuf_ref[...], wb - 1, 1)
    wout_ref[...] = jnp.where(lax.broadcasted_iota(jnp.int32, (2 * KV_W, wb), 1) == wb - 1, winn_col, shifted)

    gates = _sigmoid(sm_ref[:, 0:3 * N_HEADS])
    glane = lax.broadcasted_iota(jnp.int32, (N_HEADS, 3 * N_HEADS), 1)
    gsel = lambda r: jnp.sum(jnp.where(glane == 3 * hrow + r, gates, 0.0), axis=-1, keepdims=True)
    o8 = gsel(0) * o_cmp + gsel(1) * o_slc + gsel(2) * o_win
    pieces = []
    for h in range(N_HEADS):
        n = h // GROUP
        pieces.append(o8[h:h + 1, n * HEAD_DIM:(n + 1) * HEAD_DIM])
    o_ref[...] = jnp.concatenate(pieces, axis=1)


def _nsa_decode(page_table, cache3, page_base, q3, kvn3, winn3, sm3, wbuf3, seq_base, pe, phi, wout_prev):
    nb, n_pages = page_table.shape
    page = cache3.shape[2]
    wb = wbuf3.shape[2]
    assert wb == WINDOW and n_pages * page >= wb
    chained = wout_prev is not None
    page_specs = [pl.BlockSpec((None, 4 * KV_W, page),
                               functools.partial(lambda i, pt, k: (pt[i, k] + page_base, 0, 0), k=k))
                  for k in range(n_pages)]
    one = lambda n: pl.BlockSpec((None, 1, n), lambda i, pt: (i, 0, 0))
    win_spec = pl.BlockSpec((None, 2 * KV_W, wb), lambda i, pt: (i + seq_base, 0, 0))
    in_specs = page_specs + [one(512), one(512), one(256), one(SM_W), win_spec,
                             pl.BlockSpec((2, CMP_BLOCK, HEAD_DIM), lambda i, pt: (0, 0, 0)),
                             pl.BlockSpec((2, HEAD_DIM, HEAD_DIM), lambda i, pt: (0, 0, 0))]
    args = [page_table] + [cache3] * n_pages + [q3, kvn3, winn3, sm3, wbuf3, pe, phi]
    aliases = {}
    if chained:
        in_specs.append(pl.BlockSpec(memory_space=pl.ANY))
        args.append(wout_prev)
        aliases = {len(args) - 1: 1}
    grid_spec = pltpu.PrefetchScalarGridSpec(
        num_scalar_prefetch=1, grid=(nb,), in_specs=in_specs, out_specs=[one(512), win_spec])
    est = 2 * n_pages * page * 512 * 4 + 4 * wb * 256 * 4 + 8 * 1024 * 1024
    return pl.pallas_call(
        functools.partial(_nsa_decode_body, n_pages=n_pages, page=page, chained=chained),
        grid_spec=grid_spec,
        out_shape=[jax.ShapeDtypeStruct((nb, 1, 512), F32), jax.ShapeDtypeStruct(wbuf3.shape, F32)],
        input_output_aliases=aliases,
        compiler_params=_params(("arbitrary",), est),
        name="nsa_decode",
    )(*args)


def _gdn_dec_prep_body(gqkv_ref, cst_ref, sm_ref, cw_ref, alog_ref, dtb_ref, qn_ref, kn_ref, v_ref, db_ref, cnew_ref):
    w = GDN_QKV
    xn = gqkv_ref[...]
    y = cw_ref[GDN_CONV - 1:GDN_CONV, :] * xn
    for tap in range(GDN_CONV - 1):
        y = y + cw_ref[tap:tap + 1, :] * cst_ref[:, tap * w:(tap + 1) * w]
    for tap in range(GDN_CONV - 2):
        cnew_ref[:, tap * w:(tap + 1) * w] = cst_ref[:, (tap + 1) * w:(tap + 2) * w]
    cnew_ref[:, (GDN_CONV - 2) * w:(GDN_CONV - 1) * w] = xn
    act = _silu(y)
    ind = _head_indicator(GDN_W, N_GDN)
    ind_t = _head_indicator_t(N_GDN, GDN_W)
    qraw = act[:, 0:GDN_W]
    kraw = act[:, GDN_W:2 * GDN_W]
    qn_ref[...] = (qraw * _l2_scale(qraw, ind, ind_t) * (HEAD_DIM ** -0.5)).T
    kn_ref[...] = (kraw * _l2_scale(kraw, ind, ind_t)).T
    v_ref[...] = act[:, 2 * GDN_W:3 * GDN_W].T
    sm = sm_ref[...]
    g = -jnp.exp(alog_ref[...]) * _softplus(sm[:, SM_GA:SM_GA + N_GDN] + dtb_ref[...])
    beta = _sigmoid(sm[:, SM_GB:SM_GB + N_GDN])
    m = sm.shape[0]
    db_ref[...] = jnp.concatenate([jnp.exp(g), beta, jnp.zeros((m, SM_W - 2 * N_GDN), F32)], axis=1).T


def _gdn_dec_prep(gqkv2, cst2, sm2, cw, alog, dtb):
    m = gqkv2.shape[0]
    full = lambda n: pl.BlockSpec((m, n), lambda i: (0, 0))
    full_t = lambda n: pl.BlockSpec((n, m), lambda i: (0, 0))
    est = 2 * m * (8 * GDN_QKV + 4 * 512) * 4 + 8 * m * GDN_QKV * 4
    return pl.pallas_call(
        _gdn_dec_prep_body,
        grid=(1,),
        in_specs=[full(GDN_QKV), full((GDN_CONV - 1) * GDN_QKV), full(SM_W),
                  pl.BlockSpec((GDN_CONV, GDN_QKV), lambda i: (0, 0)),
                  pl.BlockSpec((1, N_GDN), lambda i: (0, 0)), pl.BlockSpec((1, N_GDN), lambda i: (0, 0))],
        out_specs=[full_t(GDN_W), full_t(GDN_W), full_t(GDN_W), full_t(SM_W), full((GDN_CONV - 1) * GDN_QKV)],
        out_shape=[jax.ShapeDtypeStruct((GDN_W, m), F32)] * 3 + [jax.ShapeDtypeStruct((SM_W, m), F32),
                   jax.ShapeDtypeStruct((m, (GDN_CONV - 1) * GDN_QKV), F32)],
        compiler_params=_params(("arbitrary",), est),
        name="gdn_decode_prep",
    )(gqkv2, cst2, sm2, cw, alog.reshape(1, N_GDN), dtb.reshape(1, N_GDN))


def _gdn_dec_step_body(qn_ref, kn_ref, v_ref, db_ref, gz_ref, s_ref, nw_ref, *out_refs):
    o_ref, snew_ref = out_refs[-2:]
    h = pl.program_id(0)
    a = db_ref[pl.ds(h, 1), :]
    bta = db_ref[pl.ds(N_GDN + h, 1), :]
    sk = jnp.zeros(v_ref.shape, F32)
    for d in range(HEAD_DIM):
        sk = sk + kn_ref[d:d + 1, :] * s_ref[d]
    vn = bta * (v_ref[...] - a * sk)
    o = jnp.zeros(v_ref.shape, F32)
    for d in range(HEAD_DIM):
        s_d = a * s_ref[d] + kn_ref[d:d + 1, :] * vn
        snew_ref[d] = s_d
        o = o + qn_ref[d:d + 1, :] * s_d
    o = o * lax.rsqrt(jnp.mean(o * o, axis=0, keepdims=True) + RMS_EPS) * nw_ref[...]
    o_ref[...] = o * _silu(gz_ref[...])


def _gdn_dec_step(qn_t, kn_t, v_t, db_t, gz_t, s4, layer_base, nw_col, s_prev):
    nb = qn_t.shape[1]
    head = lambda: pl.BlockSpec((HEAD_DIM, nb), lambda i: (i, 0))
    st = pl.BlockSpec((None, HEAD_DIM, HEAD_DIM, nb), lambda i: (i + layer_base, 0, 0, 0))
    in_specs = [head(), head(), head(), pl.BlockSpec((SM_W, nb), lambda i: (0, 0)), head(), st,
                pl.BlockSpec((HEAD_DIM, 1), lambda i: (0, 0))]
    args = [qn_t, kn_t, v_t, db_t, gz_t, s4, nw_col]
    aliases = {}
    if s_prev is not None:
        in_specs.append(pl.BlockSpec(memory_space=pl.ANY))
        args.append(s_prev)
        aliases = {len(args) - 1: 1}
    est = 4 * HEAD_DIM * HEAD_DIM * nb * 4 + 4 * 1024 * 1024
    return pl.pallas_call(
        _gdn_dec_step_body,
        grid=(N_GDN,),
        in_specs=in_specs,
        out_specs=[head(), st],
        out_shape=[jax.ShapeDtypeStruct((GDN_W, nb), F32), jax.ShapeDtypeStruct(s4.shape, F32)],
        input_output_aliases=aliases,
        compiler_params=_params(("parallel",), est),
        name="gdn_decode_step",
    )(*args)


def _prep_w_in(w):
    small = jnp.concatenate([w[:, C_GATE:C_GQKV], w[:, C_GA:C_GB], w[:, C_GB:C_GZ],
                             jnp.zeros((D_MODEL, SM_W - (C_GQKV - C_GATE) - 2 * N_GDN), w.dtype)], axis=1)
    return jnp.concatenate([w[:, 0:C_KV], w[:, C_KV:C_WIN], w[:, C_WIN:C_GATE], w[:, C_GQKV:C_GA],
                            w[:, C_GZ:IN_WIDTH], small], axis=1).astype(BF16)


def kernel(x_prompt, x_sample, cache_nsa_kv, state_nsa_win, state_gdn, state_gdn_conv, state_ffn_conv, page_table, p_prompt, p_sample, w_in, nsa_pe, nsa_phi, gdn_conv_w, gdn_A_log, gdn_dt_bias, gdn_norm_w, w_out, ln_g, ln_b, ffn_w_up, ffn_conv_w, ffn_w_down, ple_w_proj, ple_w_gate):
    b, t, d = x_prompt.shape
    nb, ts, _ = x_sample.shape
    assert d == D_MODEL and ts == 1 and t % Q_BLOCK == 0 and t >= WINDOW
    depth = w_in.shape[0]
    n_pool, page = cache_nsa_kv.shape[1], cache_nsa_kv.shape[2]
    wb = state_nsa_win.shape[2]
    tm_p = 512
    xp, xs = x_prompt, x_sample.reshape(nb, d)
    cache_all = cache_nsa_kv.transpose(0, 1, 3, 4, 5, 2).reshape(depth * n_pool, 4 * KV_W, page)
    wbuf_all = state_nsa_win.transpose(0, 1, 3, 4, 5, 2).reshape(depth * nb, 2 * KV_W, wb)
    gstate_all = state_gdn.transpose(0, 2, 3, 4, 1).reshape(depth * N_GDN, HEAD_DIM, HEAD_DIM, nb)
    p_all = p_prompt.reshape(depth * b, t, PLE_DIM)
    win_all = None
    gs_all = None
    st_p, st_s = [], []
    for l in range(depth):
        w_cat = _prep_w_in(w_in[l])
        post_w = (w_out[l].astype(BF16), ln_g[l], ln_b[l], ffn_w_up[l].astype(BF16), ffn_conv_w[l],
                  ffn_w_down[l].astype(BF16), ple_w_proj[l].astype(BF16), ple_w_gate[l].astype(BF16))
        nw_t = jnp.tile(gdn_norm_w[l].reshape(1, HEAD_DIM), (1, N_GDN))

        q, kv, win, gqkv, gz, sm = _proj_in(xp.reshape(b * t, d), w_cat, 512)
        kv3, win3 = kv.reshape(b, t, 512), win.reshape(b, t, 256)
        sm3, gqkv3 = sm.reshape(b, t, SM_W), gqkv.reshape(b, t, GDN_QKV)
        kc, vc = _compress(kv3, nsa_pe[l], nsa_phi[l])
        o_nsa = _nsa_prompt(q.reshape(b, t, 512), kv3, kc, vc, win3, sm3)
        o_g, s_p = _gdn_prompt(gqkv3, gz.reshape(b, t, GDN_W), sm3, gdn_conv_w[l], gdn_A_log[l], gdn_dt_bias[l], nw_t)
        xp, fbuf_p = _post_prompt(o_nsa, o_g, xp, p_all, l * b, post_w, tm_p)
        st_p.append((kv3.reshape(b, t, 4, N_KV, HEAD_DIM), win3[:, t - wb:].reshape(b, wb, 2, N_KV, HEAD_DIM),
                     s_p, gqkv3[:, t - (GDN_CONV - 1):], fbuf_p))

        q, kv, win, gqkv, gz, sm = _proj_in(xs, w_cat, nb)
        o_nsa, win_all = _nsa_decode(
            page_table, cache_all, l * n_pool, q.reshape(nb, 1, 512), kv.reshape(nb, 1, 512),
            win.reshape(nb, 1, 256), sm.reshape(nb, 1, SM_W), wbuf_all, l * nb, nsa_pe[l], nsa_phi[l], win_all)
        qn_t, kn_t, v_t, db_t, cnew = _gdn_dec_prep(gqkv, state_gdn_conv[l].reshape(nb, (GDN_CONV - 1) * GDN_QKV), sm,
                                                    gdn_conv_w[l], gdn_A_log[l], gdn_dt_bias[l])
        o_g_t, gs_all = _gdn_dec_step(qn_t, kn_t, v_t, db_t, gz.T, gstate_all, l * N_GDN,
                                      gdn_norm_w[l].reshape(HEAD_DIM, 1), gs_all)
        xs, fnew = _post_decode(o_nsa.reshape(nb, 512), o_g_t.T, xs, p_sample[l].reshape(nb, PLE_DIM),
                                state_ffn_conv[l].reshape(nb, (FFN_CONV - 1) * D_FF), post_w)
        st_s.append((kv.reshape(nb, 1, 4, N_KV, HEAD_DIM), None, None, cnew.reshape(nb, GDN_CONV - 1, GDN_QKV),
                     fnew.reshape(nb, FFN_CONV - 1, D_FF)))

    sample_states = {
        1: win_all.reshape(depth, nb, 2, N_KV, HEAD_DIM, wb).transpose(0, 1, 5, 2, 3, 4),
        2: gs_all.reshape(depth, N_GDN, HEAD_DIM, HEAD_DIM, nb).transpose(0, 4, 1, 2, 3),
    }
    outs = [xp, xs.reshape(nb, 1, d)]
    for k in range(5):
        outs.append(jnp.stack([s[k] for s in st_p]))
        outs.append(sample_states[k] if k in sample_states else jnp.stack([s[k] for s in st_s]))
    return tuple(outs)
```

```python
import functools

import jax
import jax.numpy as jnp
from jax import lax
from jax.experimental import pallas as pl
from jax.experimental.pallas import tpu as pltpu

F32 = jnp.float32
BF16 = jnp.bfloat16

D_MODEL = 1024
DEPTH = 2
HEAD_DIM = 64
N_HEADS = 8
N_KV = 2
GROUP = 4
N_GDN = 8
KV_W = N_KV * HEAD_DIM
GDN_W = N_GDN * HEAD_DIM
GDN_QKV = 3 * GDN_W
CMP_BLOCK = 32
SLC_BLOCK = 64
TOP_N = 16
N_LOCAL = 2
WINDOW = 512
Q_BLOCK = 128
GDN_CHUNK = 64
GDN_CONV = 4
FFN_CONV = 3
D_FF = 2816
PLE_DIM = 256
LN_EPS = 1e-5
RMS_EPS = 1e-6
DN_ALPHA = (2 * DEPTH) ** 0.25
NEG = -1e30
BIG = 1e30

C_KV = 512
C_WIN = 1024
C_GATE = 1280
C_GQKV = 1304
C_GA = 2840
C_GB = 2848
C_GZ = 2856
IN_WIDTH = 3368
SM_W = 128
SM_GA = 24
SM_GB = 32

LANES = 128
SUBLANES = 8
VMEM_CAP = 56 * 1024 * 1024
FF_CHUNK = 256
SLC_TILE = 512
DEC_SEQS = 4

_NT = (((1,), (1,)), ((), ()))
_TN = (((0,), (0,)), ((), ()))


def _dot(a, b):
    return jnp.dot(a.astype(BF16), b.astype(BF16), preferred_element_type=F32)


def _dot_nt(a, b):
    return lax.dot_general(a.astype(BF16), b.astype(BF16), _NT, preferred_element_type=F32)


def _dot_tn(a, b):
    return lax.dot_general(a.astype(BF16), b.astype(BF16), _TN, preferred_element_type=F32)


def _dot_f32(a, b):
    return jnp.dot(a, b, precision=lax.Precision.HIGHEST, preferred_element_type=F32)


def _split3(a):
    hi = a.astype(BF16)
    r = a - hi.astype(F32)
    mid = r.astype(BF16)
    lo = (r - mid.astype(F32)).astype(BF16)
    return hi, mid, lo


def _dot_sel(a, sel):
    hi, mid, lo = _split3(a)
    sel = sel.astype(BF16)
    d = lambda u: jnp.dot(u, sel, preferred_element_type=F32)
    return d(hi) + d(mid) + d(lo)


def _dot_tn_sel(sel, a):
    hi, mid, lo = _split3(a)
    sel = sel.astype(BF16)
    d = lambda u: lax.dot_general(sel, u, _TN, preferred_element_type=F32)
    return d(hi) + d(mid) + d(lo)


def _sigmoid(x):
    return 1.0 / (1.0 + jnp.exp(-x))


def _silu(x):
    return x * _sigmoid(x)


def _softplus(x):
    return jnp.maximum(x, 0.0) + jnp.log1p(jnp.exp(-jnp.abs(x)))


def _layer_norm(x, g, b):
    mu = jnp.mean(x, axis=-1, keepdims=True)
    xc = x - mu
    var = jnp.mean(xc * xc, axis=-1, keepdims=True)
    return xc * lax.rsqrt(var + LN_EPS) * g + b


def _gelu(x):
    return 0.5 * x * (1.0 + lax.erf(x * (0.5 ** 0.5)))


def _head_indicator(width, heads):
    r = lax.broadcasted_iota(jnp.int32, (width, heads), 0) // HEAD_DIM
    c = lax.broadcasted_iota(jnp.int32, (width, heads), 1)
    return jnp.where(r == c, 1.0, 0.0).astype(BF16)


def _head_indicator_t(heads, width):
    r = lax.broadcasted_iota(jnp.int32, (heads, width), 0)
    c = lax.broadcasted_iota(jnp.int32, (heads, width), 1) // HEAD_DIM
    return jnp.where(r == c, 1.0, 0.0).astype(BF16)


def _params(sem, est_bytes):
    limit = int(min(max(2 * est_bytes, 32 * 1024 * 1024), VMEM_CAP))
    return pltpu.CompilerParams(dimension_semantics=sem, vmem_limit_bytes=limit)


def _resident(shape):
    nd = len(shape)
    return pl.BlockSpec(shape, lambda *_: (0,) * nd, pipeline_mode=pl.Buffered(1))


_PROJ_WIDTHS = (512, 512, 256, GDN_QKV, GDN_W, SM_W)


def _proj_in_body(x_ref, w_ref, *out_refs):
    xb = x_ref[...].astype(BF16)
    off = 0
    for ref in out_refs:
        n = ref.shape[-1]
        ref[...] = jnp.dot(xb, w_ref[:, off:off + n], preferred_element_type=F32)
        off += n


def _proj_in(x2, w_cat, tm):
    m = x2.shape[0]
    tot = sum(_PROJ_WIDTHS)
    est = 2 * tm * D_MODEL * 4 + D_MODEL * tot * 2 + 2 * tm * tot * 4
    return pl.pallas_call(
        _proj_in_body,
        grid=(m // tm,),
        in_specs=[pl.BlockSpec((tm, D_MODEL), lambda i: (i, 0)), _resident((D_MODEL, tot))],
        out_specs=[pl.BlockSpec((tm, n), lambda i: (i, 0)) for n in _PROJ_WIDTHS],
        out_shape=[jax.ShapeDtypeStruct((m, n), F32) for n in _PROJ_WIDTHS],
        compiler_params=_params(("parallel",), est),
        name="proj_in",
    )(x2, w_cat)


def _compress_rows(rows, pe, phi):
    nblk = rows.shape[0] // CMP_BLOCK
    mean = rows.reshape(nblk, CMP_BLOCK, KV_W).sum(axis=1) * (1.0 / CMP_BLOCK)
    pem = jnp.mean(pe, axis=0, keepdims=True)
    outs = []
    for n in range(N_KV):
        outs.append(_dot_f32(mean[:, n * HEAD_DIM:(n + 1) * HEAD_DIM] + pem, phi))
    return jnp.concatenate(outs, axis=1)


def _compress_body(kv_ref, pe_ref, phi_ref, kc_ref, vc_ref):
    kc_ref[...] = _compress_rows(kv_ref[:, 0:KV_W], pe_ref[0], phi_ref[0])
    vc_ref[...] = _compress_rows(kv_ref[:, KV_W:2 * KV_W], pe_ref[1], phi_ref[1])


def _compress(kv3, pe, phi):
    b, t, _ = kv3.shape
    nc = t // CMP_BLOCK
    est = 2 * t * 256 * 4 + 4 * nc * KV_W * 4
    return pl.pallas_call(
        _compress_body,
        grid=(b,),
        in_specs=[pl.BlockSpec((None, t, 2 * KV_W), lambda i: (i, 0, 0)),
                  _resident((2, CMP_BLOCK, HEAD_DIM)), _resident((2, HEAD_DIM, HEAD_DIM))],
        out_specs=[pl.BlockSpec((None, nc, KV_W), lambda i: (i, 0, 0))] * 2,
        out_shape=[jax.ShapeDtypeStruct((b, nc, KV_W), F32)] * 2,
        compiler_params=_params(("parallel",), est),
        name="nsa_compress",
    )(kv3, pe, phi)


def _softmax_rows(s):
    m = jnp.max(s, axis=-1, keepdims=True)
    e = jnp.exp(s - m)
    return e / jnp.sum(e, axis=-1, keepdims=True)


def _key_aux(npos, pos_of_row, with_blocks):
    pos = pos_of_row(lax.broadcasted_iota(jnp.int32, (npos, LANES), 0).astype(F32))
    lane = lax.broadcasted_iota(jnp.int32, (npos, LANES), 1)
    hi = jnp.floor(pos * (1.0 / SLC_BLOCK))
    aux = jnp.where(lane == AUX0, hi, jnp.where(lane == AUX0 + 1, pos - hi * SLC_BLOCK,
                    jnp.where((lane == AUX0 + 2) | (lane == AUX0 + 3), 1.0, 0.0)))
    if with_blocks:
        aux = jnp.where(lane < AUX0, jnp.where(hi == lane.astype(F32), NEG, 0.0), aux)
    return aux.astype(BF16)


AUX0 = 64


def _softmax_cols(s):
    m = jnp.max(s, axis=0, keepdims=True)
    e = jnp.exp(s - m)
    return e / jnp.sum(e, axis=0, keepdims=True)


def _nsa_prompt_body(q_ref, slc_ref, kc_ref, vc_ref, win_ref, sm_ref, kaux_ref, caux_ref, o_ref,
                     imp_ref, kcat, wcat, ccat, vts, vtw, vtc, *, seq):
    j = pl.program_id(1)
    nc = seq // CMP_BLOCK
    ns = seq // SLC_BLOCK
    tq = Q_BLOCK
    cols4 = GROUP * tq
    assert ns <= AUX0 and seq <= 256 * SLC_BLOCK

    @pl.when(j == 0)
    def _():
        for c0 in range(0, seq, SLC_TILE):
            rs = slice(c0, c0 + SLC_TILE)
            kcat[rs, 0:KV_W] = slc_ref[rs, 0:KV_W].astype(BF16)
            kcat[rs, KV_W:2 * KV_W] = kaux_ref[rs, :]
            wcat[rs, 0:KV_W] = win_ref[rs, 0:KV_W].astype(BF16)
            wcat[rs, KV_W:2 * KV_W] = kaux_ref[rs, :]
            vts[:, rs] = slc_ref[rs, KV_W:2 * KV_W].T.astype(BF16)
            vtw[:, rs] = win_ref[rs, KV_W:2 * KV_W].T.astype(BF16)
        ccat[:, 0:KV_W] = kc_ref[...].astype(BF16)
        ccat[:, KV_W:2 * KV_W] = caux_ref[...]
        vtc[...] = vc_ref[...].T.astype(BF16)

    q_t = (q_ref[...] * (HEAD_DIM ** -0.5)).T
    gates_t = _sigmoid(sm_ref[...].T[0:3 * N_HEADS, :])
    qpos_r = j * tq + lax.broadcasted_iota(jnp.int32, (1, tq), 1)
    qpos4 = jnp.concatenate([qpos_r] * GROUP, axis=1)
    qa = (qpos4 // SLC_BLOCK).astype(F32)
    qb = (qpos4 % SLC_BLOCK).astype(F32)
    arow = lax.broadcasted_iota(jnp.int32, (SUBLANES, cols4), 0)
    t_q = lax.broadcasted_iota(jnp.int32, (tq, cols4), 1) % tq
    t_k = lax.broadcasted_iota(jnp.int32, (tq, cols4), 0)
    zeros_h = jnp.zeros((HEAD_DIM, cols4), F32)
    outs = [None] * N_HEADS
    rhs_slcs, o_cmps, o_wins = [], [], []
    for n in range(N_KV):
        hs = slice(n * HEAD_DIM, (n + 1) * HEAD_DIM)
        qn_t = jnp.concatenate(
            [q_t[(GROUP * n + g) * HEAD_DIM:(GROUP * n + g + 1) * HEAD_DIM, :] for g in range(GROUP)], axis=1)
        slope = jnp.concatenate(
            [jnp.full((1, tq), 2.0 ** -(GROUP * n + g + 1), F32) for g in range(GROUP)], axis=1)
        q_top = jnp.concatenate([qn_t, zeros_h] if n == 0 else [zeros_h, qn_t], axis=0)
        q_pos = jnp.where(arow == 0, slope * SLC_BLOCK,
                          jnp.where(arow == 1, slope,
                                    jnp.where(arow == 2, -slope * SLC_BLOCK * qa,
                                              jnp.where(arow == 3, -slope * qb, 0.0))))
        q_tail = jnp.zeros((2 * KV_W - KV_W - AUX0 - SUBLANES, cols4), F32)

        def rhs(block_rows):
            return jnp.concatenate([q_top, block_rows, q_pos, q_tail], axis=0).astype(BF16)

        rhs_plain = rhs(jnp.zeros((AUX0, cols4), F32))

        c_end = lax.broadcasted_iota(jnp.int32, (nc, 1), 0) * CMP_BLOCK + (CMP_BLOCK - 1)
        c_vis = c_end <= qpos4
        s = jnp.dot(ccat[...], rhs_plain, preferred_element_type=F32)
        p = jnp.where(c_vis, _softmax_cols(jnp.where(c_vis, s, NEG)), 0.0)
        o_cmp = jnp.dot(vtc[hs, :], p.astype(BF16), preferred_element_type=F32)

        imp_ref[...] = p[:, 0:tq] + p[:, tq:2 * tq] + p[:, 2 * tq:3 * tq] + p[:, 3 * tq:4 * tq]
        impb = imp_ref[pl.ds(0, ns, stride=2), :] + imp_ref[pl.ds(1, ns, stride=2), :]
        blk = lax.broadcasted_iota(jnp.int32, (ns, tq), 0)
        cur = qpos_r // SLC_BLOCK
        future = blk > cur
        forced = (blk == 0) | (((cur - blk) < N_LOCAL) & jnp.logical_not(future))
        score = jnp.where(future, -BIG, jnp.where(forced, BIG, impb))
        rank = jnp.zeros((ns, tq), F32)
        for bp in range(ns):
            row = score[bp:bp + 1, :]
            rank = rank + jnp.where(blk > bp, jnp.where(row >= score, 1.0, 0.0), jnp.where(row > score, 1.0, 0.0))
        unsel_t = jnp.where(rank < float(min(TOP_N, ns)), 0.0, 1.0)
        if ns < AUX0:
            unsel_t = jnp.concatenate([unsel_t, jnp.zeros((AUX0 - ns, tq), F32)], axis=0)
        rhs_slcs.append(rhs(jnp.concatenate([unsel_t] * GROUP, axis=1)))

        nwb = WINDOW // tq + 1
        s_parts, starts = [], []
        for i in range(nwb):
            kb = j - (nwb - 1) + i
            r0 = pl.multiple_of(jnp.maximum(kb, 0) * tq, tq)
            sw = jnp.dot(wcat[pl.ds(r0, tq), :], rhs_plain, preferred_element_type=F32)
            if i == 0:
                sw = jnp.where(t_k > t_q, sw, NEG)
            if i == nwb - 1:
                sw = jnp.where(t_k <= t_q, sw, NEG)
            else:
                sw = jnp.where(kb >= 0, sw, NEG)
            s_parts.append(sw)
            starts.append(r0)
        pw = _softmax_cols(jnp.concatenate(s_parts, axis=0)).astype(BF16)
        o_win = jnp.dot(vtw[hs, pl.ds(starts[0], tq)], pw[0:tq], preferred_element_type=F32)
        for i in range(1, nwb):
            o_win = o_win + jnp.dot(vtw[hs, pl.ds(starts[i], tq)], pw[i * tq:(i + 1) * tq], preferred_element_type=F32)

        o_cmps.append(o_cmp)
        o_wins.append(o_win)

    heads = range(N_KV)

    def slc_scores(k0):
        k_rows = kcat[pl.ds(k0, SLC_TILE), :]
        return tuple(jnp.dot(k_rows, rhs_slcs[n], preferred_element_type=F32) for n in heads)

    def slc_absorb(k0, sts, stats, diagonal):
        if diagonal:
            kpos = k0 + lax.broadcasted_iota(jnp.int32, (SLC_TILE, 1), 0)
            sts = [jnp.where(kpos <= qpos4, st, NEG) for st in sts]
        m2s = [jnp.maximum(stats[n][0], jnp.max(sts[n], axis=0, keepdims=True)) for n in heads]
        es = [jnp.exp(sts[n] - m2s[n]) for n in heads]
        pvs = [jnp.dot(vts[n * HEAD_DIM:(n + 1) * HEAD_DIM, pl.ds(k0, SLC_TILE)], es[n].astype(BF16),
                       preferred_element_type=F32) for n in heads]
        new = []
        for n in heads:
            m, l, acc = stats[n]
            a = jnp.exp(m - m2s[n])
            new.append((m2s[n], a * l + jnp.sum(es[n], axis=0, keepdims=True), a * acc + pvs[n]))
        return tuple(new)

    def slc_iter(kt, stats):
        k0 = pl.multiple_of(kt * SLC_TILE, SLC_TILE)
        return slc_absorb(k0, slc_scores(k0), stats, False)

    n_full = (j * tq) // SLC_TILE
    start = (jnp.full((1, cols4), NEG, F32), jnp.zeros((1, cols4), F32), jnp.zeros((HEAD_DIM, cols4), F32))
    stats = lax.fori_loop(0, n_full, slc_iter, (start,) * N_KV)
    k_diag = pl.multiple_of(n_full * SLC_TILE, SLC_TILE)
    carry = slc_absorb(k_diag, slc_scores(k_diag), stats, True)

    for n in range(N_KV):
        _, l_s, acc_s = carry[n]
        o_slc = acc_s / l_s
        for g in range(GROUP):
            h = GROUP * n + g
            cs = slice(g * tq, (g + 1) * tq)
            outs[h] = (gates_t[3 * h:3 * h + 1, :] * o_cmps[n][:, cs] + gates_t[3 * h + 1:3 * h + 2, :] * o_slc[:, cs]
                       + gates_t[3 * h + 2:3 * h + 3, :] * o_wins[n][:, cs])
    o_ref[...] = jnp.concatenate(outs, axis=0).T


def _nsa_prompt(q3, kv3, kc, vc, win3, sm3):
    b, t, _ = q3.shape
    assert t % SLC_TILE == 0 and t % Q_BLOCK == 0
    nc = t // CMP_BLOCK
    est = 2 * (2 * t * 256 * 4) + 4 * nc * KV_W * 4 + 8 * Q_BLOCK * 512 * 4 + 16 * 512 * 640 * 4
    kaux = _key_aux(t, lambda r: r, True)
    caux = _key_aux(nc, lambda r: r * CMP_BLOCK + (CMP_BLOCK - 1) / 2, False)
    return pl.pallas_call(
        functools.partial(_nsa_prompt_body, seq=t),
        grid=(b, t // Q_BLOCK),
        in_specs=[
            pl.BlockSpec((None, Q_BLOCK, 512), lambda i, j: (i, j, 0)),
            pl.BlockSpec((None, t, 2 * KV_W), lambda i, j: (i, 0, 1)),
            pl.BlockSpec((None, nc, KV_W), lambda i, j: (i, 0, 0)),
            pl.BlockSpec((None, nc, KV_W), lambda i, j: (i, 0, 0)),
            pl.BlockSpec((None, t, 2 * KV_W), lambda i, j: (i, 0, 0)),
            pl.BlockSpec((None, Q_BLOCK, SM_W), lambda i, j: (i, j, 0)),
            pl.BlockSpec((t, LANES), lambda i, j: (0, 0)),
            pl.BlockSpec((nc, LANES), lambda i, j: (0, 0)),
        ],
        out_specs=pl.BlockSpec((None, Q_BLOCK, 512), lambda i, j: (i, j, 0)),
        out_shape=jax.ShapeDtypeStruct((b, t, 512), F32),
        scratch_shapes=[pltpu.VMEM((nc, Q_BLOCK), F32),
                        pltpu.VMEM((t, 2 * KV_W), BF16), pltpu.VMEM((t, 2 * KV_W), BF16), pltpu.VMEM((nc, 2 * KV_W), BF16),
                        pltpu.VMEM((KV_W, t), BF16), pltpu.VMEM((KV_W, t), BF16), pltpu.VMEM((KV_W, nc), BF16)],
        compiler_params=_params(("parallel", "arbitrary"), est),
        name="nsa_prompt",
    )(q3, kv3, kc, vc, win3, sm3, kaux, caux)


_BNN = (((2,), (1,)), ((0,), (0,)))
_BNT = (((2,), (2,)), ((0,), (0,)))
_BTN = (((1,), (1,)), ((0,), (0,)))


def _bdot(a, b, dims=_BNN):
    return lax.dot_general(a.astype(BF16), b.astype(BF16), dims, preferred_element_type=F32)


def _solve_unit_lower(nmat, rhs):
    c = nmat.shape[1]
    ii = lax.broadcasted_iota(jnp.int32, (c, c), 0)
    jj = lax.broadcasted_iota(jnp.int32, (c, c), 1)
    tmat = jnp.where(ii == jj, 1.0, 0.0) - nmat
    pw = _bdot(nmat, nmat)
    steps = c.bit_length() - 2
    for _ in range(steps - 1):
        both = _bdot(jnp.concatenate([tmat, pw], axis=1), pw)
        tmat = tmat + both[:, 0:c]
        pw = both[:, c:2 * c]
    tmat = tmat + _bdot(tmat, pw)
    return _bdot(tmat, rhs)


def _l2_scale(x, ind, ind_t):
    ssq = _dot_sel(x * x, ind)
    return _dot_sel(lax.rsqrt(ssq + RMS_EPS), ind_t)


def _gdn_prompt_body(gqkv_ref, gz_ref, sm_ref, cw_ref, alog_r_ref, dtb_r_ref, alog_c_ref, dtb_c_ref, nw_ref,
                     o_ref, s_ref, xbuf):
    t = pl.program_id(1)
    c = GDN_CHUNK
    nt = gqkv_ref.shape[0]
    nchunk = nt // c
    pad = SUBLANES

    @pl.when(t == 0)
    def _():
        xbuf[0:pad, :] = jnp.zeros((pad, GDN_QKV), F32)
        s_ref[...] = jnp.zeros(s_ref.shape, F32)

    xbuf[pad:pad + nt, :] = gqkv_ref[...]
    y = cw_ref[GDN_CONV - 1:GDN_CONV, :] * xbuf[pad:pad + nt, :]
    for tap in range(GDN_CONV - 1):
        y = y + cw_ref[tap:tap + 1, :] * xbuf[pad - (GDN_CONV - 1) + tap:pad - (GDN_CONV - 1) + tap + nt, :]
    xbuf[0:pad, :] = xbuf[nt:nt + pad, :]
    act = _silu(y)
    ind = _head_indicator(GDN_W, N_GDN)
    ind_t = _head_indicator_t(N_GDN, GDN_W)
    qraw = act[:, 0:GDN_W]
    kraw = act[:, GDN_W:2 * GDN_W]
    v = act[:, 2 * GDN_W:3 * GDN_W]
    qn = qraw * _l2_scale(qraw, ind, ind_t) * (HEAD_DIM ** -0.5)
    kn = kraw * _l2_scale(kraw, ind, ind_t)

    sm = sm_ref[...]
    sm_t = sm.T
    g_col = -jnp.exp(alog_r_ref[...]) * _softplus(sm[:, SM_GA:SM_GA + N_GDN] + dtb_r_ref[...])
    g_row = -jnp.exp(alog_c_ref[...]) * _softplus(sm_t[SM_GA:SM_GA + N_GDN, :] + dtb_c_ref[...])
    beta = _sigmoid(sm[:, SM_GB:SM_GB + N_GDN])
    ti = lax.broadcasted_iota(jnp.int32, (nt, nt), 0)
    tj = lax.broadcasted_iota(jnp.int32, (nt, nt), 1)
    same = (ti // c) == (tj // c)
    gc_col = _dot_f32(jnp.where(same & (ti >= tj), 1.0, 0.0), g_col)
    gc_row = _dot_f32(g_row, jnp.where(same & (ti <= tj), 1.0, 0.0))

    items = [(g, h) for g in range(nchunk) for h in range(N_GDN)]
    rows = lambda g: slice(g * c, (g + 1) * c)
    heads = lambda x: jnp.stack([x[rows(g), h * HEAD_DIM:(h + 1) * HEAD_DIM] for g, h in items], axis=0)
    cols = lambda x: jnp.stack([x[rows(g), h:h + 1] for g, h in items], axis=0)
    qb, kb, vb = heads(qn), heads(kn), heads(v)
    gcc = cols(gc_col)
    bcol = cols(beta)
    gcr = jnp.stack([gc_row[h:h + 1, rows(g)] for g, h in items], axis=0)
    gl = jnp.stack([gc_row[h:h + 1, (g + 1) * c - 1:(g + 1) * c] for g, h in items], axis=0)
    ii = lax.broadcasted_iota(jnp.int32, (c, c), 0)
    jj = lax.broadcasted_iota(jnp.int32, (c, c), 1)
    dec = jnp.exp(jnp.where(ii >= jj, gcc - gcr, NEG))
    egc = jnp.exp(gcc)
    kbeta = kb * bcol
    nmat = jnp.where(ii > jj, _bdot(kbeta, kb, _BNT) * dec, 0.0)
    x = _solve_unit_lower(nmat, jnp.concatenate([vb * bcol, kbeta * egc], axis=2))
    val, kcd = x[:, :, 0:HEAD_DIM], x[:, :, HEAD_DIM:2 * HEAD_DIM]
    inner = _bdot(qb, kb, _BNT) * dec
    qg = qb * egc
    kend = kb * jnp.exp(gl - gcc)
    egl = jnp.exp(gl)

    s = s_ref[...]
    outs = []
    for g in range(nchunk):
        it = slice(g * N_GDN, (g + 1) * N_GDN)
        vn = val[it] - _bdot(kcd[it], s)
        og = _bdot(qg[it], s) + _bdot(inner[it], vn)
        s = s * egl[it] + _bdot(kend[it], vn, _BTN)
        outs.append(jnp.concatenate([og[h] for h in range(N_GDN)], axis=1))
    s_ref[...] = s
    o = jnp.concatenate(outs, axis=0)
    rms = _dot_sel(lax.rsqrt(_dot_sel(o * o, ind) * (1.0 / HEAD_DIM) + RMS_EPS), ind_t)
    o_ref[...] = o * rms * nw_ref[...] * _silu(gz_ref[...])


GDN_TOK = 4 * GDN_CHUNK


def _gdn_prompt(gqkv3, gz3, sm3, cw, alog, dtb, nw_t):
    b, t, _ = gqkv3.shape
    assert t % GDN_TOK == 0
    c = GDN_TOK
    est = 4 * c * GDN_QKV * 4 + 64 * c * 512 * 4
    return pl.pallas_call(
        _gdn_prompt_body,
        grid=(b, t // c),
        in_specs=[
            pl.BlockSpec((None, c, GDN_QKV), lambda i, j: (i, j, 0)),
            pl.BlockSpec((None, c, GDN_W), lambda i, j: (i, j, 0)),
            pl.BlockSpec((None, c, SM_W), lambda i, j: (i, j, 0)),
            _resident((GDN_CONV, GDN_QKV)), _resident((1, N_GDN)), _resident((1, N_GDN)),
            _resident((N_GDN, 1)), _resident((N_GDN, 1)), _resident((1, GDN_W)),
        ],
        out_specs=[pl.BlockSpec((None, c, GDN_W), lambda i, j: (i, j, 0)),
                   pl.BlockSpec((None, N_GDN, HEAD_DIM, HEAD_DIM), lambda i, j: (i, 0, 0, 0))],
        out_shape=[jax.ShapeDtypeStruct((b, t, GDN_W), F32),
                   jax.ShapeDtypeStruct((b, N_GDN, HEAD_DIM, HEAD_DIM), F32)],
        scratch_shapes=[pltpu.VMEM((c + SUBLANES, GDN_QKV), F32)],
        compiler_params=_params(("parallel", "arbitrary"), est),
        name="gdn_prompt",
    )(gqkv3, gz3, sm3, cw, alog.reshape(1, N_GDN), dtb.reshape(1, N_GDN),
      alog.reshape(N_GDN, 1), dtb.reshape(N_GDN, 1), nw_t)


def _post_tail(x1, ffn, p, lng_ref, lnb_ref, wg_ref, wp_ref):
    x2 = _layer_norm(DN_ALPHA * x1 + ffn, lng_ref[1:2, :], lnb_ref[1:2, :])
    ple = _sigmoid(jnp.dot(x2.astype(BF16), wg_ref[...], preferred_element_type=F32)) \
        * jnp.dot(p.astype(BF16), wp_ref[...], preferred_element_type=F32)
    return _layer_norm(DN_ALPHA * x2 + ple, lng_ref[2:3, :], lnb_ref[2:3, :])


def _post_head(onsa_ref, og_ref, x_ref, wo_ref, lng_ref, lnb_ref):
    attn = jnp.dot(onsa_ref[...].astype(BF16), wo_ref[0:512, :], preferred_element_type=F32) \
        + jnp.dot(og_ref[...].astype(BF16), wo_ref[512:1024, :], preferred_element_type=F32)
    return _layer_norm(DN_ALPHA * x_ref[...] + attn, lng_ref[0:1, :], lnb_ref[0:1, :])


def _post_prompt_body(onsa_ref, og_ref, x_ref, p_ref, wo_ref, lng_ref, lnb_ref, wu_ref, cw_ref, wd_ref,
                      wp_ref, wg_ref, y_ref, fbuf_ref, upbuf):
    t = pl.program_id(1)
    tm = x_ref.shape[0]
    pad = SUBLANES

    @pl.when(t == 0)
    def _():
        upbuf[0:pad, :] = jnp.zeros((pad, D_FF), F32)

    x1 = _post_head(onsa_ref, og_ref, x_ref, wo_ref, lng_ref, lnb_ref)
    x1b = x1.astype(BF16)
    ffn = jnp.zeros((tm, D_MODEL), F32)
    for cc in range(D_FF // FF_CHUNK):
        cs = slice(cc * FF_CHUNK, (cc + 1) * FF_CHUNK)
        up_g = jnp.dot(x1b, wu_ref[:, cs], preferred_element_type=F32)
        up_v = jnp.dot(x1b, wu_ref[:, D_FF + cc * FF_CHUNK:D_FF + (cc + 1) * FF_CHUNK], preferred_element_type=F32)
        upbuf[pad:pad + tm, cs] = up_g
        hg = cw_ref[FFN_CONV - 1:FFN_CONV, cs] * up_g
        for tap in range(FFN_CONV - 1):
            o0 = pad - (FFN_CONV - 1) + tap
            hg = hg + cw_ref[tap:tap + 1, cs] * upbuf[o0:o0 + tm, cs]
        ffn = ffn + jnp.dot((_gelu(hg) * up_v).astype(BF16), wd_ref[cs, :], preferred_element_type=F32)
    fbuf_ref[...] = upbuf[pad + tm - (FFN_CONV - 1):pad + tm, :]
    upbuf[0:pad, :] = upbuf[tm:tm + pad, :]
    y_ref[...] = _post_tail(x1, ffn, p_ref[...], lng_ref, lnb_ref, wg_ref, wp_ref)


def _post_decode_body(onsa_ref, og_ref, x_ref, p_ref, fst_ref, wo_ref, lng_ref, lnb_ref, wu_ref, cw_ref, wd_ref,
                      wp_ref, wg_ref, y_ref, fnew_ref):
    tm = x_ref.shape[0]
    x1 = _post_head(onsa_ref, og_ref, x_ref, wo_ref, lng_ref, lnb_ref)
    x1b = x1.astype(BF16)
    ffn = jnp.zeros((tm, D_MODEL), F32)
    for cc in range(D_FF // FF_CHUNK):
        cs = slice(cc * FF_CHUNK, (cc + 1) * FF_CHUNK)
        cs1 = slice(D_FF + cc * FF_CHUNK, D_FF + (cc + 1) * FF_CHUNK)
        up_g = jnp.dot(x1b, wu_ref[:, cs], preferred_element_type=F32)
        up_v = jnp.dot(x1b, wu_ref[:, cs1], preferred_element_type=F32)
        hg = cw_ref[0:1, cs] * fst_ref[:, cs] + cw_ref[1:2, cs] * fst_ref[:, cs1] + cw_ref[2:3, cs] * up_g
        fnew_ref[:, cs] = fst_ref[:, cs1]
        fnew_ref[:, cs1] = up_g
        ffn = ffn + jnp.dot((_gelu(hg) * up_v).astype(BF16), wd_ref[cs, :], preferred_element_type=F32)
    y_ref[...] = _post_tail(x1, ffn, p_ref[...], lng_ref, lnb_ref, wg_ref, wp_ref)


def _post_weight_specs():
    return [_resident((D_MODEL, D_MODEL)), _resident((3, D_MODEL)), _resident((3, D_MODEL)),
            _resident((D_MODEL, 2 * D_FF)), _resident((FFN_CONV, D_FF)), _resident((D_FF, D_MODEL)),
            _resident((PLE_DIM, D_MODEL)), _resident((D_MODEL, D_MODEL))]


_POST_WEIGHT_BYTES = 2 * (2 * D_MODEL * D_MODEL + 3 * D_MODEL * D_FF + PLE_DIM * D_MODEL) + 4 * 9 * D_FF


def _post_prompt(onsa3, og3, x3, p3, p_base, wts, tm):
    b, t, _ = x3.shape
    row = lambda n: pl.BlockSpec((None, tm, n), lambda i, j: (i, j, 0))
    p_spec = pl.BlockSpec((None, tm, PLE_DIM), lambda i, j: (i + p_base, j, 0))
    est = _POST_WEIGHT_BYTES + 2 * tm * (512 + 512 + 1024 + 256 + 1024) * 4 + (tm + 8) * D_FF * 4 + 12 * tm * 1024 * 4
    return pl.pallas_call(
        _post_prompt_body,
        grid=(b, t // tm),
        in_specs=[row(512), row(512), row(D_MODEL), p_spec] + _post_weight_specs(),
        out_specs=[row(D_MODEL), pl.BlockSpec((None, FFN_CONV - 1, D_FF), lambda i, j: (i, 0, 0))],
        out_shape=[jax.ShapeDtypeStruct((b, t, D_MODEL), F32),
                   jax.ShapeDtypeStruct((b, FFN_CONV - 1, D_FF), F32)],
        scratch_shapes=[pltpu.VMEM((tm + SUBLANES, D_FF), F32)],
        compiler_params=_params(("parallel", "arbitrary"), est),
        name="post_prompt",
    )(onsa3, og3, x3, p3, *wts)


def _post_decode(onsa2, og2, x2, p2, fst2, wts):
    m = x2.shape[0]
    full = lambda n: pl.BlockSpec((m, n), lambda i: (0, 0))
    est = _POST_WEIGHT_BYTES + 2 * m * (512 + 512 + 1024 + 256 + 1024 + 4 * D_FF) * 4 + 12 * m * 1024 * 4
    return pl.pallas_call(
        _post_decode_body,
        grid=(1,),
        in_specs=[full(512), full(512), full(D_MODEL), full(PLE_DIM), full(2 * D_FF)] + _post_weight_specs(),
        out_specs=[full(D_MODEL), full(2 * D_FF)],
        out_shape=[jax.ShapeDtypeStruct((m, D_MODEL), F32), jax.ShapeDtypeStruct((m, 2 * D_FF), F32)],
        compiler_params=_params(("arbitrary",), est),
        name="post_decode",
    )(onsa2, og2, x2, p2, fst2, *wts)


def _nsa_decode_body(pt_ref, *refs, n_pages, page, chained, seqs):
    del pt_ref
    n_in = seqs * n_pages
    page_refs = refs[:n_in]
    (q_ref, kvn_ref, winn_ref, sm_ref, wbuf_ref, pe_ref, phi_ref) = refs[n_in:n_in + 7]
    o_ref, wout_ref = refs[n_in + 7 + (1 if chained else 0):]
    chains = [_nsa_decode_seq(page_refs[s * n_pages:(s + 1) * n_pages], q_ref.at[s], kvn_ref.at[s], winn_ref.at[s],
                              sm_ref.at[s], wbuf_ref.at[s], pe_ref, phi_ref, o_ref.at[s], wout_ref.at[s], page)
              for s in range(seqs)]
    for _ in zip(*chains):
        pass


def _nsa_decode_seq(page_refs, q_ref, kvn_ref, winn_ref, sm_ref, wbuf_ref, pe_ref, phi_ref, o_ref, wout_ref, page):
    n_pages = len(page_refs)
    past = n_pages * page
    ncv = past // CMP_BLOCK
    nsb = past // SLC_BLOCK
    assert ncv <= LANES and ncv % 2 == 0
    q = q_ref[...] * (HEAD_DIM ** -0.5)
    zero = jnp.zeros((1, HEAD_DIM), F32)
    rows = []
    for h in range(N_HEADS):
        piece = q[:, h * HEAD_DIM:(h + 1) * HEAD_DIM]
        rows.append(jnp.concatenate([piece, zero] if h < GROUP else [zero, piece], axis=1))
    qbd = jnp.concatenate(rows, axis=0)
    qbd_b = qbd.astype(BF16)
    hrow = lax.broadcasted_iota(jnp.int32, (N_HEADS, 1), 0)
    slope = jnp.concatenate([jnp.full((1, 1), 2.0 ** -(h + 1), F32) for h in range(N_HEADS)], axis=0)

    zero_hh = jnp.zeros((HEAD_DIM, HEAD_DIM), F32)
    both_heads = lambda w: jnp.concatenate([jnp.concatenate([w, zero_hh], axis=1),
                                            jnp.concatenate([zero_hh, w], axis=1)], axis=0)
    pe_row = lambda which: jnp.concatenate([jnp.mean(pe_ref[which], axis=0, keepdims=True)] * N_KV, axis=1)
    pool = jnp.where(lax.broadcasted_iota(jnp.int32, (past, ncv), 0) // CMP_BLOCK
                     == lax.broadcasted_iota(jnp.int32, (past, ncv), 1), 1.0, 0.0)
    pool_t = jnp.where(lax.broadcasted_iota(jnp.int32, (ncv, past), 1) // CMP_BLOCK
                       == lax.broadcasted_iota(jnp.int32, (ncv, past), 0), 1.0, 0.0).astype(BF16)
    q_phi = lax.dot_general(qbd, both_heads(phi_ref[0]), _NT, precision=lax.Precision.HIGHEST,
                            preferred_element_type=F32)
    yield
    row_s = jnp.concatenate([_dot(q_phi, pr[0:KV_W, :]) for pr in page_refs], axis=1)
    yield
    cidx = lax.broadcasted_iota(jnp.int32, (1, ncv), 1)
    c_mid = cidx.astype(F32) * CMP_BLOCK + (CMP_BLOCK - 1) / 2
    s = _dot_sel(row_s, pool) * (1.0 / CMP_BLOCK) + jnp.sum(q_phi * pe_row(0), axis=-1, keepdims=True) \
        - slope * (float(past) - c_mid)
    p = _softmax_rows(s)
    yield
    p_rows = jnp.dot(p.astype(BF16), pool_t, preferred_element_type=F32) * (1.0 / CMP_BLOCK)
    u = jnp.sum(p, axis=-1, keepdims=True) * pe_row(1)
    for i, pr in enumerate(page_refs):
        u = u + _dot_nt(p_rows[:, i * page:(i + 1) * page], pr[KV_W:2 * KV_W, :])
    yield
    o_cmp = jnp.dot(u, both_heads(phi_ref[1]), precision=lax.Precision.HIGHEST, preferred_element_type=F32)

    p_l = p if ncv == LANES else jnp.concatenate([p, jnp.zeros((N_HEADS, LANES - ncv), F32)], axis=1)
    pair = p_l + pltpu.roll(p_l, LANES - 1, 1)
    pooled = jnp.concatenate([jnp.sum(pair[n * GROUP:(n + 1) * GROUP], axis=0, keepdims=True) for n in range(N_KV)],
                             axis=0)
    lane = lax.broadcasted_iota(jnp.int32, (N_KV, LANES), 1)
    blk = lane // 2
    is_blk = ((lane % 2) == 0) & (lane < ncv)
    is_cur = lane == ncv
    forced = is_cur | (is_blk & ((blk == 0) | ((nsb - blk) < N_LOCAL)))
    score = jnp.where(forced, BIG, jnp.where(is_blk, pooled, -BIG))
    rank = jnp.zeros((N_KV, LANES), F32)
    cand = [2 * b for b in range(nsb)] + [ncv]
    for lc in cand:
        col = score[:, lc:lc + 1]
        rank = rank + jnp.where(lane > lc, jnp.where(col >= score, 1.0, 0.0), jnp.where(col > score, 1.0, 0.0))
    sel = jnp.where((rank < float(min(TOP_N, nsb + 1))) & (is_blk | is_cur), 1.0, 0.0)
    sel8 = jnp.concatenate([jnp.broadcast_to(sel[n:n + 1], (GROUP, LANES)) for n in range(N_KV)], axis=0).astype(BF16)

    kvn = kvn_ref[...]
    s_new = jnp.sum(qbd * kvn[:, 2 * KV_W:3 * KV_W], axis=-1, keepdims=True)
    s_parts = []
    for i, pr in enumerate(page_refs):
        tpos = i * page + lax.broadcasted_iota(jnp.int32, (1, page), 1)
        member = jnp.where((2 * ((i * page + lax.broadcasted_iota(jnp.int32, (LANES, page), 1)) // SLC_BLOCK))
                           == lax.broadcasted_iota(jnp.int32, (LANES, page), 0), 1.0, 0.0).astype(BF16)
        chosen = jnp.dot(sel8, member, preferred_element_type=F32)
        st = _dot(qbd_b, pr[2 * KV_W:3 * KV_W, :]) - slope * (float(past) - tpos.astype(F32))
        s_parts.append(jnp.where(chosen > 0.5, st, NEG))
    yield
    s_all = jnp.concatenate(s_parts, axis=1)
    m = jnp.maximum(jnp.max(s_all, axis=-1, keepdims=True), s_new)
    e = jnp.exp(s_all - m)
    e_new = jnp.exp(s_new - m)
    acc = e_new * kvn[:, 3 * KV_W:4 * KV_W]
    for i, pr in enumerate(page_refs):
        acc = acc + _dot_nt(e[:, i * page:(i + 1) * page], pr[3 * KV_W:4 * KV_W, :])
    yield
    o_slc = acc / (jnp.sum(e, axis=-1, keepdims=True) + e_new)

    wb = wbuf_ref.shape[1]
    winn = winn_ref[...]
    rpos = lax.broadcasted_iota(jnp.int32, (1, wb), 1)
    dist = wb - rpos
    sw = _dot(qbd_b, wbuf_ref[0:KV_W, :]) - slope * dist.astype(F32)
    sw = jnp.where(dist < WINDOW, sw, NEG)
    sw_new = jnp.sum(qbd * winn[:, 0:KV_W], axis=-1, keepdims=True)
    mw = jnp.maximum(jnp.max(sw, axis=-1, keepdims=True), sw_new)
    ew = jnp.exp(sw - mw)
    ew_new = jnp.exp(sw_new - mw)
    o_win = (_dot_nt(ew, wbuf_ref[KV_W:2 * KV_W, :]) + ew_new * winn[:, KV_W:2 * KV_W]) \
        / (jnp.sum(ew, axis=-1, keepdims=True) + ew_new)
    winn_col = jnp.broadcast_to(winn, (SUBLANES, 2 * KV_W)).T[:, 0:1]
    shifted = pltpu.roll(wbuf_ref[...], wb - 1, 1)
    wout_ref[...] = jnp.where(lax.broadcasted_iota(jnp.int32, (2 * KV_W, wb), 1) == wb - 1, winn_col, shifted)

    gates = _sigmoid(sm_ref[:, 0:3 * N_HEADS])
    glane = lax.broadcasted_iota(jnp.int32, (N_HEADS, 3 * N_HEADS), 1)
    gsel = lambda r: jnp.sum(jnp.where(glane == 3 * hrow + r, gates, 0.0), axis=-1, keepdims=True)
    o8 = gsel(0) * o_cmp + gsel(1) * o_slc + gsel(2) * o_win
    pieces = []
    for h in range(N_HEADS):
        n = h // GROUP
        pieces.append(o8[h:h + 1, n * HEAD_DIM:(n + 1) * HEAD_DIM])
    o_ref[...] = jnp.concatenate(pieces, axis=1)
    yield


def _nsa_decode(page_table, cache3, page_base, q3, kvn3, winn3, sm3, wbuf3, seq_base, pe, phi, wout_prev):
    nb, n_pages = page_table.shape
    page = cache3.shape[2]
    wb = wbuf3.shape[2]
    assert wb == WINDOW and n_pages * page >= wb
    chained = wout_prev is not None
    seqs = DEC_SEQS if nb % DEC_SEQS == 0 else 1
    assert seq_base % seqs == 0
    page_specs = [pl.BlockSpec((None, 4 * KV_W, page),
                               functools.partial(lambda i, pt, s, k: (pt[i * seqs + s, k] + page_base, 0, 0), s=s, k=k))
                  for s in range(seqs) for k in range(n_pages)]
    one = lambda n: pl.BlockSpec((seqs, 1, n), lambda i, pt: (i, 0, 0))
    win_spec = pl.BlockSpec((seqs, 2 * KV_W, wb), lambda i, pt: (i + seq_base // seqs, 0, 0))
    in_specs = page_specs + [one(512), one(512), one(256), one(SM_W), win_spec,
                             pl.BlockSpec((2, CMP_BLOCK, HEAD_DIM), lambda i, pt: (0, 0, 0)),
                             pl.BlockSpec((2, HEAD_DIM, HEAD_DIM), lambda i, pt: (0, 0, 0))]
    args = [page_table] + [cache3] * (seqs * n_pages) + [q3, kvn3, winn3, sm3, wbuf3, pe, phi]
    aliases = {}
    if chained:
        in_specs.append(pl.BlockSpec(memory_space=pl.ANY))
        args.append(wout_prev)
        aliases = {len(args) - 1: 1}
    grid_spec = pltpu.PrefetchScalarGridSpec(
        num_scalar_prefetch=1, grid=(nb // seqs,), in_specs=in_specs, out_specs=[one(512), win_spec])
    est = seqs * (2 * n_pages * page * 512 * 4 + 4 * wb * 256 * 4) + 8 * 1024 * 1024
    return pl.pallas_call(
        functools.partial(_nsa_decode_body, n_pages=n_pages, page=page, chained=chained, seqs=seqs),
        grid_spec=grid_spec,
        out_shape=[jax.ShapeDtypeStruct((nb, 1, 512), F32), jax.ShapeDtypeStruct(wbuf3.shape, F32)],
        input_output_aliases=aliases,
        compiler_params=_params(("arbitrary",), est),
        name="nsa_decode",
    )(*args)


def _gdn_dec_prep_body(gqkv_ref, cst_ref, sm_ref, cw_ref, alog_ref, dtb_ref, qn_ref, kn_ref, v_ref, db_ref, cnew_ref):
    w = GDN_QKV
    xn = gqkv_ref[...]
    y = cw_ref[GDN_CONV - 1:GDN_CONV, :] * xn
    for tap in range(GDN_CONV - 1):
        y = y + cw_ref[tap:tap + 1, :] * cst_ref[:, tap * w:(tap + 1) * w]
    for tap in range(GDN_CONV - 2):
        cnew_ref[:, tap * w:(tap + 1) * w] = cst_ref[:, (tap + 1) * w:(tap + 2) * w]
    cnew_ref[:, (GDN_CONV - 2) * w:(GDN_CONV - 1) * w] = xn
    act = _silu(y)
    ind = _head_indicator(GDN_W, N_GDN)
    ind_t = _head_indicator_t(N_GDN, GDN_W)
    qraw = act[:, 0:GDN_W]
    kraw = act[:, GDN_W:2 * GDN_W]
    qn_ref[...] = (qraw * _l2_scale(qraw, ind, ind_t) * (HEAD_DIM ** -0.5)).T
    kn_ref[...] = (kraw * _l2_scale(kraw, ind, ind_t)).T
    v_ref[...] = act[:, 2 * GDN_W:3 * GDN_W].T
    sm = sm_ref[...]
    g = -jnp.exp(alog_ref[...]) * _softplus(sm[:, SM_GA:SM_GA + N_GDN] + dtb_ref[...])
    beta = _sigmoid(sm[:, SM_GB:SM_GB + N_GDN])
    m = sm.shape[0]
    db_ref[...] = jnp.concatenate([jnp.exp(g), beta, jnp.zeros((m, SM_W - 2 * N_GDN), F32)], axis=1).T


def _gdn_dec_prep(gqkv2, cst2, sm2, cw, alog, dtb):
    m = gqkv2.shape[0]
    full = lambda n: pl.BlockSpec((m, n), lambda i: (0, 0))
    full_t = lambda n: pl.BlockSpec((n, m), lambda i: (0, 0))
    est = 2 * m * (8 * GDN_QKV + 4 * 512) * 4 + 8 * m * GDN_QKV * 4
    return pl.pallas_call(
        _gdn_dec_prep_body,
        grid=(1,),
        in_specs=[full(GDN_QKV), full((GDN_CONV - 1) * GDN_QKV), full(SM_W),
                  pl.BlockSpec((GDN_CONV, GDN_QKV), lambda i: (0, 0)),
                  pl.BlockSpec((1, N_GDN), lambda i: (0, 0)), pl.BlockSpec((1, N_GDN), lambda i: (0, 0))],
        out_specs=[full_t(GDN_W), full_t(GDN_W), full_t(GDN_W), full_t(SM_W), full((GDN_CONV - 1) * GDN_QKV)],
        out_shape=[jax.ShapeDtypeStruct((GDN_W, m), F32)] * 3 + [jax.ShapeDtypeStruct((SM_W, m), F32),
                   jax.ShapeDtypeStruct((m, (GDN_CONV - 1) * GDN_QKV), F32)],
        compiler_params=_params(("arbitrary",), est),
        name="gdn_decode_prep",
    )(gqkv2, cst2, sm2, cw, alog.reshape(1, N_GDN), dtb.reshape(1, N_GDN))


def _gdn_dec_step_body(qn_ref, kn_ref, v_ref, db_ref, gz_ref, s_ref, nw_ref, *out_refs):
    o_ref, snew_ref = out_refs[-2:]
    h = pl.program_id(0)
    a = db_ref[pl.ds(h, 1), :]
    bta = db_ref[pl.ds(N_GDN + h, 1), :]
    sk = jnp.zeros(v_ref.shape, F32)
    for d in range(HEAD_DIM):
        sk = sk + kn_ref[d:d + 1, :] * s_ref[d]
    vn = bta * (v_ref[...] - a * sk)
    o = jnp.zeros(v_ref.shape, F32)
    for d in range(HEAD_DIM):
        s_d = a * s_ref[d] + kn_ref[d:d + 1, :] * vn
        snew_ref[d] = s_d
        o = o + qn_ref[d:d + 1, :] * s_d
    o = o * lax.rsqrt(jnp.mean(o * o, axis=0, keepdims=True) + RMS_EPS) * nw_ref[...]
    o_ref[...] = o * _silu(gz_ref[...])


def _gdn_dec_step(qn_t, kn_t, v_t, db_t, gz_t, s4, layer_base, nw_col, s_prev):
    nb = qn_t.shape[1]
    head = lambda: pl.BlockSpec((HEAD_DIM, nb), lambda i: (i, 0))
    st = pl.BlockSpec((None, HEAD_DIM, HEAD_DIM, nb), lambda i: (i + layer_base, 0, 0, 0))
    in_specs = [head(), head(), head(), pl.BlockSpec((SM_W, nb), lambda i: (0, 0)), head(), st,
                pl.BlockSpec((HEAD_DIM, 1), lambda i: (0, 0))]
    args = [qn_t, kn_t, v_t, db_t, gz_t, s4, nw_col]
    aliases = {}
    if s_prev is not None:
        in_specs.append(pl.BlockSpec(memory_space=pl.ANY))
        args.append(s_prev)
        aliases = {len(args) - 1: 1}
    est = 4 * HEAD_DIM * HEAD_DIM * nb * 4 + 4 * 1024 * 1024
    return pl.pallas_call(
        _gdn_dec_step_body,
        grid=(N_GDN,),
        in_specs=in_specs,
        out_specs=[head(), st],
        out_shape=[jax.ShapeDtypeStruct((GDN_W, nb), F32), jax.ShapeDtypeStruct(s4.shape, F32)],
        input_output_aliases=aliases,
        compiler_params=_params(("parallel",), est),
        name="gdn_decode_step",
    )(*args)


def _prep_w_in(w):
    small = jnp.concatenate([w[:, C_GATE:C_GQKV], w[:, C_GA:C_GB], w[:, C_GB:C_GZ],
                             jnp.zeros((D_MODEL, SM_W - (C_GQKV - C_GATE) - 2 * N_GDN), w.dtype)], axis=1)
    return jnp.concatenate([w[:, 0:C_KV], w[:, C_KV:C_WIN], w[:, C_WIN:C_GATE], w[:, C_GQKV:C_GA],
                            w[:, C_GZ:IN_WIDTH], small], axis=1).astype(BF16)


def kernel(x_prompt, x_sample, cache_nsa_kv, state_nsa_win, state_gdn, state_gdn_conv, state_ffn_conv, page_table, p_prompt, p_sample, w_in, nsa_pe, nsa_phi, gdn_conv_w, gdn_A_log, gdn_dt_bias, gdn_norm_w, w_out, ln_g, ln_b, ffn_w_up, ffn_conv_w, ffn_w_down, ple_w_proj, ple_w_gate):
    b, t, d = x_prompt.shape
    nb, ts, _ = x_sample.shape
    assert d == D_MODEL and ts == 1 and t % Q_BLOCK == 0 and t >= WINDOW
    depth = w_in.shape[0]
    n_pool, page = cache_nsa_kv.shape[1], cache_nsa_kv.shape[2]
    wb = state_nsa_win.shape[2]
    tm_p = 512
    xp, xs = x_prompt, x_sample.reshape(nb, d)
    cache_all = cache_nsa_kv.transpose(0, 1, 3, 4, 5, 2).reshape(depth * n_pool, 4 * KV_W, page)
    wbuf_all = state_nsa_win.transpose(0, 1, 3, 4, 5, 2).reshape(depth * nb, 2 * KV_W, wb)
    gstate_all = state_gdn.transpose(0, 2, 3, 4, 1).reshape(depth * N_GDN, HEAD_DIM, HEAD_DIM, nb)
    p_all = p_prompt.reshape(depth * b, t, PLE_DIM)
    win_all = None
    gs_all = None
    st_p, st_s = [], []
    for l in range(depth):
        w_cat = _prep_w_in(w_in[l])
        post_w = (w_out[l].astype(BF16), ln_g[l], ln_b[l], ffn_w_up[l].astype(BF16), ffn_conv_w[l],
                  ffn_w_down[l].astype(BF16), ple_w_proj[l].astype(BF16), ple_w_gate[l].astype(BF16))
        nw_t = jnp.tile(gdn_norm_w[l].reshape(1, HEAD_DIM), (1, N_GDN))

        q, kv, win, gqkv, gz, sm = _proj_in(xp.reshape(b * t, d), w_cat, 512)
        kv3, win3 = kv.reshape(b, t, 512), win.reshape(b, t, 256)
        sm3, gqkv3 = sm.reshape(b, t, SM_W), gqkv.reshape(b, t, GDN_QKV)
        kc, vc = _compress(kv3, nsa_pe[l], nsa_phi[l])
        o_nsa = _nsa_prompt(q.reshape(b, t, 512), kv3, kc, vc, win3, sm3)
        o_g, s_p = _gdn_prompt(gqkv3, gz.reshape(b, t, GDN_W), sm3, gdn_conv_w[l], gdn_A_log[l], gdn_dt_bias[l], nw_t)
        xp, fbuf_p = _post_prompt(o_nsa, o_g, xp, p_all, l * b, post_w, tm_p)
        st_p.append((kv3.reshape(b, t, 4, N_KV, HEAD_DIM), win3[:, t - wb:].reshape(b, wb, 2, N_KV, HEAD_DIM),
                     s_p, gqkv3[:, t - (GDN_CONV - 1):], fbuf_p))

        q, kv, win, gqkv, gz, sm = _proj_in(xs, w_cat, nb)
        o_nsa, win_all = _nsa_decode(
            page_table, cache_all, l * n_pool, q.reshape(nb, 1, 512), kv.reshape(nb, 1, 512),
            win.reshape(nb, 1, 256), sm.reshape(nb, 1, SM_W), wbuf_all, l * nb, nsa_pe[l], nsa_phi[l], win_all)
        qn_t, kn_t, v_t, db_t, cnew = _gdn_dec_prep(gqkv, state_gdn_conv[l].reshape(nb, (GDN_CONV - 1) * GDN_QKV), sm,
                                                    gdn_conv_w[l], gdn_A_log[l], gdn_dt_bias[l])
        o_g_t, gs_all = _gdn_dec_step(qn_t, kn_t, v_t, db_t, gz.T, gstate_all, l * N_GDN,
                                      gdn_norm_w[l].reshape(HEAD_DIM, 1), gs_all)
        xs, fnew = _post_decode(o_nsa.reshape(nb, 512), o_g_t.T, xs, p_sample[l].reshape(nb, PLE_DIM),
                                state_ffn_conv[l].reshape(nb, (FFN_CONV - 1) * D_FF), post_w)
        st_s.append((kv.reshape(nb, 1, 4, N_KV, HEAD_DIM), None, None, cnew.reshape(nb, GDN_CONV - 1, GDN_QKV),
                     fnew.reshape(nb, FFN_CONV - 1, D_FF)))

    sample_states = {
        1: win_all.reshape(depth, nb, 2, N_KV, HEAD_DIM, wb).transpose(0, 1, 5, 2, 3, 4),
        2: gs_all.reshape(depth, N_GDN, HEAD_DIM, HEAD_DIM, nb).transpose(0, 4, 1, 2, 3),
    }
    outs = [xp, xs.reshape(nb, 1, d)]
    for k in range(5):
        outs.append(jnp.stack([s[k] for s in st_p]))
        outs.append(sample_states[k] if k in sample_states else jnp.stack([s[k] for s in st_s]))
    return tuple(outs)
```

```python
import functools

import jax
import jax.numpy as jnp
from jax import lax
from jax.experimental import pallas as pl
from jax.experimental.pallas import tpu as pltpu

F32 = jnp.float32
BF16 = jnp.bfloat16

D_MODEL = 1024
DEPTH = 2
HEAD_DIM = 64
N_HEADS = 8
N_KV = 2
GROUP = 4
N_GDN = 8
KV_W = N_KV * HEAD_DIM
GDN_W = N_GDN * HEAD_DIM
GDN_QKV = 3 * GDN_W
CMP_BLOCK = 32
SLC_BLOCK = 64
TOP_N = 16
N_LOCAL = 2
WINDOW = 512
Q_BLOCK = 128
GDN_CHUNK = 64
GDN_CONV = 4
FFN_CONV = 3
D_FF = 2816
PLE_DIM = 256
LN_EPS = 1e-5
RMS_EPS = 1e-6
DN_ALPHA = (2 * DEPTH) ** 0.25
NEG = -1e30
BIG = 1e30

C_KV = 512
C_WIN = 1024
C_GATE = 1280
C_GQKV = 1304
C_GA = 2840
C_GB = 2848
C_GZ = 2856
IN_WIDTH = 3368
SM_W = 128
SM_GA = 24
SM_GB = 32

LANES = 128
SUBLANES = 8
VMEM_CAP = 56 * 1024 * 1024
FF_CHUNK = 256
SLC_TILE = 512
DEC_SEQS = 4

_NT = (((1,), (1,)), ((), ()))
_TN = (((0,), (0,)), ((), ()))


def _dot(a, b):
    return jnp.dot(a.astype(BF16), b.astype(BF16), preferred_element_type=F32)


def _dot_nt(a, b):
    return lax.dot_general(a.astype(BF16), b.astype(BF16), _NT, preferred_element_type=F32)


def _dot_tn(a, b):
    return lax.dot_general(a.astype(BF16), b.astype(BF16), _TN, preferred_element_type=F32)


def _dot_f32(a, b):
    return jnp.dot(a, b, precision=lax.Precision.HIGHEST, preferred_element_type=F32)


def _split3(a):
    hi = a.astype(BF16)
    r = a - hi.astype(F32)
    mid = r.astype(BF16)
    lo = (r - mid.astype(F32)).astype(BF16)
    return hi, mid, lo


def _dot_sel(a, sel):
    hi, mid, lo = _split3(a)
    sel = sel.astype(BF16)
    d = lambda u: jnp.dot(u, sel, preferred_element_type=F32)
    return d(hi) + d(mid) + d(lo)


def _dot_tn_sel(sel, a):
    hi, mid, lo = _split3(a)
    sel = sel.astype(BF16)
    d = lambda u: lax.dot_general(sel, u, _TN, preferred_element_type=F32)
    return d(hi) + d(mid) + d(lo)


def _sigmoid(x):
    return 1.0 / (1.0 + jnp.exp(-x))


def _silu(x):
    return x * _sigmoid(x)


def _softplus(x):
    return jnp.maximum(x, 0.0) + jnp.log1p(jnp.exp(-jnp.abs(x)))


def _layer_norm(x, g, b):
    mu = jnp.mean(x, axis=-1, keepdims=True)
    xc = x - mu
    var = jnp.mean(xc * xc, axis=-1, keepdims=True)
    return xc * lax.rsqrt(var + LN_EPS) * g + b


def _gelu(x):
    return 0.5 * x * (1.0 + lax.erf(x * (0.5 ** 0.5)))


def _head_indicator(width, heads):
    r = lax.broadcasted_iota(jnp.int32, (width, heads), 0) // HEAD_DIM
    c = lax.broadcasted_iota(jnp.int32, (width, heads), 1)
    return jnp.where(r == c, 1.0, 0.0).astype(BF16)


def _head_indicator_t(heads, width):
    r = lax.broadcasted_iota(jnp.int32, (heads, width), 0)
    c = lax.broadcasted_iota(jnp.int32, (heads, width), 1) // HEAD_DIM
    return jnp.where(r == c, 1.0, 0.0).astype(BF16)


def _params(sem, est_bytes):
    limit = int(min(max(2 * est_bytes, 32 * 1024 * 1024), VMEM_CAP))
    return pltpu.CompilerParams(dimension_semantics=sem, vmem_limit_bytes=limit)


def _resident(shape):
    nd = len(shape)
    return pl.BlockSpec(shape, lambda *_: (0,) * nd, pipeline_mode=pl.Buffered(1))


_PROJ_WIDTHS = (512, 512, 256, GDN_QKV, GDN_W, SM_W)


def _proj_in_body(x_ref, w_ref, *refs, n_extra_in, emit_kv_t):
    out_refs = refs[n_extra_in:]
    xb = x_ref[...].astype(BF16)
    off = 0
    for ref, n in zip(out_refs, _PROJ_WIDTHS):
        ref[...] = jnp.dot(xb, w_ref[:, off:off + n], preferred_element_type=F32)
        off += n
    if emit_kv_t:
        out_refs[len(_PROJ_WIDTHS)][...] = out_refs[1][...].T


def _proj_in(x2, w_cat, tm, kv_t=None):
    m = x2.shape[0]
    tot = sum(_PROJ_WIDTHS)
    est = 2 * tm * D_MODEL * 4 + D_MODEL * tot * 2 + 4 * tm * tot * 4
    in_specs = [pl.BlockSpec((tm, D_MODEL), lambda i: (i, 0)), _resident((D_MODEL, tot))]
    out_specs = [pl.BlockSpec((tm, n), lambda i: (i, 0)) for n in _PROJ_WIDTHS]
    out_shape = [jax.ShapeDtypeStruct((m, n), F32) for n in _PROJ_WIDTHS]
    args = [x2, w_cat]
    aliases = {}
    if kv_t is not None:
        prev, slab_base, rows, n_slabs = kv_t
        per_seq = rows // tm
        out_specs.append(pl.BlockSpec((None, 512, tm), lambda i: (slab_base + i // per_seq, 0, i % per_seq)))
        out_shape.append(jax.ShapeDtypeStruct((n_slabs, 512, rows), F32))
        if prev is not None:
            in_specs.append(pl.BlockSpec(memory_space=pl.ANY))
            args.append(prev)
            aliases = {2: len(_PROJ_WIDTHS)}
    return pl.pallas_call(
        functools.partial(_proj_in_body, n_extra_in=len(args) - 2, emit_kv_t=kv_t is not None),
        grid=(m // tm,),
        in_specs=in_specs,
        out_specs=out_specs,
        out_shape=out_shape,
        input_output_aliases=aliases,
        compiler_params=_params(("parallel",), est),
        name="proj_in",
    )(*args)


def _compress_rows(rows, pe, phi):
    nblk = rows.shape[0] // CMP_BLOCK
    mean = rows.reshape(nblk, CMP_BLOCK, KV_W).sum(axis=1) * (1.0 / CMP_BLOCK)
    pem = jnp.mean(pe, axis=0, keepdims=True)
    outs = []
    for n in range(N_KV):
        outs.append(_dot_f32(mean[:, n * HEAD_DIM:(n + 1) * HEAD_DIM] + pem, phi))
    return jnp.concatenate(outs, axis=1)


def _compress_body(kv_ref, pe_ref, phi_ref, kc_ref, vc_ref):
    kc_ref[...] = _compress_rows(kv_ref[:, 0:KV_W], pe_ref[0], phi_ref[0])
    vc_ref[...] = _compress_rows(kv_ref[:, KV_W:2 * KV_W], pe_ref[1], phi_ref[1])


def _compress(kv3, pe, phi):
    b, t, _ = kv3.shape
    nc = t // CMP_BLOCK
    est = 2 * t * 256 * 4 + 4 * nc * KV_W * 4
    return pl.pallas_call(
        _compress_body,
        grid=(b,),
        in_specs=[pl.BlockSpec((None, t, 2 * KV_W), lambda i: (i, 0, 0)),
                  _resident((2, CMP_BLOCK, HEAD_DIM)), _resident((2, HEAD_DIM, HEAD_DIM))],
        out_specs=[pl.BlockSpec((None, nc, KV_W), lambda i: (i, 0, 0))] * 2,
        out_shape=[jax.ShapeDtypeStruct((b, nc, KV_W), F32)] * 2,
        compiler_params=_params(("parallel",), est),
        name="nsa_compress",
    )(kv3, pe, phi)


def _softmax_rows(s):
    m = jnp.max(s, axis=-1, keepdims=True)
    e = jnp.exp(s - m)
    return e / jnp.sum(e, axis=-1, keepdims=True)


def _key_aux(npos, pos_of_row, with_blocks):
    pos = pos_of_row(lax.broadcasted_iota(jnp.int32, (npos, LANES), 0).astype(F32))
    lane = lax.broadcasted_iota(jnp.int32, (npos, LANES), 1)
    hi = jnp.floor(pos * (1.0 / SLC_BLOCK))
    aux = jnp.where(lane == AUX0, hi, jnp.where(lane == AUX0 + 1, pos - hi * SLC_BLOCK,
                    jnp.where((lane == AUX0 + 2) | (lane == AUX0 + 3), 1.0, 0.0)))
    if with_blocks:
        aux = jnp.where(lane < AUX0, jnp.where(hi == lane.astype(F32), NEG, 0.0), aux)
    return aux.astype(BF16)


AUX0 = 64


def _softmax_cols(s):
    m = jnp.max(s, axis=0, keepdims=True)
    e = jnp.exp(s - m)
    return e / jnp.sum(e, axis=0, keepdims=True)


def _nsa_prompt_body(q_ref, slc_ref, kc_ref, vc_ref, win_ref, sm_ref, kaux_ref, caux_ref, o_ref,
                     imp_ref, kcat, wcat, ccat, vts, vtw, vtc, st_buf, *, seq):
    j = pl.program_id(1)
    nc = seq // CMP_BLOCK
    ns = seq // SLC_BLOCK
    tq = Q_BLOCK
    cols4 = GROUP * tq
    assert ns <= AUX0 and seq <= 256 * SLC_BLOCK

    @pl.when(j == 0)
    def _():
        for c0 in range(0, seq, SLC_TILE):
            rs = slice(c0, c0 + SLC_TILE)
            kcat[rs, 0:KV_W] = slc_ref[rs, 0:KV_W].astype(BF16)
            kcat[rs, KV_W:2 * KV_W] = kaux_ref[rs, :]
            wcat[rs, 0:KV_W] = win_ref[rs, 0:KV_W].astype(BF16)
            wcat[rs, KV_W:2 * KV_W] = kaux_ref[rs, :]
            vts[:, rs] = slc_ref[rs, KV_W:2 * KV_W].T.astype(BF16)
            vtw[:, rs] = win_ref[rs, KV_W:2 * KV_W].T.astype(BF16)
        ccat[:, 0:KV_W] = kc_ref[...].astype(BF16)
        ccat[:, KV_W:2 * KV_W] = caux_ref[...]
        vtc[...] = vc_ref[...].T.astype(BF16)

    q_t = (q_ref[...] * (HEAD_DIM ** -0.5)).T
    gates_t = _sigmoid(sm_ref[...].T[0:3 * N_HEADS, :])
    qpos_r = j * tq + lax.broadcasted_iota(jnp.int32, (1, tq), 1)
    qpos4 = jnp.concatenate([qpos_r] * GROUP, axis=1)
    qa = (qpos4 // SLC_BLOCK).astype(F32)
    qb = (qpos4 % SLC_BLOCK).astype(F32)
    arow = lax.broadcasted_iota(jnp.int32, (SUBLANES, cols4), 0)
    t_q = lax.broadcasted_iota(jnp.int32, (tq, cols4), 1) % tq
    t_k = lax.broadcasted_iota(jnp.int32, (tq, cols4), 0)
    zeros_h = jnp.zeros((HEAD_DIM, cols4), F32)
    outs = [None] * N_HEADS
    rhs_slcs, o_cmps, o_wins = [], [], []
    for n in range(N_KV):
        hs = slice(n * HEAD_DIM, (n + 1) * HEAD_DIM)
        qn_t = jnp.concatenate(
            [q_t[(GROUP * n + g) * HEAD_DIM:(GROUP * n + g + 1) * HEAD_DIM, :] for g in range(GROUP)], axis=1)
        slope = jnp.concatenate(
            [jnp.full((1, tq), 2.0 ** -(GROUP * n + g + 1), F32) for g in range(GROUP)], axis=1)
        q_top = jnp.concatenate([qn_t, zeros_h] if n == 0 else [zeros_h, qn_t], axis=0)
        q_pos = jnp.where(arow == 0, slope * SLC_BLOCK,
                          jnp.where(arow == 1, slope,
                                    jnp.where(arow == 2, -slope * SLC_BLOCK * qa,
                                              jnp.where(arow == 3, -slope * qb, 0.0))))
        q_tail = jnp.zeros((2 * KV_W - KV_W - AUX0 - SUBLANES, cols4), F32)

        def rhs(block_rows):
            return jnp.concatenate([q_top, block_rows, q_pos, q_tail], axis=0).astype(BF16)

        rhs_plain = rhs(jnp.zeros((AUX0, cols4), F32))

        c_end = lax.broadcasted_iota(jnp.int32, (nc, 1), 0) * CMP_BLOCK + (CMP_BLOCK - 1)
        c_vis = c_end <= qpos4
        s = jnp.dot(ccat[...], rhs_plain, preferred_element_type=F32)
        p = jnp.where(c_vis, _softmax_cols(jnp.where(c_vis, s, NEG)), 0.0)
        o_cmp = jnp.dot(vtc[hs, :], p.astype(BF16), preferred_element_type=F32)

        imp_ref[...] = p[:, 0:tq] + p[:, tq:2 * tq] + p[:, 2 * tq:3 * tq] + p[:, 3 * tq:4 * tq]
        impb = imp_ref[pl.ds(0, ns, stride=2), :] + imp_ref[pl.ds(1, ns, stride=2), :]
        blk = lax.broadcasted_iota(jnp.int32, (ns, tq), 0)
        cur = qpos_r // SLC_BLOCK
        future = blk > cur
        forced = (blk == 0) | (((cur - blk) < N_LOCAL) & jnp.logical_not(future))
        score = jnp.where(future, -BIG, jnp.where(forced, BIG, impb))
        rank = jnp.zeros((ns, tq), F32)
        for bp in range(ns):
            row = score[bp:bp + 1, :]
            rank = rank + jnp.where(blk > bp, jnp.where(row >= score, 1.0, 0.0), jnp.where(row > score, 1.0, 0.0))
        unsel_t = jnp.where(rank < float(min(TOP_N, ns)), 0.0, 1.0)
        if ns < AUX0:
            unsel_t = jnp.concatenate([unsel_t, jnp.zeros((AUX0 - ns, tq), F32)], axis=0)
        rhs_slcs.append(rhs(jnp.concatenate([unsel_t] * GROUP, axis=1)))

        nwb = WINDOW // tq + 1
        s_parts, starts = [], []
        for i in range(nwb):
            kb = j - (nwb - 1) + i
            r0 = pl.multiple_of(jnp.maximum(kb, 0) * tq, tq)
            sw = jnp.dot(wcat[pl.ds(r0, tq), :], rhs_plain, preferred_element_type=F32)
            if i == 0:
                sw = jnp.where(t_k > t_q, sw, NEG)
            if i == nwb - 1:
                sw = jnp.where(t_k <= t_q, sw, NEG)
            else:
                sw = jnp.where(kb >= 0, sw, NEG)
            s_parts.append(sw)
            starts.append(r0)
        pw = _softmax_cols(jnp.concatenate(s_parts, axis=0)).astype(BF16)
        o_win = jnp.dot(vtw[hs, pl.ds(starts[0], tq)], pw[0:tq], preferred_element_type=F32)
        for i in range(1, nwb):
            o_win = o_win + jnp.dot(vtw[hs, pl.ds(starts[i], tq)], pw[i * tq:(i + 1) * tq], preferred_element_type=F32)

        o_cmps.append(o_cmp)
        o_wins.append(o_win)

    heads = range(N_KV)

    def slc_scores(k0):
        k_rows = kcat[pl.ds(k0, SLC_TILE), :]
        return tuple(jnp.dot(k_rows, rhs_slcs[n], preferred_element_type=F32) for n in heads)

    def slc_absorb(k0, sts, stats, q_limit=None):
        if q_limit is not None:
            kpos = k0 + lax.broadcasted_iota(jnp.int32, (SLC_TILE, 1), 0)
            sts = [jnp.where(kpos <= q_limit, st, NEG) for st in sts]
        m2s = [jnp.maximum(stats[n][0], jnp.max(sts[n], axis=0, keepdims=True)) for n in heads]
        es = [jnp.exp(sts[n] - m2s[n]) for n in heads]
        pvs = [jnp.dot(vts[n * HEAD_DIM:(n + 1) * HEAD_DIM, pl.ds(k0, SLC_TILE)], es[n].astype(BF16),
                       preferred_element_type=F32) for n in heads]
        new = []
        for n in heads:
            m, l, acc = stats[n]
            a = jnp.exp(m - m2s[n])
            new.append((m2s[n], a * l + jnp.sum(es[n], axis=0, keepdims=True), a * acc + pvs[n]))
        return tuple(new)

    def slc_issue(k0, slot):
        for n, st in enumerate(slc_scores(k0)):
            st_buf[slot, n] = st

    def slc_held(slot):
        return [st_buf[slot, n] for n in heads]

    def slc_pair(i, stats):
        k0 = pl.multiple_of(2 * i * SLC_TILE, SLC_TILE)
        slc_issue(k0 + SLC_TILE, 1)
        stats = slc_absorb(k0, slc_held(0), stats)
        slc_issue(k0 + 2 * SLC_TILE, 0)
        return slc_absorb(k0 + SLC_TILE, slc_held(1), stats)

    n_full = (j * tq) // SLC_TILE
    start = (jnp.full((1, cols4), NEG, F32), jnp.zeros((1, cols4), F32), jnp.zeros((HEAD_DIM, cols4), F32))
    slc_issue(0, 0)
    stats = lax.fori_loop(0, n_full // 2, slc_pair, (start,) * N_KV)
    odd = (n_full % 2) == 1
    k_even = pl.multiple_of((n_full // 2) * 2 * SLC_TILE, SLC_TILE)
    k_diag = pl.multiple_of(n_full * SLC_TILE, SLC_TILE)
    slc_issue(k_diag, 1)
    stats = slc_absorb(k_even, slc_held(0), stats, q_limit=jnp.where(odd, jnp.int32(seq), qpos4))
    carry = lax.cond(odd, lambda s: slc_absorb(k_diag, slc_held(1), s, q_limit=qpos4), lambda s: s, stats)

    for n in range(N_KV):
        _, l_s, acc_s = carry[n]
        o_slc = acc_s / l_s
        for g in range(GROUP):
            h = GROUP * n + g
            cs = slice(g * tq, (g + 1) * tq)
            outs[h] = (gates_t[3 * h:3 * h + 1, :] * o_cmps[n][:, cs] + gates_t[3 * h + 1:3 * h + 2, :] * o_slc[:, cs]
                       + gates_t[3 * h + 2:3 * h + 3, :] * o_wins[n][:, cs])
    o_ref[...] = jnp.concatenate(outs, axis=0).T


def _nsa_prompt(q3, kv3, kc, vc, win3, sm3):
    b, t, _ = q3.shape
    assert t % SLC_TILE == 0 and t % Q_BLOCK == 0
    nc = t // CMP_BLOCK
    est = 2 * (2 * t * 256 * 4) + 4 * nc * KV_W * 4 + 8 * Q_BLOCK * 512 * 4 + 16 * 512 * 640 * 4
    kaux = _key_aux(t, lambda r: r, True)
    caux = _key_aux(nc, lambda r: r * CMP_BLOCK + (CMP_BLOCK - 1) / 2, False)
    return pl.pallas_call(
        functools.partial(_nsa_prompt_body, seq=t),
        grid=(b, t // Q_BLOCK),
        in_specs=[
            pl.BlockSpec((None, Q_BLOCK, 512), lambda i, j: (i, j, 0)),
            pl.BlockSpec((None, t, 2 * KV_W), lambda i, j: (i, 0, 1)),
            pl.BlockSpec((None, nc, KV_W), lambda i, j: (i, 0, 0)),
            pl.BlockSpec((None, nc, KV_W), lambda i, j: (i, 0, 0)),
            pl.BlockSpec((None, t, 2 * KV_W), lambda i, j: (i, 0, 0)),
            pl.BlockSpec((None, Q_BLOCK, SM_W), lambda i, j: (i, j, 0)),
            pl.BlockSpec((t, LANES), lambda i, j: (0, 0)),
            pl.BlockSpec((nc, LANES), lambda i, j: (0, 0)),
        ],
        out_specs=pl.BlockSpec((None, Q_BLOCK, 512), lambda i, j: (i, j, 0)),
        out_shape=jax.ShapeDtypeStruct((b, t, 512), F32),
        scratch_shapes=[pltpu.VMEM((nc, Q_BLOCK), F32),
                        pltpu.VMEM((t, 2 * KV_W), BF16), pltpu.VMEM((t, 2 * KV_W), BF16), pltpu.VMEM((nc, 2 * KV_W), BF16),
                        pltpu.VMEM((KV_W, t), BF16), pltpu.VMEM((KV_W, t), BF16), pltpu.VMEM((KV_W, nc), BF16),
                        pltpu.VMEM((2, N_KV, SLC_TILE, GROUP * Q_BLOCK), F32)],
        compiler_params=_params(("parallel", "arbitrary"), est),
        name="nsa_prompt",
    )(q3, kv3, kc, vc, win3, sm3, kaux, caux)


_BNN = (((2,), (1,)), ((0,), (0,)))
_BNT = (((2,), (2,)), ((0,), (0,)))
_BTN = (((1,), (1,)), ((0,), (0,)))


def _bdot(a, b, dims=_BNN):
    return lax.dot_general(a.astype(BF16), b.astype(BF16), dims, preferred_element_type=F32)


def _solve_unit_lower(nmat, rhs):
    c = nmat.shape[1]
    ii = lax.broadcasted_iota(jnp.int32, (c, c), 0)
    jj = lax.broadcasted_iota(jnp.int32, (c, c), 1)
    tmat = jnp.where(ii == jj, 1.0, 0.0) - nmat
    pw = _bdot(nmat, nmat)
    steps = c.bit_length() - 2
    for _ in range(steps - 1):
        both = _bdot(jnp.concatenate([tmat, pw], axis=1), pw)
        tmat = tmat + both[:, 0:c]
        pw = both[:, c:2 * c]
    tmat = tmat + _bdot(tmat, pw)
    return _bdot(tmat, rhs)


def _l2_scale(x, ind, ind_t):
    ssq = _dot_sel(x * x, ind)
    return _dot_sel(lax.rsqrt(ssq + RMS_EPS), ind_t)


def _gdn_prompt_body(gqkv_ref, gz_ref, sm_ref, cw_ref, alog_r_ref, dtb_r_ref, alog_c_ref, dtb_c_ref, nw_ref,
                     o_ref, s_ref, xbuf):
    t = pl.program_id(1)
    c = GDN_CHUNK
    nt = gqkv_ref.shape[0]
    nchunk = nt // c
    pad = SUBLANES

    @pl.when(t == 0)
    def _():
        xbuf[0:pad, :] = jnp.zeros((pad, GDN_QKV), F32)
        s_ref[...] = jnp.zeros(s_ref.shape, F32)

    xbuf[pad:pad + nt, :] = gqkv_ref[...]
    y = cw_ref[GDN_CONV - 1:GDN_CONV, :] * xbuf[pad:pad + nt, :]
    for tap in range(GDN_CONV - 1):
        y = y + cw_ref[tap:tap + 1, :] * xbuf[pad - (GDN_CONV - 1) + tap:pad - (GDN_CONV - 1) + tap + nt, :]
    xbuf[0:pad, :] = xbuf[nt:nt + pad, :]
    act = _silu(y)
    ind = _head_indicator(GDN_W, N_GDN)
    ind_t = _head_indicator_t(N_GDN, GDN_W)
    qraw = act[:, 0:GDN_W]
    kraw = act[:, GDN_W:2 * GDN_W]
    v = act[:, 2 * GDN_W:3 * GDN_W]
    qn = qraw * _l2_scale(qraw, ind, ind_t) * (HEAD_DIM ** -0.5)
    kn = kraw * _l2_scale(kraw, ind, ind_t)

    sm = sm_ref[...]
    sm_t = sm.T
    g_col = -jnp.exp(alog_r_ref[...]) * _softplus(sm[:, SM_GA:SM_GA + N_GDN] + dtb_r_ref[...])
    g_row = -jnp.exp(alog_c_ref[...]) * _softplus(sm_t[SM_GA:SM_GA + N_GDN, :] + dtb_c_ref[...])
    beta = _sigmoid(sm[:, SM_GB:SM_GB + N_GDN])
    ti = lax.broadcasted_iota(jnp.int32, (nt, nt), 0)
    tj = lax.broadcasted_iota(jnp.int32, (nt, nt), 1)
    same = (ti // c) == (tj // c)
    upto = jnp.where(same & (ti <= tj), 1.0, 0.0)
    gc_col = _dot_tn_sel(upto, g_col)
    gc_row = _dot_sel(g_row, upto)

    items = [(g, h) for g in range(nchunk) for h in range(N_GDN)]
    rows = lambda g: slice(g * c, (g + 1) * c)
    heads = lambda x: jnp.stack([x[rows(g), h * HEAD_DIM:(h + 1) * HEAD_DIM] for g, h in items], axis=0)
    cols = lambda x: jnp.stack([x[rows(g), h:h + 1] for g, h in items], axis=0)
    qb, kb, vb = heads(qn), heads(kn), heads(v)
    gcc = cols(gc_col)
    bcol = cols(beta)
    gcr = jnp.stack([gc_row[h:h + 1, rows(g)] for g, h in items], axis=0)
    gl = jnp.stack([gc_row[h:h + 1, (g + 1) * c - 1:(g + 1) * c] for g, h in items], axis=0)
    ii = lax.broadcasted_iota(jnp.int32, (c, c), 0)
    jj = lax.broadcasted_iota(jnp.int32, (c, c), 1)
    dec = jnp.exp(jnp.where(ii >= jj, gcc - gcr, NEG))
    egc = jnp.exp(gcc)
    kbeta = kb * bcol
    nmat = jnp.where(ii > jj, _bdot(kbeta, kb, _BNT) * dec, 0.0)
    x = _solve_unit_lower(nmat, jnp.concatenate([vb * bcol, kbeta * egc], axis=2))
    val, kcd = x[:, :, 0:HEAD_DIM], x[:, :, HEAD_DIM:2 * HEAD_DIM]
    inner = _bdot(qb, kb, _BNT) * dec
    qg = qb * egc
    kend = kb * jnp.exp(gl - gcc)
    egl = jnp.exp(gl)

    s = s_ref[...]
    outs = []
    for g in range(nchunk):
        it = slice(g * N_GDN, (g + 1) * N_GDN)
        vn = val[it] - _bdot(kcd[it], s)
        og = _bdot(qg[it], s) + _bdot(inner[it], vn)
        s = s * egl[it] + _bdot(kend[it], vn, _BTN)
        outs.append(jnp.concatenate([og[h] for h in range(N_GDN)], axis=1))
    s_ref[...] = s
    o = jnp.concatenate(outs, axis=0)
    rms = _dot_sel(lax.rsqrt(_dot_sel(o * o, ind) * (1.0 / HEAD_DIM) + RMS_EPS), ind_t)
    o_ref[...] = o * rms * nw_ref[...] * _silu(gz_ref[...])


GDN_TOK = 4 * GDN_CHUNK


def _gdn_prompt(gqkv3, gz3, sm3, cw, alog, dtb, nw_t):
    b, t, _ = gqkv3.shape
    assert t % GDN_TOK == 0
    c = GDN_TOK
    est = 4 * c * GDN_QKV * 4 + 64 * c * 512 * 4
    return pl.pallas_call(
        _gdn_prompt_body,
        grid=(b, t // c),
        in_specs=[
            pl.BlockSpec((None, c, GDN_QKV), lambda i, j: (i, j, 0)),
            pl.BlockSpec((None, c, GDN_W), lambda i, j: (i, j, 0)),
            pl.BlockSpec((None, c, SM_W), lambda i, j: (i, j, 0)),
            _resident((GDN_CONV, GDN_QKV)), _resident((1, N_GDN)), _resident((1, N_GDN)),
            _resident((N_GDN, 1)), _resident((N_GDN, 1)), _resident((1, GDN_W)),
        ],
        out_specs=[pl.BlockSpec((None, c, GDN_W), lambda i, j: (i, j, 0)),
                   pl.BlockSpec((None, N_GDN, HEAD_DIM, HEAD_DIM), lambda i, j: (i, 0, 0, 0))],
        out_shape=[jax.ShapeDtypeStruct((b, t, GDN_W), F32),
                   jax.ShapeDtypeStruct((b, N_GDN, HEAD_DIM, HEAD_DIM), F32)],
        scratch_shapes=[pltpu.VMEM((c + SUBLANES, GDN_QKV), F32)],
        compiler_params=_params(("parallel", "arbitrary"), est),
        name="gdn_prompt",
    )(gqkv3, gz3, sm3, cw, alog.reshape(1, N_GDN), dtb.reshape(1, N_GDN),
      alog.reshape(N_GDN, 1), dtb.reshape(N_GDN, 1), nw_t)


def _post_tail(x1, ffn, p, lng_ref, lnb_ref, wg_ref, wp_ref):
    x2 = _layer_norm(DN_ALPHA * x1 + ffn, lng_ref[1:2, :], lnb_ref[1:2, :])
    ple = _sigmoid(jnp.dot(x2.astype(BF16), wg_ref[...], preferred_element_type=F32)) \
        * jnp.dot(p.astype(BF16), wp_ref[...], preferred_element_type=F32)
    return _layer_norm(DN_ALPHA * x2 + ple, lng_ref[2:3, :], lnb_ref[2:3, :])


def _post_head(onsa_ref, og_ref, x_ref, wo_ref, lng_ref, lnb_ref):
    attn = jnp.dot(onsa_ref[...].astype(BF16), wo_ref[0:512, :], preferred_element_type=F32) \
        + jnp.dot(og_ref[...].astype(BF16), wo_ref[512:1024, :], preferred_element_type=F32)
    return _layer_norm(DN_ALPHA * x_ref[...] + attn, lng_ref[0:1, :], lnb_ref[0:1, :])


def _post_prompt_body(onsa_ref, og_ref, x_ref, p_ref, wo_ref, lng_ref, lnb_ref, wu_ref, cw_ref, wd_ref,
                      wp_ref, wg_ref, y_ref, fbuf_ref, upbuf):
    t = pl.program_id(1)
    tm = x_ref.shape[0]
    pad = SUBLANES

    @pl.when(t == 0)
    def _():
        upbuf[0:pad, :] = jnp.zeros((pad, D_FF), F32)

    x1 = _post_head(onsa_ref, og_ref, x_ref, wo_ref, lng_ref, lnb_ref)
    x1b = x1.astype(BF16)
    ffn = jnp.zeros((tm, D_MODEL), F32)
    for cc in range(D_FF // FF_CHUNK):
        cs = slice(cc * FF_CHUNK, (cc + 1) * FF_CHUNK)
        up_g = jnp.dot(x1b, wu_ref[:, cs], preferred_element_type=F32)
        up_v = jnp.dot(x1b, wu_ref[:, D_FF + cc * FF_CHUNK:D_FF + (cc + 1) * FF_CHUNK], preferred_element_type=F32)
        upbuf[pad:pad + tm, cs] = up_g
        hg = cw_ref[FFN_CONV - 1:FFN_CONV, cs] * up_g
        for tap in range(FFN_CONV - 1):
            o0 = pad - (FFN_CONV - 1) + tap
            hg = hg + cw_ref[tap:tap + 1, cs] * upbuf[o0:o0 + tm, cs]
        ffn = ffn + jnp.dot((_gelu(hg) * up_v).astype(BF16), wd_ref[cs, :], preferred_element_type=F32)
    fbuf_ref[...] = upbuf[pad + tm - (FFN_CONV - 1):pad + tm, :]
    upbuf[0:pad, :] = upbuf[tm:tm + pad, :]
    y_ref[...] = _post_tail(x1, ffn, p_ref[...], lng_ref, lnb_ref, wg_ref, wp_ref)


def _post_decode_body(onsa_ref, og_ref, x_ref, p_ref, fst_ref, wo_ref, lng_ref, lnb_ref, wu_ref, cw_ref, wd_ref,
                      wp_ref, wg_ref, y_ref, fnew_ref):
    tm = x_ref.shape[0]
    x1 = _post_head(onsa_ref, og_ref, x_ref, wo_ref, lng_ref, lnb_ref)
    x1b = x1.astype(BF16)
    ffn = jnp.zeros((tm, D_MODEL), F32)
    for cc in range(D_FF // FF_CHUNK):
        cs = slice(cc * FF_CHUNK, (cc + 1) * FF_CHUNK)
        cs1 = slice(D_FF + cc * FF_CHUNK, D_FF + (cc + 1) * FF_CHUNK)
        up_g = jnp.dot(x1b, wu_ref[:, cs], preferred_element_type=F32)
        up_v = jnp.dot(x1b, wu_ref[:, cs1], preferred_element_type=F32)
        hg = cw_ref[0:1, cs] * fst_ref[:, cs] + cw_ref[1:2, cs] * fst_ref[:, cs1] + cw_ref[2:3, cs] * up_g
        fnew_ref[:, cs] = fst_ref[:, cs1]
        fnew_ref[:, cs1] = up_g
        ffn = ffn + jnp.dot((_gelu(hg) * up_v).astype(BF16), wd_ref[cs, :], preferred_element_type=F32)
    y_ref[...] = _post_tail(x1, ffn, p_ref[...], lng_ref, lnb_ref, wg_ref, wp_ref)


def _post_weight_specs():
    return [_resident((D_MODEL, D_MODEL)), _resident((3, D_MODEL)), _resident((3, D_MODEL)),
            _resident((D_MODEL, 2 * D_FF)), _resident((FFN_CONV, D_FF)), _resident((D_FF, D_MODEL)),
            _resident((PLE_DIM, D_MODEL)), _resident((D_MODEL, D_MODEL))]


_POST_WEIGHT_BYTES = 2 * (2 * D_MODEL * D_MODEL + 3 * D_MODEL * D_FF + PLE_DIM * D_MODEL) + 4 * 9 * D_FF


def _post_prompt(onsa3, og3, x3, p3, p_base, wts, tm):
    b, t, _ = x3.shape
    row = lambda n: pl.BlockSpec((None, tm, n), lambda i, j: (i, j, 0))
    p_spec = pl.BlockSpec((None, tm, PLE_DIM), lambda i, j: (i + p_base, j, 0))
    est = _POST_WEIGHT_BYTES + 2 * tm * (512 + 512 + 1024 + 256 + 1024) * 4 + (tm + 8) * D_FF * 4 + 12 * tm * 1024 * 4
    return pl.pallas_call(
        _post_prompt_body,
        grid=(b, t // tm),
        in_specs=[row(512), row(512), row(D_MODEL), p_spec] + _post_weight_specs(),
        out_specs=[row(D_MODEL), pl.BlockSpec((None, FFN_CONV - 1, D_FF), lambda i, j: (i, 0, 0))],
        out_shape=[jax.ShapeDtypeStruct((b, t, D_MODEL), F32),
                   jax.ShapeDtypeStruct((b, FFN_CONV - 1, D_FF), F32)],
        scratch_shapes=[pltpu.VMEM((tm + SUBLANES, D_FF), F32)],
        compiler_params=_params(("parallel", "arbitrary"), est),
        name="post_prompt",
    )(onsa3, og3, x3, p3, *wts)


def _post_decode(onsa2, og2, x2, p2, fst2, wts):
    m = x2.shape[0]
    full = lambda n: pl.BlockSpec((m, n), lambda i: (0, 0))
    est = _POST_WEIGHT_BYTES + 2 * m * (512 + 512 + 1024 + 256 + 1024 + 4 * D_FF) * 4 + 12 * m * 1024 * 4
    return pl.pallas_call(
        _post_decode_body,
        grid=(1,),
        in_specs=[full(512), full(512), full(D_MODEL), full(PLE_DIM), full(2 * D_FF)] + _post_weight_specs(),
        out_specs=[full(D_MODEL), full(2 * D_FF)],
        out_shape=[jax.ShapeDtypeStruct((m, D_MODEL), F32), jax.ShapeDtypeStruct((m, 2 * D_FF), F32)],
        compiler_params=_params(("arbitrary",), est),
        name="post_decode",
    )(onsa2, og2, x2, p2, fst2, *wts)


DEC_AUX0 = 96


def _decode_aux(n_pages, page):
    shape = (n_pages, LANES, page)
    pos = (lax.broadcasted_iota(jnp.int32, shape, 0) * page + lax.broadcasted_iota(jnp.int32, shape, 2)).astype(F32)
    row = lax.broadcasted_iota(jnp.int32, shape, 1)
    hi = jnp.floor(pos * (1.0 / SLC_BLOCK))
    aux = jnp.where(row == DEC_AUX0, hi, jnp.where(row == DEC_AUX0 + 1, pos - hi * SLC_BLOCK,
                    jnp.where((row == DEC_AUX0 + 2) | (row == DEC_AUX0 + 3), 1.0, 0.0)))
    assert 2 * (n_pages * page // SLC_BLOCK) <= DEC_AUX0
    aux = jnp.where((row < DEC_AUX0) & (row % 2 == 0) & (hi == (row // 2).astype(F32)), NEG, aux)
    return aux.astype(BF16)


def _nsa_decode_body(pt_ref, *refs, n_pages, page, chained, seqs):
    del pt_ref
    n_in = seqs * n_pages
    page_refs = refs[:n_in]
    (q_ref, kvn_ref, winn_ref, sm_ref, wbuf_ref, pe_ref, phi_ref, daux_ref) = refs[n_in:n_in + 8]
    o_ref, wout_ref = refs[n_in + 8 + (1 if chained else 0):]
    chains = [_nsa_decode_seq(page_refs[s * n_pages:(s + 1) * n_pages], q_ref.at[s], kvn_ref.at[s], winn_ref.at[s],
                              sm_ref.at[s], wbuf_ref.at[s], pe_ref, phi_ref, daux_ref, o_ref.at[s], wout_ref.at[s], page)
              for s in range(seqs)]
    for _ in zip(*chains):
        pass


def _nsa_decode_seq(page_refs, q_ref, kvn_ref, winn_ref, sm_ref, wbuf_ref, pe_ref, phi_ref, daux_ref, o_ref, wout_ref,
                    page):
    n_pages = len(page_refs)
    past = n_pages * page
    ncv = past // CMP_BLOCK
    nsb = past // SLC_BLOCK
    assert ncv <= LANES and ncv % 2 == 0
    q = q_ref[...] * (HEAD_DIM ** -0.5)
    zero = jnp.zeros((1, HEAD_DIM), F32)
    rows = []
    for h in range(N_HEADS):
        piece = q[:, h * HEAD_DIM:(h + 1) * HEAD_DIM]
        rows.append(jnp.concatenate([piece, zero] if h < GROUP else [zero, piece], axis=1))
    qbd = jnp.concatenate(rows, axis=0)
    qbd_b = qbd.astype(BF16)
    hrow = lax.broadcasted_iota(jnp.int32, (N_HEADS, 1), 0)
    slope = jnp.concatenate([jnp.full((1, 1), 2.0 ** -(h + 1), F32) for h in range(N_HEADS)], axis=0)

    zero_hh = jnp.zeros((HEAD_DIM, HEAD_DIM), F32)
    both_heads = lambda w: jnp.concatenate([jnp.concatenate([w, zero_hh], axis=1),
                                            jnp.concatenate([zero_hh, w], axis=1)], axis=0)
    pe_row = lambda which: jnp.concatenate([jnp.mean(pe_ref[which], axis=0, keepdims=True)] * N_KV, axis=1)
    pool = jnp.where(lax.broadcasted_iota(jnp.int32, (past, ncv), 0) // CMP_BLOCK
                     == lax.broadcasted_iota(jnp.int32, (past, ncv), 1), 1.0, 0.0)
    pool_t = jnp.where(lax.broadcasted_iota(jnp.int32, (ncv, past), 1) // CMP_BLOCK
                       == lax.broadcasted_iota(jnp.int32, (ncv, past), 0), 1.0, 0.0).astype(BF16)
    q_phi = lax.dot_general(qbd, both_heads(phi_ref[0]), _NT, precision=lax.Precision.HIGHEST,
                            preferred_element_type=F32)
    yield
    row_s = jnp.concatenate([_dot(q_phi, pr[0:KV_W, :]) for pr in page_refs], axis=1)
    yield
    cidx = lax.broadcasted_iota(jnp.int32, (1, ncv), 1)
    c_mid = cidx.astype(F32) * CMP_BLOCK + (CMP_BLOCK - 1) / 2
    s = _dot_sel(row_s, pool) * (1.0 / CMP_BLOCK) + jnp.sum(q_phi * pe_row(0), axis=-1, keepdims=True) \
        - slope * (float(past) - c_mid)
    p = _softmax_rows(s)
    yield
    p_rows = jnp.dot(p.astype(BF16), pool_t, preferred_element_type=F32) * (1.0 / CMP_BLOCK)
    u = jnp.sum(p, axis=-1, keepdims=True) * pe_row(1)
    for i, pr in enumerate(page_refs):
        u = u + _dot_nt(p_rows[:, i * page:(i + 1) * page], pr[KV_W:2 * KV_W, :])
    yield
    o_cmp = jnp.dot(u, both_heads(phi_ref[1]), precision=lax.Precision.HIGHEST, preferred_element_type=F32)

    p_l = p if ncv == LANES else jnp.concatenate([p, jnp.zeros((N_HEADS, LANES - ncv), F32)], axis=1)
    pair = p_l + pltpu.roll(p_l, LANES - 1, 1)
    pooled = jnp.concatenate([jnp.sum(pair[n * GROUP:(n + 1) * GROUP], axis=0, keepdims=True) for n in range(N_KV)],
                             axis=0)
    lane = lax.broadcasted_iota(jnp.int32, (N_KV, LANES), 1)
    blk = lane // 2
    is_blk = ((lane % 2) == 0) & (lane < ncv)
    is_cur = lane == ncv
    forced = is_cur | (is_blk & ((blk == 0) | ((nsb - blk) < N_LOCAL)))
    score = jnp.where(forced, BIG, jnp.where(is_blk, pooled, -BIG))
    rank = jnp.zeros((N_KV, LANES), F32)
    cand = [2 * b for b in range(nsb)] + [ncv]
    for lc in cand:
        col = score[:, lc:lc + 1]
        rank = rank + jnp.where(lane > lc, jnp.where(col >= score, 1.0, 0.0), jnp.where(col > score, 1.0, 0.0))
    sel = jnp.where((rank < float(min(TOP_N, nsb + 1))) & (is_blk | is_cur), 1.0, 0.0)
    unsel = jnp.where(is_blk, 1.0 - sel, 0.0)
    unsel8 = jnp.concatenate([jnp.broadcast_to(unsel[n:n + 1], (GROUP, LANES)) for n in range(N_KV)], axis=0)

    lane8 = lax.broadcasted_iota(jnp.int32, (N_HEADS, LANES), 1)
    q_side = unsel8 + jnp.where(lane8 == DEC_AUX0, slope * SLC_BLOCK,
                                jnp.where(lane8 == DEC_AUX0 + 1, slope,
                                          jnp.where(lane8 == DEC_AUX0 + 2, -slope * float(past - past % SLC_BLOCK),
                                                    jnp.where(lane8 == DEC_AUX0 + 3, -slope * float(past % SLC_BLOCK), 0.0))))
    lhs_cat = jnp.concatenate([qbd, q_side], axis=1).astype(BF16)
    kvn = kvn_ref[...]
    s_new = jnp.sum(qbd * kvn[:, 2 * KV_W:3 * KV_W], axis=-1, keepdims=True)
    s_parts = []
    for i, pr in enumerate(page_refs):
        rhs_cat = jnp.concatenate([pr[2 * KV_W:3 * KV_W, :].astype(BF16), daux_ref[i]], axis=0)
        s_parts.append(jnp.dot(lhs_cat, rhs_cat, preferred_element_type=F32))
    yield
    s_all = jnp.concatenate(s_parts, axis=1)
    m = jnp.maximum(jnp.max(s_all, axis=-1, keepdims=True), s_new)
    e = jnp.exp(s_all - m)
    e_new = jnp.exp(s_new - m)
    acc = e_new * kvn[:, 3 * KV_W:4 * KV_W]
    for i, pr in enumerate(page_refs):
        acc = acc + _dot_nt(e[:, i * page:(i + 1) * page], pr[3 * KV_W:4 * KV_W, :])
    yield
    o_slc = acc / (jnp.sum(e, axis=-1, keepdims=True) + e_new)

    wb = wbuf_ref.shape[1]
    winn = winn_ref[...]
    rpos = lax.broadcasted_iota(jnp.int32, (1, wb), 1)
    dist = wb - rpos
    sw = _dot(qbd_b, wbuf_ref[0:KV_W, :]) - slope * dist.astype(F32)
    sw = jnp.where(dist < WINDOW, sw, NEG)
    sw_new = jnp.sum(qbd * winn[:, 0:KV_W], axis=-1, keepdims=True)
    mw = jnp.maximum(jnp.max(sw, axis=-1, keepdims=True), sw_new)
    ew = jnp.exp(sw - mw)
    ew_new = jnp.exp(sw_new - mw)
    o_win = (_dot_nt(ew, wbuf_ref[KV_W:2 * KV_W, :]) + ew_new * winn[:, KV_W:2 * KV_W]) \
        / (jnp.sum(ew, axis=-1, keepdims=True) + ew_new)
    winn_col = jnp.broadcast_to(winn, (SUBLANES, 2 * KV_W)).T[:, 0:1]
    shifted = pltpu.roll(wbuf_ref[...], wb - 1, 1)
    wout_ref[...] = jnp.where(lax.broadcasted_iota(jnp.int32, (2 * KV_W, wb), 1) == wb - 1, winn_col, shifted)

    gates = _sigmoid(sm_ref[:, 0:3 * N_HEADS])
    glane = lax.broadcasted_iota(jnp.int32, (N_HEADS, 3 * N_HEADS), 1)
    gsel = lambda r: jnp.sum(jnp.where(glane == 3 * hrow + r, gates, 0.0), axis=-1, keepdims=True)
    o8 = gsel(0) * o_cmp + gsel(1) * o_slc + gsel(2) * o_win
    pieces = []
    for h in range(N_HEADS):
        n = h // GROUP
        pieces.append(o8[h:h + 1, n * HEAD_DIM:(n + 1) * HEAD_DIM])
    o_ref[...] = jnp.concatenate(pieces, axis=1)
    yield


def _nsa_decode(page_table, cache3, page_base, q3, kvn3, winn3, sm3, wbuf3, seq_base, pe, phi, wout_prev):
    nb, n_pages = page_table.shape
    page = cache3.shape[2]
    wb = wbuf3.shape[2]
    assert wb == WINDOW and n_pages * page >= wb
    chained = wout_prev is not None
    seqs = DEC_SEQS if nb % DEC_SEQS == 0 else 1
    assert seq_base % seqs == 0
    page_specs = [pl.BlockSpec((None, 4 * KV_W, page),
                               functools.partial(lambda i, pt, s, k: (pt[i * seqs + s, k] + page_base, 0, 0), s=s, k=k))
                  for s in range(seqs) for k in range(n_pages)]
    one = lambda n: pl.BlockSpec((seqs, 1, n), lambda i, pt: (i, 0, 0))
    win_spec = pl.BlockSpec((seqs, 2 * KV_W, wb), lambda i, pt: (i + seq_base // seqs, 0, 0))
    in_specs = page_specs + [one(512), one(512), one(256), one(SM_W), win_spec,
                             pl.BlockSpec((2, CMP_BLOCK, HEAD_DIM), lambda i, pt: (0, 0, 0)),
                             pl.BlockSpec((2, HEAD_DIM, HEAD_DIM), lambda i, pt: (0, 0, 0)),
                             pl.BlockSpec((n_pages, LANES, page), lambda i, pt: (0, 0, 0))]
    args = [page_table] + [cache3] * (seqs * n_pages) + [q3, kvn3, winn3, sm3, wbuf3, pe, phi,
                                                          _decode_aux(n_pages, page)]
    aliases = {}
    if chained:
        in_specs.append(pl.BlockSpec(memory_space=pl.ANY))
        args.append(wout_prev)
        aliases = {len(args) - 1: 1}
    grid_spec = pltpu.PrefetchScalarGridSpec(
        num_scalar_prefetch=1, grid=(nb // seqs,), in_specs=in_specs, out_specs=[one(512), win_spec])
    est = seqs * (2 * n_pages * page * 512 * 4 + 4 * wb * 256 * 4) + 8 * 1024 * 1024
    return pl.pallas_call(
        functools.partial(_nsa_decode_body, n_pages=n_pages, page=page, chained=chained, seqs=seqs),
        grid_spec=grid_spec,
        out_shape=[jax.ShapeDtypeStruct((nb, 1, 512), F32), jax.ShapeDtypeStruct(wbuf3.shape, F32)],
        input_output_aliases=aliases,
        compiler_params=_params(("arbitrary",), est),
        name="nsa_decode",
    )(*args)


def _gdn_dec_prep_body(gqkv_ref, cst_ref, sm_ref, cw_ref, alog_ref, dtb_ref, qn_ref, kn_ref, v_ref, db_ref, cnew_ref):
    w = GDN_QKV
    xn = gqkv_ref[...]
    y = cw_ref[GDN_CONV - 1:GDN_CONV, :] * xn
    for tap in range(GDN_CONV - 1):
        y = y + cw_ref[tap:tap + 1, :] * cst_ref[:, tap * w:(tap + 1) * w]
    for tap in range(GDN_CONV - 2):
        cnew_ref[:, tap * w:(tap + 1) * w] = cst_ref[:, (tap + 1) * w:(tap + 2) * w]
    cnew_ref[:, (GDN_CONV - 2) * w:(GDN_CONV - 1) * w] = xn
    act = _silu(y)
    ind = _head_indicator(GDN_W, N_GDN)
    ind_t = _head_indicator_t(N_GDN, GDN_W)
    qraw = act[:, 0:GDN_W]
    kraw = act[:, GDN_W:2 * GDN_W]
    qn_ref[...] = (qraw * _l2_scale(qraw, ind, ind_t) * (HEAD_DIM ** -0.5)).T
    kn_ref[...] = (kraw * _l2_scale(kraw, ind, ind_t)).T
    v_ref[...] = act[:, 2 * GDN_W:3 * GDN_W].T
    sm = sm_ref[...]
    g = -jnp.exp(alog_ref[...]) * _softplus(sm[:, SM_GA:SM_GA + N_GDN] + dtb_ref[...])
    beta = _sigmoid(sm[:, SM_GB:SM_GB + N_GDN])
    m = sm.shape[0]
    db_ref[...] = jnp.concatenate([jnp.exp(g), beta, jnp.zeros((m, SM_W - 2 * N_GDN), F32)], axis=1).T


def _gdn_dec_prep(gqkv2, cst2, sm2, cw, alog, dtb):
    m = gqkv2.shape[0]
    full = lambda n: pl.BlockSpec((m, n), lambda i: (0, 0))
    full_t = lambda n: pl.BlockSpec((n, m), lambda i: (0, 0))
    est = 2 * m * (8 * GDN_QKV + 4 * 512) * 4 + 8 * m * GDN_QKV * 4
    return pl.pallas_call(
        _gdn_dec_prep_body,
        grid=(1,),
        in_specs=[full(GDN_QKV), full((GDN_CONV - 1) * GDN_QKV), full(SM_W),
                  pl.BlockSpec((GDN_CONV, GDN_QKV), lambda i: (0, 0)),
                  pl.BlockSpec((1, N_GDN), lambda i: (0, 0)), pl.BlockSpec((1, N_GDN), lambda i: (0, 0))],
        out_specs=[full_t(GDN_W), full_t(GDN_W), full_t(GDN_W), full_t(SM_W), full((GDN_CONV - 1) * GDN_QKV)],
        out_shape=[jax.ShapeDtypeStruct((GDN_W, m), F32)] * 3 + [jax.ShapeDtypeStruct((SM_W, m), F32),
                   jax.ShapeDtypeStruct((m, (GDN_CONV - 1) * GDN_QKV), F32)],
        compiler_params=_params(("arbitrary",), est),
        name="gdn_decode_prep",
    )(gqkv2, cst2, sm2, cw, alog.reshape(1, N_GDN), dtb.reshape(1, N_GDN))


def _gdn_dec_step_body(qn_ref, kn_ref, v_ref, db_ref, gz_ref, s_ref, nw_ref, *out_refs):
    o_ref, snew_ref = out_refs[-2:]
    h = pl.program_id(0)
    a = db_ref[pl.ds(h, 1), :]
    bta = db_ref[pl.ds(N_GDN + h, 1), :]
    sk = jnp.zeros(v_ref.shape, F32)
    for d in range(HEAD_DIM):
        sk = sk + kn_ref[d:d + 1, :] * s_ref[d]
    vn = bta * (v_ref[...] - a * sk)
    o = jnp.zeros(v_ref.shape, F32)
    for d in range(HEAD_DIM):
        s_d = a * s_ref[d] + kn_ref[d:d + 1, :] * vn
        snew_ref[d] = s_d
        o = o + qn_ref[d:d + 1, :] * s_d
    o = o * lax.rsqrt(jnp.mean(o * o, axis=0, keepdims=True) + RMS_EPS) * nw_ref[...]
    o_ref[...] = o * _silu(gz_ref[...])


def _gdn_dec_step(qn_t, kn_t, v_t, db_t, gz_t, s4, layer_base, nw_col, s_prev):
    nb = qn_t.shape[1]
    head = lambda: pl.BlockSpec((HEAD_DIM, nb), lambda i: (i, 0))
    st = pl.BlockSpec((None, HEAD_DIM, HEAD_DIM, nb), lambda i: (i + layer_base, 0, 0, 0))
    in_specs = [head(), head(), head(), pl.BlockSpec((SM_W, nb), lambda i: (0, 0)), head(), st,
                pl.BlockSpec((HEAD_DIM, 1), lambda i: (0, 0))]
    args = [qn_t, kn_t, v_t, db_t, gz_t, s4, nw_col]
    aliases = {}
    if s_prev is not None:
        in_specs.append(pl.BlockSpec(memory_space=pl.ANY))
        args.append(s_prev)
        aliases = {len(args) - 1: 1}
    est = 4 * HEAD_DIM * HEAD_DIM * nb * 4 + 4 * 1024 * 1024
    return pl.pallas_call(
        _gdn_dec_step_body,
        grid=(N_GDN,),
        in_specs=in_specs,
        out_specs=[head(), st],
        out_shape=[jax.ShapeDtypeStruct((GDN_W, nb), F32), jax.ShapeDtypeStruct(s4.shape, F32)],
        input_output_aliases=aliases,
        compiler_params=_params(("parallel",), est),
        name="gdn_decode_step",
    )(*args)


def _prep_w_in(w):
    small = jnp.concatenate([w[:, C_GATE:C_GQKV], w[:, C_GA:C_GB], w[:, C_GB:C_GZ],
                             jnp.zeros((D_MODEL, SM_W - (C_GQKV - C_GATE) - 2 * N_GDN), w.dtype)], axis=1)
    return jnp.concatenate([w[:, 0:C_KV], w[:, C_KV:C_WIN], w[:, C_WIN:C_GATE], w[:, C_GQKV:C_GA],
                            w[:, C_GZ:IN_WIDTH], small], axis=1).astype(BF16)


def kernel(x_prompt, x_sample, cache_nsa_kv, state_nsa_win, state_gdn, state_gdn_conv, state_ffn_conv, page_table, p_prompt, p_sample, w_in, nsa_pe, nsa_phi, gdn_conv_w, gdn_A_log, gdn_dt_bias, gdn_norm_w, w_out, ln_g, ln_b, ffn_w_up, ffn_conv_w, ffn_w_down, ple_w_proj, ple_w_gate):
    b, t, d = x_prompt.shape
    nb, ts, _ = x_sample.shape
    assert d == D_MODEL and ts == 1 and t % Q_BLOCK == 0 and t >= WINDOW
    depth = w_in.shape[0]
    n_pool, page = cache_nsa_kv.shape[1], cache_nsa_kv.shape[2]
    wb = state_nsa_win.shape[2]
    tm_p = 512
    xp, xs = x_prompt, x_sample.reshape(nb, d)
    cache_all = cache_nsa_kv.transpose(0, 1, 3, 4, 5, 2).reshape(depth * n_pool, 4 * KV_W, page)
    wbuf_all = state_nsa_win.transpose(0, 1, 3, 4, 5, 2).reshape(depth * nb, 2 * KV_W, wb)
    gstate_all = state_gdn.transpose(0, 2, 3, 4, 1).reshape(depth * N_GDN, HEAD_DIM, HEAD_DIM, nb)
    p_all = p_prompt.reshape(depth * b, t, PLE_DIM)
    win_all = None
    gs_all = None
    kvt_all = None
    st_p, st_s = [], []
    for l in range(depth):
        w_cat = _prep_w_in(w_in[l])
        post_w = (w_out[l].astype(BF16), ln_g[l], ln_b[l], ffn_w_up[l].astype(BF16), ffn_conv_w[l],
                  ffn_w_down[l].astype(BF16), ple_w_proj[l].astype(BF16), ple_w_gate[l].astype(BF16))
        nw_t = jnp.tile(gdn_norm_w[l].reshape(1, HEAD_DIM), (1, N_GDN))

        q, kv, win, gqkv, gz, sm, kvt_all = _proj_in(xp.reshape(b * t, d), w_cat, 512, (kvt_all, l * b, t, depth * b))
        kv3, win3 = kv.reshape(b, t, 512), win.reshape(b, t, 256)
        sm3, gqkv3 = sm.reshape(b, t, SM_W), gqkv.reshape(b, t, GDN_QKV)
        kc, vc = _compress(kv3, nsa_pe[l], nsa_phi[l])
        o_nsa = _nsa_prompt(q.reshape(b, t, 512), kv3, kc, vc, win3, sm3)
        o_g, s_p = _gdn_prompt(gqkv3, gz.reshape(b, t, GDN_W), sm3, gdn_conv_w[l], gdn_A_log[l], gdn_dt_bias[l], nw_t)
        xp, fbuf_p = _post_prompt(o_nsa, o_g, xp, p_all, l * b, post_w, tm_p)
        st_p.append((None, win3[:, t - wb:].reshape(b, wb, 2, N_KV, HEAD_DIM),
                     s_p, gqkv3[:, t - (GDN_CONV - 1):], fbuf_p))

        q, kv, win, gqkv, gz, sm = _proj_in(xs, w_cat, nb)
        o_nsa, win_all = _nsa_decode(
            page_table, cache_all, l * n_pool, q.reshape(nb, 1, 512), kv.reshape(nb, 1, 512),
            win.reshape(nb, 1, 256), sm.reshape(nb, 1, SM_W), wbuf_all, l * nb, nsa_pe[l], nsa_phi[l], win_all)
        qn_t, kn_t, v_t, db_t, cnew = _gdn_dec_prep(gqkv, state_gdn_conv[l].reshape(nb, (GDN_CONV - 1) * GDN_QKV), sm,
                                                    gdn_conv_w[l], gdn_A_log[l], gdn_dt_bias[l])
        o_g_t, gs_all = _gdn_dec_step(qn_t, kn_t, v_t, db_t, gz.T, gstate_all, l * N_GDN,
                                      gdn_norm_w[l].reshape(HEAD_DIM, 1), gs_all)
        xs, fnew = _post_decode(o_nsa.reshape(nb, 512), o_g_t.T, xs, p_sample[l].reshape(nb, PLE_DIM),
                                state_ffn_conv[l].reshape(nb, (FFN_CONV - 1) * D_FF), post_w)
        st_s.append((kv.reshape(nb, 1, 4, N_KV, HEAD_DIM), None, None, cnew.reshape(nb, GDN_CONV - 1, GDN_QKV),
                     fnew.reshape(nb, FFN_CONV - 1, D_FF)))

    sample_states = {
        1: win_all.reshape(depth, nb, 2, N_KV, HEAD_DIM, wb).transpose(0, 1, 5, 2, 3, 4),
        2: gs_all.reshape(depth, N_GDN, HEAD_DIM, HEAD_DIM, nb).transpose(0, 4, 1, 2, 3),
    }
    kv_rows_prompt = kvt_all.reshape(depth, b, 4, N_KV, HEAD_DIM, t).transpose(0, 1, 5, 2, 3, 4)
    outs = [xp, xs.reshape(nb, 1, d)]
    for k in range(5):
        outs.append(kv_rows_prompt if k == 0 else jnp.stack([s[k] for s in st_p]))
        outs.append(sample_states[k] if k in sample_states else jnp.stack([s[k] for s in st_s]))
    return tuple(outs)
```

```python
import functools

import jax
import jax.numpy as jnp
from jax import lax
from jax.experimental import pallas as pl
from jax.experimental.pallas import tpu as pltpu

F32 = jnp.float32
BF16 = jnp.bfloat16

D_MODEL = 1024
DEPTH = 2
HEAD_DIM = 64
N_HEADS = 8
N_KV = 2
GROUP = 4
N_GDN = 8
KV_W = N_KV * HEAD_DIM
GDN_W = N_GDN * HEAD_DIM
GDN_QKV = 3 * GDN_W
CMP_BLOCK = 32
SLC_BLOCK = 64
TOP_N = 16
N_LOCAL = 2
WINDOW = 512
Q_BLOCK = 128
GDN_CHUNK = 64
GDN_CONV = 4
FFN_CONV = 3
D_FF = 2816
PLE_DIM = 256
LN_EPS = 1e-5
RMS_EPS = 1e-6
DN_ALPHA = (2 * DEPTH) ** 0.25
NEG = -1e30
BIG = 1e30

C_KV = 512
C_WIN = 1024
C_GATE = 1280
C_GQKV = 1304
C_GA = 2840
C_GB = 2848
C_GZ = 2856
IN_WIDTH = 3368
SM_W = 128
SM_GA = 24
SM_GB = 32

LANES = 128
SUBLANES = 8
VMEM_CAP = 56 * 1024 * 1024
FF_CHUNK = 1408
SLC_TILE = 512
DEC_SEQS = 4

_NT = (((1,), (1,)), ((), ()))
_TN = (((0,), (0,)), ((), ()))


def _dot(a, b):
    return jnp.dot(a.astype(BF16), b.astype(BF16), preferred_element_type=F32)


def _dot_nt(a, b):
    return lax.dot_general(a.astype(BF16), b.astype(BF16), _NT, preferred_element_type=F32)


def _dot_tn(a, b):
    return lax.dot_general(a.astype(BF16), b.astype(BF16), _TN, preferred_element_type=F32)


def _dot_f32(a, b):
    return jnp.dot(a, b, precision=lax.Precision.HIGHEST, preferred_element_type=F32)


def _split3(a):
    hi = a.astype(BF16)
    r = a - hi.astype(F32)
    mid = r.astype(BF16)
    lo = (r - mid.astype(F32)).astype(BF16)
    return hi, mid, lo


def _dot_sel(a, sel):
    hi, mid, lo = _split3(a)
    sel = sel.astype(BF16)
    d = lambda u: jnp.dot(u, sel, preferred_element_type=F32)
    return d(hi) + d(mid) + d(lo)


def _dot_tn_sel(sel, a):
    hi, mid, lo = _split3(a)
    sel = sel.astype(BF16)
    d = lambda u: lax.dot_general(sel, u, _TN, preferred_element_type=F32)
    return d(hi) + d(mid) + d(lo)


def _sigmoid(x):
    return 1.0 / (1.0 + jnp.exp(-x))


def _silu(x):
    return x * _sigmoid(x)


def _softplus(x):
    return jnp.maximum(x, 0.0) + jnp.log1p(jnp.exp(-jnp.abs(x)))


def _layer_norm(x, g, b):
    mu = jnp.mean(x, axis=-1, keepdims=True)
    xc = x - mu
    var = jnp.mean(xc * xc, axis=-1, keepdims=True)
    return xc * lax.rsqrt(var + LN_EPS) * g + b


def _gelu(x):
    return 0.5 * x * (1.0 + lax.erf(x * (0.5 ** 0.5)))


def _head_indicator(width, heads):
    r = lax.broadcasted_iota(jnp.int32, (width, heads), 0) // HEAD_DIM
    c = lax.broadcasted_iota(jnp.int32, (width, heads), 1)
    return jnp.where(r == c, 1.0, 0.0).astype(BF16)


def _head_indicator_t(heads, width):
    r = lax.broadcasted_iota(jnp.int32, (heads, width), 0)
    c = lax.broadcasted_iota(jnp.int32, (heads, width), 1) // HEAD_DIM
    return jnp.where(r == c, 1.0, 0.0).astype(BF16)


def _params(sem, est_bytes):
    limit = int(min(max(2 * est_bytes, 32 * 1024 * 1024), VMEM_CAP))
    return pltpu.CompilerParams(dimension_semantics=sem, vmem_limit_bytes=limit)


def _resident(shape):
    nd = len(shape)
    return pl.BlockSpec(shape, lambda *_: (0,) * nd, pipeline_mode=pl.Buffered(1))


_PROJ_WIDTHS = (512, 512, 256, GDN_QKV, GDN_W, SM_W)


def _proj_in_body(x_ref, w_ref, *refs, n_extra_in, emit_kv_t):
    out_refs = refs[n_extra_in:]
    xb = x_ref[...].astype(BF16)
    off = 0
    for ref, n in zip(out_refs, _PROJ_WIDTHS):
        ref[...] = jnp.dot(xb, w_ref[:, off:off + n], preferred_element_type=F32)
        off += n
    if emit_kv_t:
        out_refs[len(_PROJ_WIDTHS)][...] = out_refs[1][...].T


def _proj_in(x2, w_cat, tm, kv_t=None):
    m = x2.shape[0]
    tot = sum(_PROJ_WIDTHS)
    est = 2 * tm * D_MODEL * 4 + D_MODEL * tot * 2 + 4 * tm * tot * 4
    in_specs = [pl.BlockSpec((tm, D_MODEL), lambda i: (i, 0)), _resident((D_MODEL, tot))]
    out_specs = [pl.BlockSpec((tm, n), lambda i: (i, 0)) for n in _PROJ_WIDTHS]
    out_shape = [jax.ShapeDtypeStruct((m, n), F32) for n in _PROJ_WIDTHS]
    args = [x2, w_cat]
    aliases = {}
    if kv_t is not None:
        prev, slab_base, rows, n_slabs = kv_t
        per_seq = rows // tm
        out_specs.append(pl.BlockSpec((None, 512, tm), lambda i: (slab_base + i // per_seq, 0, i % per_seq)))
        out_shape.append(jax.ShapeDtypeStruct((n_slabs, 512, rows), F32))
        if prev is not None:
            in_specs.append(pl.BlockSpec(memory_space=pl.ANY))
            args.append(prev)
            aliases = {2: len(_PROJ_WIDTHS)}
    return pl.pallas_call(
        functools.partial(_proj_in_body, n_extra_in=len(args) - 2, emit_kv_t=kv_t is not None),
        grid=(m // tm,),
        in_specs=in_specs,
        out_specs=out_specs,
        out_shape=out_shape,
        input_output_aliases=aliases,
        compiler_params=_params(("parallel",), est),
        name="proj_in",
    )(*args)


def _compress_rows(rows, pe, phi):
    nblk = rows.shape[0] // CMP_BLOCK
    mean = rows.reshape(nblk, CMP_BLOCK, KV_W).sum(axis=1) * (1.0 / CMP_BLOCK)
    pem = jnp.mean(pe, axis=0, keepdims=True)
    outs = []
    for n in range(N_KV):
        outs.append(_dot_f32(mean[:, n * HEAD_DIM:(n + 1) * HEAD_DIM] + pem, phi))
    return jnp.concatenate(outs, axis=1)


def _compress_body(kv_ref, pe_ref, phi_ref, kc_ref, vc_ref):
    kc_ref[...] = _compress_rows(kv_ref[:, 0:KV_W], pe_ref[0], phi_ref[0])
    vc_ref[...] = _compress_rows(kv_ref[:, KV_W:2 * KV_W], pe_ref[1], phi_ref[1])


def _compress(kv3, pe, phi):
    b, t, _ = kv3.shape
    nc = t // CMP_BLOCK
    est = 2 * t * 256 * 4 + 4 * nc * KV_W * 4
    return pl.pallas_call(
        _compress_body,
        grid=(b,),
        in_specs=[pl.BlockSpec((None, t, 2 * KV_W), lambda i: (i, 0, 0)),
                  _resident((2, CMP_BLOCK, HEAD_DIM)), _resident((2, HEAD_DIM, HEAD_DIM))],
        out_specs=[pl.BlockSpec((None, nc, KV_W), lambda i: (i, 0, 0))] * 2,
        out_shape=[jax.ShapeDtypeStruct((b, nc, KV_W), F32)] * 2,
        compiler_params=_params(("parallel",), est),
        name="nsa_compress",
    )(kv3, pe, phi)


def _softmax_rows(s):
    m = jnp.max(s, axis=-1, keepdims=True)
    e = jnp.exp(s - m)
    return e / jnp.sum(e, axis=-1, keepdims=True)


def _key_aux(npos, pos_of_row, with_blocks):
    pos = pos_of_row(lax.broadcasted_iota(jnp.int32, (npos, LANES), 0).astype(F32))
    lane = lax.broadcasted_iota(jnp.int32, (npos, LANES), 1)
    hi = jnp.floor(pos * (1.0 / SLC_BLOCK))
    aux = jnp.where(lane == AUX0, hi, jnp.where(lane == AUX0 + 1, pos - hi * SLC_BLOCK,
                    jnp.where((lane == AUX0 + 2) | (lane == AUX0 + 3), 1.0, 0.0)))
    if with_blocks:
        aux = jnp.where(lane < AUX0, jnp.where(hi == lane.astype(F32), NEG, 0.0), aux)
    return aux.astype(BF16)


AUX0 = 64


def _softmax_cols(s):
    m = jnp.max(s, axis=0, keepdims=True)
    e = jnp.exp(s - m)
    return e / jnp.sum(e, axis=0, keepdims=True)


def _nsa_prompt_body(q_ref, slc_ref, kc_ref, vc_ref, win_ref, sm_ref, kaux_ref, caux_ref, o_ref,
                     imp_ref, kcat, wcat, ccat, vts, vtw, vtc, st_buf, *, seq):
    j = pl.program_id(1)
    nc = seq // CMP_BLOCK
    ns = seq // SLC_BLOCK
    tq = Q_BLOCK
    cols4 = GROUP * tq
    assert ns <= AUX0 and seq <= 256 * SLC_BLOCK

    @pl.when(j == 0)
    def _():
        for c0 in range(0, seq, SLC_TILE):
            rs = slice(c0, c0 + SLC_TILE)
            kcat[rs, 0:KV_W] = slc_ref[rs, 0:KV_W].astype(BF16)
            kcat[rs, KV_W:2 * KV_W] = kaux_ref[rs, :]
            wcat[rs, 0:KV_W] = win_ref[rs, 0:KV_W].astype(BF16)
            wcat[rs, KV_W:2 * KV_W] = kaux_ref[rs, :]
            vts[:, rs] = slc_ref[rs, KV_W:2 * KV_W].T.astype(BF16)
            vtw[:, rs] = win_ref[rs, KV_W:2 * KV_W].T.astype(BF16)
        ccat[:, 0:KV_W] = kc_ref[...].astype(BF16)
        ccat[:, KV_W:2 * KV_W] = caux_ref[...]
        vtc[...] = vc_ref[...].T.astype(BF16)

    q_t = (q_ref[...] * (HEAD_DIM ** -0.5)).T
    gates_t = _sigmoid(sm_ref[...].T[0:3 * N_HEADS, :])
    qpos_r = j * tq + lax.broadcasted_iota(jnp.int32, (1, tq), 1)
    qpos4 = jnp.concatenate([qpos_r] * GROUP, axis=1)
    qa = (qpos4 // SLC_BLOCK).astype(F32)
    qb = (qpos4 % SLC_BLOCK).astype(F32)
    arow = lax.broadcasted_iota(jnp.int32, (SUBLANES, cols4), 0)
    t_q = lax.broadcasted_iota(jnp.int32, (tq, cols4), 1) % tq
    t_k = lax.broadcasted_iota(jnp.int32, (tq, cols4), 0)
    zeros_h = jnp.zeros((HEAD_DIM, cols4), F32)
    outs = [None] * N_HEADS
    rhs_slcs, o_cmps, o_wins = [], [], []
    for n in range(N_KV):
        hs = slice(n * HEAD_DIM, (n + 1) * HEAD_DIM)
        qn_t = jnp.concatenate(
            [q_t[(GROUP * n + g) * HEAD_DIM:(GROUP * n + g + 1) * HEAD_DIM, :] for g in range(GROUP)], axis=1)
        slope = jnp.concatenate(
            [jnp.full((1, tq), 2.0 ** -(GROUP * n + g + 1), F32) for g in range(GROUP)], axis=1)
        q_top = jnp.concatenate([qn_t, zeros_h] if n == 0 else [zeros_h, qn_t], axis=0)
        q_pos = jnp.where(arow == 0, slope * SLC_BLOCK,
                          jnp.where(arow == 1, slope,
                                    jnp.where(arow == 2, -slope * SLC_BLOCK * qa,
                                              jnp.where(arow == 3, -slope * qb, 0.0))))
        q_tail = jnp.zeros((2 * KV_W - KV_W - AUX0 - SUBLANES, cols4), F32)

        def rhs(block_rows):
            return jnp.concatenate([q_top, block_rows, q_pos, q_tail], axis=0).astype(BF16)

        rhs_plain = rhs(jnp.zeros((AUX0, cols4), F32))

        c_end = lax.broadcasted_iota(jnp.int32, (nc, 1), 0) * CMP_BLOCK + (CMP_BLOCK - 1)
        c_vis = c_end <= qpos4
        s = jnp.dot(ccat[...], rhs_plain, preferred_element_type=F32)
        p = jnp.where(c_vis, _softmax_cols(jnp.where(c_vis, s, NEG)), 0.0)
        o_cmp = jnp.dot(vtc[hs, :], p.astype(BF16), preferred_element_type=F32)

        imp_ref[...] = p[:, 0:tq] + p[:, tq:2 * tq] + p[:, 2 * tq:3 * tq] + p[:, 3 * tq:4 * tq]
        impb = imp_ref[pl.ds(0, ns, stride=2), :] + imp_ref[pl.ds(1, ns, stride=2), :]
        blk = lax.broadcasted_iota(jnp.int32, (ns, tq), 0)
        cur = qpos_r // SLC_BLOCK
        future = blk > cur
        forced = (blk == 0) | (((cur - blk) < N_LOCAL) & jnp.logical_not(future))
        score = jnp.where(future, -BIG, jnp.where(forced, BIG, impb))
        rank = jnp.zeros((ns, tq), F32)
        for bp in range(ns):
            row = score[bp:bp + 1, :]
            rank = rank + jnp.where(blk > bp, jnp.where(row >= score, 1.0, 0.0), jnp.where(row > score, 1.0, 0.0))
        unsel_t = jnp.where(rank < float(min(TOP_N, ns)), 0.0, 1.0)
        if ns < AUX0:
            unsel_t = jnp.concatenate([unsel_t, jnp.zeros((AUX0 - ns, tq), F32)], axis=0)
        rhs_slcs.append(rhs(jnp.concatenate([unsel_t] * GROUP, axis=1)))

        nwb = WINDOW // tq + 1
        s_parts, starts = [], []
        for i in range(nwb):
            kb = j - (nwb - 1) + i
            r0 = pl.multiple_of(jnp.maximum(kb, 0) * tq, tq)
            sw = jnp.dot(wcat[pl.ds(r0, tq), :], rhs_plain, preferred_element_type=F32)
            if i == 0:
                sw = jnp.where(t_k > t_q, sw, NEG)
            if i == nwb - 1:
                sw = jnp.where(t_k <= t_q, sw, NEG)
            else:
                sw = jnp.where(kb >= 0, sw, NEG)
            s_parts.append(sw)
            starts.append(r0)
        pw = _softmax_cols(jnp.concatenate(s_parts, axis=0)).astype(BF16)
        o_win = jnp.dot(vtw[hs, pl.ds(starts[0], tq)], pw[0:tq], preferred_element_type=F32)
        for i in range(1, nwb):
            o_win = o_win + jnp.dot(vtw[hs, pl.ds(starts[i], tq)], pw[i * tq:(i + 1) * tq], preferred_element_type=F32)

        o_cmps.append(o_cmp)
        o_wins.append(o_win)

    heads = range(N_KV)

    def slc_scores(k0):
        k_rows = kcat[pl.ds(k0, SLC_TILE), :]
        return tuple(jnp.dot(k_rows, rhs_slcs[n], preferred_element_type=F32) for n in heads)

    def slc_absorb(k0, sts, stats, q_limit=None):
        if q_limit is not None:
            kpos = k0 + lax.broadcasted_iota(jnp.int32, (SLC_TILE, 1), 0)
            sts = [jnp.where(kpos <= q_limit, st, NEG) for st in sts]
        m2s = [jnp.maximum(stats[n][0], jnp.max(sts[n], axis=0, keepdims=True)) for n in heads]
        es = [jnp.exp(sts[n] - m2s[n]) for n in heads]
        pvs = [jnp.dot(vts[n * HEAD_DIM:(n + 1) * HEAD_DIM, pl.ds(k0, SLC_TILE)], es[n].astype(BF16),
                       preferred_element_type=F32) for n in heads]
        new = []
        for n in heads:
            m, l, acc = stats[n]
            a = jnp.exp(m - m2s[n])
            new.append((m2s[n], a * l + jnp.sum(es[n], axis=0, keepdims=True), a * acc + pvs[n]))
        return tuple(new)

    def slc_issue(k0, slot):
        for n, st in enumerate(slc_scores(k0)):
            st_buf[slot, n] = st

    def slc_held(slot):
        return [st_buf[slot, n] for n in heads]

    def slc_pair(i, stats):
        k0 = pl.multiple_of(2 * i * SLC_TILE, SLC_TILE)
        slc_issue(k0 + SLC_TILE, 1)
        stats = slc_absorb(k0, slc_held(0), stats)
        slc_issue(k0 + 2 * SLC_TILE, 0)
        return slc_absorb(k0 + SLC_TILE, slc_held(1), stats)

    n_full = (j * tq) // SLC_TILE
    start = (jnp.full((1, cols4), NEG, F32), jnp.zeros((1, cols4), F32), jnp.zeros((HEAD_DIM, cols4), F32))
    slc_issue(0, 0)
    stats = lax.fori_loop(0, n_full // 2, slc_pair, (start,) * N_KV)
    odd = (n_full % 2) == 1
    k_even = pl.multiple_of((n_full // 2) * 2 * SLC_TILE, SLC_TILE)
    stats = slc_absorb(k_even, slc_held(0), stats, q_limit=jnp.where(odd, jnp.int32(seq), qpos4))

    def last_tile(stats):
        k_diag = pl.multiple_of(n_full * SLC_TILE, SLC_TILE)
        return slc_absorb(k_diag, slc_scores(k_diag), stats, q_limit=qpos4)

    carry = lax.cond(odd, last_tile, lambda s: s, stats)

    for n in range(N_KV):
        _, l_s, acc_s = carry[n]
        o_slc = acc_s / l_s
        for g in range(GROUP):
            h = GROUP * n + g
            cs = slice(g * tq, (g + 1) * tq)
            outs[h] = (gates_t[3 * h:3 * h + 1, :] * o_cmps[n][:, cs] + gates_t[3 * h + 1:3 * h + 2, :] * o_slc[:, cs]
                       + gates_t[3 * h + 2:3 * h + 3, :] * o_wins[n][:, cs])
    o_ref[...] = jnp.concatenate(outs, axis=0).T


def _nsa_prompt(q3, kv3, kc, vc, win3, sm3):
    b, t, _ = q3.shape
    assert t % SLC_TILE == 0 and t % Q_BLOCK == 0
    nc = t // CMP_BLOCK
    est = 2 * (2 * t * 256 * 4) + 4 * nc * KV_W * 4 + 8 * Q_BLOCK * 512 * 4 + 16 * 512 * 640 * 4
    kaux = _key_aux(t, lambda r: r, True)
    caux = _key_aux(nc, lambda r: r * CMP_BLOCK + (CMP_BLOCK - 1) / 2, False)
    return pl.pallas_call(
        functools.partial(_nsa_prompt_body, seq=t),
        grid=(b, t // Q_BLOCK),
        in_specs=[
            pl.BlockSpec((None, Q_BLOCK, 512), lambda i, j: (i, j, 0)),
            pl.BlockSpec((None, t, 2 * KV_W), lambda i, j: (i, 0, 1)),
            pl.BlockSpec((None, nc, KV_W), lambda i, j: (i, 0, 0)),
            pl.BlockSpec((None, nc, KV_W), lambda i, j: (i, 0, 0)),
            pl.BlockSpec((None, t, 2 * KV_W), lambda i, j: (i, 0, 0)),
            pl.BlockSpec((None, Q_BLOCK, SM_W), lambda i, j: (i, j, 0)),
            pl.BlockSpec((t, LANES), lambda i, j: (0, 0)),
            pl.BlockSpec((nc, LANES), lambda i, j: (0, 0)),
        ],
        out_specs=pl.BlockSpec((None, Q_BLOCK, 512), lambda i, j: (i, j, 0)),
        out_shape=jax.ShapeDtypeStruct((b, t, 512), F32),
        scratch_shapes=[pltpu.VMEM((nc, Q_BLOCK), F32),
                        pltpu.VMEM((t, 2 * KV_W), BF16), pltpu.VMEM((t, 2 * KV_W), BF16), pltpu.VMEM((nc, 2 * KV_W), BF16),
                        pltpu.VMEM((KV_W, t), BF16), pltpu.VMEM((KV_W, t), BF16), pltpu.VMEM((KV_W, nc), BF16),
                        pltpu.VMEM((2, N_KV, SLC_TILE, GROUP * Q_BLOCK), F32)],
        compiler_params=_params(("parallel", "arbitrary"), est),
        name="nsa_prompt",
    )(q3, kv3, kc, vc, win3, sm3, kaux, caux)


_BNN = (((2,), (1,)), ((0,), (0,)))
_BNT = (((2,), (2,)), ((0,), (0,)))
_BTN = (((1,), (1,)), ((0,), (0,)))


def _bdot(a, b, dims=_BNN):
    return lax.dot_general(a.astype(BF16), b.astype(BF16), dims, preferred_element_type=F32)


def _solve_unit_lower(nmat, rhs):
    c = nmat.shape[1]
    ii = lax.broadcasted_iota(jnp.int32, (c, c), 0)
    jj = lax.broadcasted_iota(jnp.int32, (c, c), 1)
    tmat = jnp.where(ii == jj, 1.0, 0.0) - nmat
    pw = _bdot(nmat, nmat)
    steps = c.bit_length() - 2
    for _ in range(steps - 1):
        both = _bdot(jnp.concatenate([tmat, pw], axis=1), pw)
        tmat = tmat + both[:, 0:c]
        pw = both[:, c:2 * c]
    tmat = tmat + _bdot(tmat, pw)
    return _bdot(tmat, rhs)


def _l2_scale(x, ind, ind_t):
    ssq = _dot_sel(x * x, ind)
    return _dot_sel(lax.rsqrt(ssq + RMS_EPS), ind_t)


def _gdn_prompt_body(gqkv_ref, gz_ref, sm_ref, cw_ref, alog_r_ref, dtb_r_ref, alog_c_ref, dtb_c_ref, nw_ref,
                     o_ref, s_ref, xbuf):
    t = pl.program_id(1)
    c = GDN_CHUNK
    nt = gqkv_ref.shape[0]
    nchunk = nt // c
    pad = SUBLANES

    @pl.when(t == 0)
    def _():
        xbuf[0:pad, :] = jnp.zeros((pad, GDN_QKV), F32)
        s_ref[...] = jnp.zeros(s_ref.shape, F32)

    xbuf[pad:pad + nt, :] = gqkv_ref[...]
    y = cw_ref[GDN_CONV - 1:GDN_CONV, :] * xbuf[pad:pad + nt, :]
    for tap in range(GDN_CONV - 1):
        y = y + cw_ref[tap:tap + 1, :] * xbuf[pad - (GDN_CONV - 1) + tap:pad - (GDN_CONV - 1) + tap + nt, :]
    xbuf[0:pad, :] = xbuf[nt:nt + pad, :]
    act = _silu(y)
    ind = _head_indicator(GDN_W, N_GDN)
    ind_t = _head_indicator_t(N_GDN, GDN_W)
    qraw = act[:, 0:GDN_W]
    kraw = act[:, GDN_W:2 * GDN_W]
    v = act[:, 2 * GDN_W:3 * GDN_W]
    qn = qraw * _l2_scale(qraw, ind, ind_t) * (HEAD_DIM ** -0.5)
    kn = kraw * _l2_scale(kraw, ind, ind_t)

    sm = sm_ref[...]
    sm_t = sm.T
    g_col = -jnp.exp(alog_r_ref[...]) * _softplus(sm[:, SM_GA:SM_GA + N_GDN] + dtb_r_ref[...])
    g_row = -jnp.exp(alog_c_ref[...]) * _softplus(sm_t[SM_GA:SM_GA + N_GDN, :] + dtb_c_ref[...])
    beta = _sigmoid(sm[:, SM_GB:SM_GB + N_GDN])
    ti = lax.broadcasted_iota(jnp.int32, (nt, nt), 0)
    tj = lax.broadcasted_iota(jnp.int32, (nt, nt), 1)
    same = (ti // c) == (tj // c)
    upto = jnp.where(same & (ti <= tj), 1.0, 0.0)
    gc_col = _dot_tn_sel(upto, g_col)
    gc_row = _dot_sel(g_row, upto)

    items = [(g, h) for g in range(nchunk) for h in range(N_GDN)]
    rows = lambda g: slice(g * c, (g + 1) * c)
    heads = lambda x: jnp.stack([x[rows(g), h * HEAD_DIM:(h + 1) * HEAD_DIM] for g, h in items], axis=0)
    cols = lambda x: jnp.stack([x[rows(g), h:h + 1] for g, h in items], axis=0)
    qb, kb, vb = heads(qn), heads(kn), heads(v)
    gcc = cols(gc_col)
    bcol = cols(beta)
    gcr = jnp.stack([gc_row[h:h + 1, rows(g)] for g, h in items], axis=0)
    gl = jnp.stack([gc_row[h:h + 1, (g + 1) * c - 1:(g + 1) * c] for g, h in items], axis=0)
    ii = lax.broadcasted_iota(jnp.int32, (c, c), 0)
    jj = lax.broadcasted_iota(jnp.int32, (c, c), 1)
    dec = jnp.exp(jnp.where(ii >= jj, gcc - gcr, NEG))
    egc = jnp.exp(gcc)
    kbeta = kb * bcol
    nmat = jnp.where(ii > jj, _bdot(kbeta, kb, _BNT) * dec, 0.0)
    x = _solve_unit_lower(nmat, jnp.concatenate([vb * bcol, kbeta * egc], axis=2))
    val, kcd = x[:, :, 0:HEAD_DIM], x[:, :, HEAD_DIM:2 * HEAD_DIM]
    inner = _bdot(qb, kb, _BNT) * dec
    qg = qb * egc
    kend = kb * jnp.exp(gl - gcc)
    egl = jnp.exp(gl)

    s = s_ref[...]
    outs = []
    for g in range(nchunk):
        it = slice(g * N_GDN, (g + 1) * N_GDN)
        vn = val[it] - _bdot(kcd[it], s)
        og = _bdot(qg[it], s) + _bdot(inner[it], vn)
        s = s * egl[it] + _bdot(kend[it], vn, _BTN)
        outs.append(jnp.concatenate([og[h] for h in range(N_GDN)], axis=1))
    s_ref[...] = s
    o = jnp.concatenate(outs, axis=0)
    rms = _dot_sel(lax.rsqrt(_dot_sel(o * o, ind) * (1.0 / HEAD_DIM) + RMS_EPS), ind_t)
    o_ref[...] = o * rms * nw_ref[...] * _silu(gz_ref[...])


GDN_TOK = 4 * GDN_CHUNK


def _gdn_prompt(gqkv3, gz3, sm3, cw, alog, dtb, nw_t):
    b, t, _ = gqkv3.shape
    assert t % GDN_TOK == 0
    c = GDN_TOK
    est = 4 * c * GDN_QKV * 4 + 64 * c * 512 * 4
    return pl.pallas_call(
        _gdn_prompt_body,
        grid=(b, t // c),
        in_specs=[
            pl.BlockSpec((None, c, GDN_QKV), lambda i, j: (i, j, 0)),
            pl.BlockSpec((None, c, GDN_W), lambda i, j: (i, j, 0)),
            pl.BlockSpec((None, c, SM_W), lambda i, j: (i, j, 0)),
            _resident((GDN_CONV, GDN_QKV)), _resident((1, N_GDN)), _resident((1, N_GDN)),
            _resident((N_GDN, 1)), _resident((N_GDN, 1)), _resident((1, GDN_W)),
        ],
        out_specs=[pl.BlockSpec((None, c, GDN_W), lambda i, j: (i, j, 0)),
                   pl.BlockSpec((None, N_GDN, HEAD_DIM, HEAD_DIM), lambda i, j: (i, 0, 0, 0))],
        out_shape=[jax.ShapeDtypeStruct((b, t, GDN_W), F32),
                   jax.ShapeDtypeStruct((b, N_GDN, HEAD_DIM, HEAD_DIM), F32)],
        scratch_shapes=[pltpu.VMEM((c + SUBLANES, GDN_QKV), F32)],
        compiler_params=_params(("parallel", "arbitrary"), est),
        name="gdn_prompt",
    )(gqkv3, gz3, sm3, cw, alog.reshape(1, N_GDN), dtb.reshape(1, N_GDN),
      alog.reshape(N_GDN, 1), dtb.reshape(N_GDN, 1), nw_t)


def _post_tail(x1, ffn, p, lng_ref, lnb_ref, wg_ref, wp_ref):
    x2 = _layer_norm(DN_ALPHA * x1 + ffn, lng_ref[1:2, :], lnb_ref[1:2, :])
    ple = _sigmoid(jnp.dot(x2.astype(BF16), wg_ref[...], preferred_element_type=F32)) \
        * jnp.dot(p.astype(BF16), wp_ref[...], preferred_element_type=F32)
    return _layer_norm(DN_ALPHA * x2 + ple, lng_ref[2:3, :], lnb_ref[2:3, :])


def _post_head(onsa_ref, og_ref, x_ref, wo_ref, lng_ref, lnb_ref):
    attn = jnp.dot(onsa_ref[...].astype(BF16), wo_ref[0:512, :], preferred_element_type=F32) \
        + jnp.dot(og_ref[...].astype(BF16), wo_ref[512:1024, :], preferred_element_type=F32)
    return _layer_norm(DN_ALPHA * x_ref[...] + attn, lng_ref[0:1, :], lnb_ref[0:1, :])


def _post_prompt_body(onsa_ref, og_ref, x_ref, p_ref, wo_ref, lng_ref, lnb_ref, wu_ref, cw_ref, wd_ref,
                      wp_ref, wg_ref, y_ref, fbuf_ref, upbuf):
    t = pl.program_id(1)
    tm = x_ref.shape[0]
    pad = SUBLANES

    @pl.when(t == 0)
    def _():
        upbuf[0:pad, :] = jnp.zeros((pad, D_FF), F32)

    x1 = _post_head(onsa_ref, og_ref, x_ref, wo_ref, lng_ref, lnb_ref)
    x1b = x1.astype(BF16)
    ffn = jnp.zeros((tm, D_MODEL), F32)
    for cc in range(D_FF // FF_CHUNK):
        cs = slice(cc * FF_CHUNK, (cc + 1) * FF_CHUNK)
        up_g = jnp.dot(x1b, wu_ref[:, cs], preferred_element_type=F32)
        up_v = jnp.dot(x1b, wu_ref[:, D_FF + cc * FF_CHUNK:D_FF + (cc + 1) * FF_CHUNK], preferred_element_type=F32)
        upbuf[pad:pad + tm, cs] = up_g
        hg = cw_ref[FFN_CONV - 1:FFN_CONV, cs] * up_g
        for tap in range(FFN_CONV - 1):
            o0 = pad - (FFN_CONV - 1) + tap
            hg = hg + cw_ref[tap:tap + 1, cs] * upbuf[o0:o0 + tm, cs]
        ffn = ffn + jnp.dot((_gelu(hg) * up_v).astype(BF16), wd_ref[cs, :], preferred_element_type=F32)
    fbuf_ref[...] = upbuf[pad + tm - (FFN_CONV - 1):pad + tm, :]
    upbuf[0:pad, :] = upbuf[tm:tm + pad, :]
    y_ref[...] = _post_tail(x1, ffn, p_ref[...], lng_ref, lnb_ref, wg_ref, wp_ref)


def _post_decode_body(onsa_ref, og_ref, x_ref, p_ref, fst_ref, wo_ref, lng_ref, lnb_ref, wu_ref, cw_ref, wd_ref,
                      wp_ref, wg_ref, y_ref, fnew_ref):
    tm = x_ref.shape[0]
    x1 = _post_head(onsa_ref, og_ref, x_ref, wo_ref, lng_ref, lnb_ref)
    x1b = x1.astype(BF16)
    ffn = jnp.zeros((tm, D_MODEL), F32)
    for cc in range(D_FF // FF_CHUNK):
        cs = slice(cc * FF_CHUNK, (cc + 1) * FF_CHUNK)
        cs1 = slice(D_FF + cc * FF_CHUNK, D_FF + (cc + 1) * FF_CHUNK)
        up_g = jnp.dot(x1b, wu_ref[:, cs], preferred_element_type=F32)
        up_v = jnp.dot(x1b, wu_ref[:, cs1], preferred_element_type=F32)
        hg = cw_ref[0:1, cs] * fst_ref[:, cs] + cw_ref[1:2, cs] * fst_ref[:, cs1] + cw_ref[2:3, cs] * up_g
        fnew_ref[:, cs] = fst_ref[:, cs1]
        fnew_ref[:, cs1] = up_g
        ffn = ffn + jnp.dot((_gelu(hg) * up_v).astype(BF16), wd_ref[cs, :], preferred_element_type=F32)
    y_ref[...] = _post_tail(x1, ffn, p_ref[...], lng_ref, lnb_ref, wg_ref, wp_ref)


def _post_weight_specs():
    return [_resident((D_MODEL, D_MODEL)), _resident((3, D_MODEL)), _resident((3, D_MODEL)),
            _resident((D_MODEL, 2 * D_FF)), _resident((FFN_CONV, D_FF)), _resident((D_FF, D_MODEL)),
            _resident((PLE_DIM, D_MODEL)), _resident((D_MODEL, D_MODEL))]


_POST_WEIGHT_BYTES = 2 * (2 * D_MODEL * D_MODEL + 3 * D_MODEL * D_FF + PLE_DIM * D_MODEL) + 4 * 9 * D_FF


def _post_prompt(onsa3, og3, x3, p3, p_base, wts, tm):
    b, t, _ = x3.shape
    row = lambda n: pl.BlockSpec((None, tm, n), lambda i, j: (i, j, 0))
    p_spec = pl.BlockSpec((None, tm, PLE_DIM), lambda i, j: (i + p_base, j, 0))
    est = _POST_WEIGHT_BYTES + 2 * tm * (512 + 512 + 1024 + 256 + 1024) * 4 + (tm + 8) * D_FF * 4 + 12 * tm * 1024 * 4
    return pl.pallas_call(
        _post_prompt_body,
        grid=(b, t // tm),
        in_specs=[row(512), row(512), row(D_MODEL), p_spec] + _post_weight_specs(),
        out_specs=[row(D_MODEL), pl.BlockSpec((None, FFN_CONV - 1, D_FF), lambda i, j: (i, 0, 0))],
        out_shape=[jax.ShapeDtypeStruct((b, t, D_MODEL), F32),
                   jax.ShapeDtypeStruct((b, FFN_CONV - 1, D_FF), F32)],
        scratch_shapes=[pltpu.VMEM((tm + SUBLANES, D_FF), F32)],
        compiler_params=_params(("parallel", "arbitrary"), est),
        name="post_prompt",
    )(onsa3, og3, x3, p3, *wts)


def _post_decode(onsa2, og2, x2, p2, fst2, wts):
    m = x2.shape[0]
    full = lambda n: pl.BlockSpec((m, n), lambda i: (0, 0))
    est = _POST_WEIGHT_BYTES + 2 * m * (512 + 512 + 1024 + 256 + 1024 + 4 * D_FF) * 4 + 12 * m * 1024 * 4
    return pl.pallas_call(
        _post_decode_body,
        grid=(1,),
        in_specs=[full(512), full(512), full(D_MODEL), full(PLE_DIM), full(2 * D_FF)] + _post_weight_specs(),
        out_specs=[full(D_MODEL), full(2 * D_FF)],
        out_shape=[jax.ShapeDtypeStruct((m, D_MODEL), F32), jax.ShapeDtypeStruct((m, 2 * D_FF), F32)],
        compiler_params=_params(("arbitrary",), est),
        name="post_decode",
    )(onsa2, og2, x2, p2, fst2, *wts)


DEC_AUX0 = 96


def _decode_aux(n_pages, page):
    shape = (n_pages, LANES, page)
    pos = (lax.broadcasted_iota(jnp.int32, shape, 0) * page + lax.broadcasted_iota(jnp.int32, shape, 2)).astype(F32)
    row = lax.broadcasted_iota(jnp.int32, shape, 1)
    hi = jnp.floor(pos * (1.0 / SLC_BLOCK))
    aux = jnp.where(row == DEC_AUX0, hi, jnp.where(row == DEC_AUX0 + 1, pos - hi * SLC_BLOCK,
                    jnp.where((row == DEC_AUX0 + 2) | (row == DEC_AUX0 + 3), 1.0, 0.0)))
    assert 2 * (n_pages * page // SLC_BLOCK) <= DEC_AUX0
    aux = jnp.where((row < DEC_AUX0) & (row % 2 == 0) & (hi == (row // 2).astype(F32)), NEG, aux)
    return aux.astype(BF16)


def _nsa_decode_body(pt_ref, *refs, n_pages, page, chained, seqs):
    del pt_ref
    n_in = seqs * n_pages
    page_refs = refs[:n_in]
    (q_ref, kvn_ref, winn_ref, sm_ref, wbuf_ref, pe_ref, phi_ref, daux_ref) = refs[n_in:n_in + 8]
    o_ref, wout_ref = refs[n_in + 8 + (1 if chained else 0):]
    chains = [_nsa_decode_seq(page_refs[s * n_pages:(s + 1) * n_pages], q_ref.at[s], kvn_ref.at[s], winn_ref.at[s],
                              sm_ref.at[s], wbuf_ref.at[s], pe_ref, phi_ref, daux_ref, o_ref.at[s], wout_ref.at[s], page)
              for s in range(seqs)]
    for _ in zip(*chains):
        pass


def _nsa_decode_seq(page_refs, q_ref, kvn_ref, winn_ref, sm_ref, wbuf_ref, pe_ref, phi_ref, daux_ref, o_ref, wout_ref,
                    page):
    n_pages = len(page_refs)
    past = n_pages * page
    ncv = past // CMP_BLOCK
    nsb = past // SLC_BLOCK
    assert ncv <= LANES and ncv % 2 == 0
    q = q_ref[...] * (HEAD_DIM ** -0.5)
    zero = jnp.zeros((1, HEAD_DIM), F32)
    rows = []
    for h in range(N_HEADS):
        piece = q[:, h * HEAD_DIM:(h + 1) * HEAD_DIM]
        rows.append(jnp.concatenate([piece, zero] if h < GROUP else [zero, piece], axis=1))
    qbd = jnp.concatenate(rows, axis=0)
    qbd_b = qbd.astype(BF16)
    hrow = lax.broadcasted_iota(jnp.int32, (N_HEADS, 1), 0)
    slope = jnp.concatenate([jnp.full((1, 1), 2.0 ** -(h + 1), F32) for h in range(N_HEADS)], axis=0)

    zero_hh = jnp.zeros((HEAD_DIM, HEAD_DIM), F32)
    both_heads = lambda w: jnp.concatenate([jnp.concatenate([w, zero_hh], axis=1),
                                            jnp.concatenate([zero_hh, w], axis=1)], axis=0)
    pe_row = lambda which: jnp.concatenate([jnp.mean(pe_ref[which], axis=0, keepdims=True)] * N_KV, axis=1)
    pool = jnp.where(lax.broadcasted_iota(jnp.int32, (past, ncv), 0) // CMP_BLOCK
                     == lax.broadcasted_iota(jnp.int32, (past, ncv), 1), 1.0, 0.0)
    pool_t = jnp.where(lax.broadcasted_iota(jnp.int32, (ncv, past), 1) // CMP_BLOCK
                       == lax.broadcasted_iota(jnp.int32, (ncv, past), 0), 1.0, 0.0).astype(BF16)
    q_phi = lax.dot_general(qbd, both_heads(phi_ref[0]), _NT, precision=lax.Precision.HIGHEST,
                            preferred_element_type=F32)
    yield
    row_s = jnp.concatenate([_dot(q_phi, pr[0:KV_W, :]) for pr in page_refs], axis=1)
    yield
    cidx = lax.broadcasted_iota(jnp.int32, (1, ncv), 1)
    c_mid = cidx.astype(F32) * CMP_BLOCK + (CMP_BLOCK - 1) / 2
    s = _dot_sel(row_s, pool) * (1.0 / CMP_BLOCK) + jnp.sum(q_phi * pe_row(0), axis=-1, keepdims=True) \
        - slope * (float(past) - c_mid)
    p = _softmax_rows(s)
    yield
    p_rows = jnp.dot(p.astype(BF16), pool_t, preferred_element_type=F32) * (1.0 / CMP_BLOCK)
    u = jnp.sum(p, axis=-1, keepdims=True) * pe_row(1)
    for i, pr in enumerate(page_refs):
        u = u + _dot_nt(p_rows[:, i * page:(i + 1) * page], pr[KV_W:2 * KV_W, :])
    yield
    o_cmp = jnp.dot(u, both_heads(phi_ref[1]), precision=lax.Precision.HIGHEST, preferred_element_type=F32)

    p_l = p if ncv == LANES else jnp.concatenate([p, jnp.zeros((N_HEADS, LANES - ncv), F32)], axis=1)
    pair = p_l + pltpu.roll(p_l, LANES - 1, 1)
    pooled = jnp.concatenate([jnp.sum(pair[n * GROUP:(n + 1) * GROUP], axis=0, keepdims=True) for n in range(N_KV)],
                             axis=0)
    lane = lax.broadcasted_iota(jnp.int32, (N_KV, LANES), 1)
    blk = lane // 2
    is_blk = ((lane % 2) == 0) & (lane < ncv)
    is_cur = lane == ncv
    forced = is_cur | (is_blk & ((blk == 0) | ((nsb - blk) < N_LOCAL)))
    score = jnp.where(forced, BIG, jnp.where(is_blk, pooled, -BIG))
    rank = jnp.zeros((N_KV, LANES), F32)
    cand = [2 * b for b in range(nsb)] + [ncv]
    for lc in cand:
        col = score[:, lc:lc + 1]
        rank = rank + jnp.where(lane > lc, jnp.where(col >= score, 1.0, 0.0), jnp.where(col > score, 1.0, 0.0))
    sel = jnp.where((rank < float(min(TOP_N, nsb + 1))) & (is_blk | is_cur), 1.0, 0.0)
    unsel = jnp.where(is_blk, 1.0 - sel, 0.0)
    unsel8 = jnp.concatenate([jnp.broadcast_to(unsel[n:n + 1], (GROUP, LANES)) for n in range(N_KV)], axis=0)

    lane8 = lax.broadcasted_iota(jnp.int32, (N_HEADS, LANES), 1)
    q_side = unsel8 + jnp.where(lane8 == DEC_AUX0, slope * SLC_BLOCK,
                                jnp.where(lane8 == DEC_AUX0 + 1, slope,
                                          jnp.where(lane8 == DEC_AUX0 + 2, -slope * float(past - past % SLC_BLOCK),
                                                    jnp.where(lane8 == DEC_AUX0 + 3, -slope * float(past % SLC_BLOCK), 0.0))))
    lhs_cat = jnp.concatenate([qbd, q_side], axis=1).astype(BF16)
    kvn = kvn_ref[...]
    s_new = jnp.sum(qbd * kvn[:, 2 * KV_W:3 * KV_W], axis=-1, keepdims=True)
    s_parts = []
    for i, pr in enumerate(page_refs):
        rhs_cat = jnp.concatenate([pr[2 * KV_W:3 * KV_W, :].astype(BF16), daux_ref[i]], axis=0)
        s_parts.append(jnp.dot(lhs_cat, rhs_cat, preferred_element_type=F32))
    yield
    s_all = jnp.concatenate(s_parts, axis=1)
    m = jnp.maximum(jnp.max(s_all, axis=-1, keepdims=True), s_new)
    e = jnp.exp(s_all - m)
    e_new = jnp.exp(s_new - m)
    acc = e_new * kvn[:, 3 * KV_W:4 * KV_W]
    for i, pr in enumerate(page_refs):
        acc = acc + _dot_nt(e[:, i * page:(i + 1) * page], pr[3 * KV_W:4 * KV_W, :])
    yield
    o_slc = acc / (jnp.sum(e, axis=-1, keepdims=True) + e_new)

    wb = wbuf_ref.shape[1]
    winn = winn_ref[...]
    rpos = lax.broadcasted_iota(jnp.int32, (1, wb), 1)
    dist = wb - rpos
    sw = _dot(qbd_b, wbuf_ref[0:KV_W, :]) - slope * dist.astype(F32)
    sw = jnp.where(dist < WINDOW, sw, NEG)
    sw_new = jnp.sum(qbd * winn[:, 0:KV_W], axis=-1, keepdims=True)
    mw = jnp.maximum(jnp.max(sw, axis=-1, keepdims=True), sw_new)
    ew = jnp.exp(sw - mw)
    ew_new = jnp.exp(sw_new - mw)
    o_win = (_dot_nt(ew, wbuf_ref[KV_W:2 * KV_W, :]) + ew_new * winn[:, KV_W:2 * KV_W]) \
        / (jnp.sum(ew, axis=-1, keepdims=True) + ew_new)
    winn_col = jnp.broadcast_to(winn, (SUBLANES, 2 * KV_W)).T[:, 0:1]
    shifted = pltpu.roll(wbuf_ref[...], wb - 1, 1)
    wout_ref[...] = jnp.where(lax.broadcasted_iota(jnp.int32, (2 * KV_W, wb), 1) == wb - 1, winn_col, shifted)

    gates = _sigmoid(sm_ref[:, 0:3 * N_HEADS])
    glane = lax.broadcasted_iota(jnp.int32, (N_HEADS, 3 * N_HEADS), 1)
    gsel = lambda r: jnp.sum(jnp.where(glane == 3 * hrow + r, gates, 0.0), axis=-1, keepdims=True)
    o8 = gsel(0) * o_cmp + gsel(1) * o_slc + gsel(2) * o_win
    pieces = []
    for h in range(N_HEADS):
        n = h // GROUP
        pieces.append(o8[h:h + 1, n * HEAD_DIM:(n + 1) * HEAD_DIM])
    o_ref[...] = jnp.concatenate(pieces, axis=1)
    yield


def _nsa_decode(page_table, cache3, page_base, q3, kvn3, winn3, sm3, wbuf3, seq_base, pe, phi, wout_prev):
    nb, n_pages = page_table.shape
    page = cache3.shape[2]
    wb = wbuf3.shape[2]
    assert wb == WINDOW and n_pages * page >= wb
    chained = wout_prev is not None
    seqs = DEC_SEQS if nb % DEC_SEQS == 0 else 1
    assert seq_base % seqs == 0
    page_specs = [pl.BlockSpec((None, 4 * KV_W, page),
                               functools.partial(lambda i, pt, s, k: (pt[i * seqs + s, k] + page_base, 0, 0), s=s, k=k))
                  for s in range(seqs) for k in range(n_pages)]
    one = lambda n: pl.BlockSpec((seqs, 1, n), lambda i, pt: (i, 0, 0))
    win_spec = pl.BlockSpec((seqs, 2 * KV_W, wb), lambda i, pt: (i + seq_base // seqs, 0, 0))
    in_specs = page_specs + [one(512), one(512), one(256), one(SM_W), win_spec,
                             pl.BlockSpec((2, CMP_BLOCK, HEAD_DIM), lambda i, pt: (0, 0, 0)),
                             pl.BlockSpec((2, HEAD_DIM, HEAD_DIM), lambda i, pt: (0, 0, 0)),
                             pl.BlockSpec((n_pages, LANES, page), lambda i, pt: (0, 0, 0))]
    args = [page_table] + [cache3] * (seqs * n_pages) + [q3, kvn3, winn3, sm3, wbuf3, pe, phi,
                                                          _decode_aux(n_pages, page)]
    aliases = {}
    if chained:
        in_specs.append(pl.BlockSpec(memory_space=pl.ANY))
        args.append(wout_prev)
        aliases = {len(args) - 1: 1}
    grid_spec = pltpu.PrefetchScalarGridSpec(
        num_scalar_prefetch=1, grid=(nb // seqs,), in_specs=in_specs, out_specs=[one(512), win_spec])
    est = seqs * (2 * n_pages * page * 512 * 4 + 4 * wb * 256 * 4) + 8 * 1024 * 1024
    return pl.pallas_call(
        functools.partial(_nsa_decode_body, n_pages=n_pages, page=page, chained=chained, seqs=seqs),
        grid_spec=grid_spec,
        out_shape=[jax.ShapeDtypeStruct((nb, 1, 512), F32), jax.ShapeDtypeStruct(wbuf3.shape, F32)],
        input_output_aliases=aliases,
        compiler_params=_params(("arbitrary",), est),
        name="nsa_decode",
    )(*args)


def _gdn_dec_prep_body(gqkv_ref, cst_ref, sm_ref, cw_ref, alog_ref, dtb_ref, qn_ref, kn_ref, v_ref, db_ref, cnew_ref):
    w = GDN_QKV
    xn = gqkv_ref[...]
    y = cw_ref[GDN_CONV - 1:GDN_CONV, :] * xn
    for tap in range(GDN_CONV - 1):
        y = y + cw_ref[tap:tap + 1, :] * cst_ref[:, tap * w:(tap + 1) * w]
    for tap in range(GDN_CONV - 2):
        cnew_ref[:, tap * w:(tap + 1) * w] = cst_ref[:, (tap + 1) * w:(tap + 2) * w]
    cnew_ref[:, (GDN_CONV - 2) * w:(GDN_CONV - 1) * w] = xn
    act = _silu(y)
    ind = _head_indicator(GDN_W, N_GDN)
    ind_t = _head_indicator_t(N_GDN, GDN_W)
    qraw = act[:, 0:GDN_W]
    kraw = act[:, GDN_W:2 * GDN_W]
    qn_ref[...] = (qraw * _l2_scale(qraw, ind, ind_t) * (HEAD_DIM ** -0.5)).T
    kn_ref[...] = (kraw * _l2_scale(kraw, ind, ind_t)).T
    v_ref[...] = act[:, 2 * GDN_W:3 * GDN_W].T
    sm = sm_ref[...]
    g = -jnp.exp(alog_ref[...]) * _softplus(sm[:, SM_GA:SM_GA + N_GDN] + dtb_ref[...])
    beta = _sigmoid(sm[:, SM_GB:SM_GB + N_GDN])
    m = sm.shape[0]
    db_ref[...] = jnp.concatenate([jnp.exp(g), beta, jnp.zeros((m, SM_W - 2 * N_GDN), F32)], axis=1).T


def _gdn_dec_prep(gqkv2, cst2, sm2, cw, alog, dtb):
    m = gqkv2.shape[0]
    full = lambda n: pl.BlockSpec((m, n), lambda i: (0, 0))
    full_t = lambda n: pl.BlockSpec((n, m), lambda i: (0, 0))
    est = 2 * m * (8 * GDN_QKV + 4 * 512) * 4 + 8 * m * GDN_QKV * 4
    return pl.pallas_call(
        _gdn_dec_prep_body,
        grid=(1,),
        in_specs=[full(GDN_QKV), full((GDN_CONV - 1) * GDN_QKV), full(SM_W),
                  pl.BlockSpec((GDN_CONV, GDN_QKV), lambda i: (0, 0)),
                  pl.BlockSpec((1, N_GDN), lambda i: (0, 0)), pl.BlockSpec((1, N_GDN), lambda i: (0, 0))],
        out_specs=[full_t(GDN_W), full_t(GDN_W), full_t(GDN_W), full_t(SM_W), full((GDN_CONV - 1) * GDN_QKV)],
        out_shape=[jax.ShapeDtypeStruct((GDN_W, m), F32)] * 3 + [jax.ShapeDtypeStruct((SM_W, m), F32),
                   jax.ShapeDtypeStruct((m, (GDN_CONV - 1) * GDN_QKV), F32)],
        compiler_params=_params(("arbitrary",), est),
        name="gdn_decode_prep",
    )(gqkv2, cst2, sm2, cw, alog.reshape(1, N_GDN), dtb.reshape(1, N_GDN))


def _gdn_dec_step_body(qn_ref, kn_ref, v_ref, db_ref, gz_ref, s_ref, nw_ref, *out_refs):
    o_ref, snew_ref = out_refs[-2:]
    h = pl.program_id(0)
    a = db_ref[pl.ds(h, 1), :]
    bta = db_ref[pl.ds(N_GDN + h, 1), :]
    sk = jnp.zeros(v_ref.shape, F32)
    for d in range(HEAD_DIM):
        sk = sk + kn_ref[d:d + 1, :] * s_ref[d]
    vn = bta * (v_ref[...] - a * sk)
    o = jnp.zeros(v_ref.shape, F32)
    for d in range(HEAD_DIM):
        s_d = a * s_ref[d] + kn_ref[d:d + 1, :] * vn
        snew_ref[d] = s_d
        o = o + qn_ref[d:d + 1, :] * s_d
    o = o * lax.rsqrt(jnp.mean(o * o, axis=0, keepdims=True) + RMS_EPS) * nw_ref[...]
    o_ref[...] = o * _silu(gz_ref[...])


def _gdn_dec_step(qn_t, kn_t, v_t, db_t, gz_t, s4, layer_base, nw_col, s_prev):
    nb = qn_t.shape[1]
    head = lambda: pl.BlockSpec((HEAD_DIM, nb), lambda i: (i, 0))
    st = pl.BlockSpec((None, HEAD_DIM, HEAD_DIM, nb), lambda i: (i + layer_base, 0, 0, 0))
    in_specs = [head(), head(), head(), pl.BlockSpec((SM_W, nb), lambda i: (0, 0)), head(), st,
                pl.BlockSpec((HEAD_DIM, 1), lambda i: (0, 0))]
    args = [qn_t, kn_t, v_t, db_t, gz_t, s4, nw_col]
    aliases = {}
    if s_prev is not None:
        in_specs.append(pl.BlockSpec(memory_space=pl.ANY))
        args.append(s_prev)
        aliases = {len(args) - 1: 1}
    est = 4 * HEAD_DIM * HEAD_DIM * nb * 4 + 4 * 1024 * 1024
    return pl.pallas_call(
        _gdn_dec_step_body,
        grid=(N_GDN,),
        in_specs=in_specs,
        out_specs=[head(), st],
        out_shape=[jax.ShapeDtypeStruct((GDN_W, nb), F32), jax.ShapeDtypeStruct(s4.shape, F32)],
        input_output_aliases=aliases,
        compiler_params=_params(("parallel",), est),
        name="gdn_decode_step",
    )(*args)


def _prep_w_in(w):
    small = jnp.concatenate([w[:, C_GATE:C_GQKV], w[:, C_GA:C_GB], w[:, C_GB:C_GZ],
                             jnp.zeros((D_MODEL, SM_W - (C_GQKV - C_GATE) - 2 * N_GDN), w.dtype)], axis=1)
    return jnp.concatenate([w[:, 0:C_KV], w[:, C_KV:C_WIN], w[:, C_WIN:C_GATE], w[:, C_GQKV:C_GA],
                            w[:, C_GZ:IN_WIDTH], small], axis=1).astype(BF16)


def kernel(x_prompt, x_sample, cache_nsa_kv, state_nsa_win, state_gdn, state_gdn_conv, state_ffn_conv, page_table, p_prompt, p_sample, w_in, nsa_pe, nsa_phi, gdn_conv_w, gdn_A_log, gdn_dt_bias, gdn_norm_w, w_out, ln_g, ln_b, ffn_w_up, ffn_conv_w, ffn_w_down, ple_w_proj, ple_w_gate):
    b, t, d = x_prompt.shape
    nb, ts, _ = x_sample.shape
    assert d == D_MODEL and ts == 1 and t % Q_BLOCK == 0 and t >= WINDOW
    depth = w_in.shape[0]
    n_pool, page = cache_nsa_kv.shape[1], cache_nsa_kv.shape[2]
    wb = state_nsa_win.shape[2]
    tm_p = 512
    xp, xs = x_prompt, x_sample.reshape(nb, d)
    cache_all = cache_nsa_kv.transpose(0, 1, 3, 4, 5, 2).reshape(depth * n_pool, 4 * KV_W, page)
    wbuf_all = state_nsa_win.transpose(0, 1, 3, 4, 5, 2).reshape(depth * nb, 2 * KV_W, wb)
    gstate_all = state_gdn.transpose(0, 2, 3, 4, 1).reshape(depth * N_GDN, HEAD_DIM, HEAD_DIM, nb)
    p_all = p_prompt.reshape(depth * b, t, PLE_DIM)
    win_all = None
    gs_all = None
    kvt_all = None
    st_p, st_s = [], []
    for l in range(depth):
        w_cat = _prep_w_in(w_in[l])
        post_w = (w_out[l].astype(BF16), ln_g[l], ln_b[l], ffn_w_up[l].astype(BF16), ffn_conv_w[l],
                  ffn_w_down[l].astype(BF16), ple_w_proj[l].astype(BF16), ple_w_gate[l].astype(BF16))
        nw_t = jnp.tile(gdn_norm_w[l].reshape(1, HEAD_DIM), (1, N_GDN))

        q, kv, win, gqkv, gz, sm, kvt_all = _proj_in(xp.reshape(b * t, d), w_cat, 512, (kvt_all, l * b, t, depth * b))
        kv3, win3 = kv.reshape(b, t, 512), win.reshape(b, t, 256)
        sm3, gqkv3 = sm.reshape(b, t, SM_W), gqkv.reshape(b, t, GDN_QKV)
        kc, vc = _compress(kv3, nsa_pe[l], nsa_phi[l])
        o_nsa = _nsa_prompt(q.reshape(b, t, 512), kv3, kc, vc, win3, sm3)
        o_g, s_p = _gdn_prompt(gqkv3, gz.reshape(b, t, GDN_W), sm3, gdn_conv_w[l], gdn_A_log[l], gdn_dt_bias[l], nw_t)
        xp, fbuf_p = _post_prompt(o_nsa, o_g, xp, p_all, l * b, post_w, tm_p)
        st_p.append((None, win3[:, t - wb:].reshape(b, wb, 2, N_KV, HEAD_DIM),
                     s_p, gqkv3[:, t - (GDN_CONV - 1):], fbuf_p))

        q, kv, win, gqkv, gz, sm = _proj_in(xs, w_cat, nb)
        o_nsa, win_all = _nsa_decode(
            page_table, cache_all, l * n_pool, q.reshape(nb, 1, 512), kv.reshape(nb, 1, 512),
            win.reshape(nb, 1, 256), sm.reshape(nb, 1, SM_W), wbuf_all, l * nb, nsa_pe[l], nsa_phi[l], win_all)
        qn_t, kn_t, v_t, db_t, cnew = _gdn_dec_prep(gqkv, state_gdn_conv[l].reshape(nb, (GDN_CONV - 1) * GDN_QKV), sm,
                                                    gdn_conv_w[l], gdn_A_log[l], gdn_dt_bias[l])
        o_g_t, gs_all = _gdn_dec_step(qn_t, kn_t, v_t, db_t, gz.T, gstate_all, l * N_GDN,
                                      gdn_norm_w[l].reshape(HEAD_DIM, 1), gs_all)
        xs, fnew = _post_decode(o_nsa.reshape(nb, 512), o_g_t.T, xs, p_sample[l].reshape(nb, PLE_DIM),
                                state_ffn_conv[l].reshape(nb, (FFN_CONV - 1) * D_FF), post_w)
        st_s.append((kv.reshape(nb, 1, 4, N_KV, HEAD_DIM), None, None, cnew.reshape(nb, GDN_CONV - 1, GDN_QKV),
                     fnew.reshape(nb, FFN_CONV - 1, D_FF)))

    sample_states = {
        1: win_all.reshape(depth, nb, 2, N_KV, HEAD_DIM, wb).transpose(0, 1, 5, 2, 3, 4),
        2: gs_all.reshape(depth, N_GDN, HEAD_DIM, HEAD_DIM, nb).transpose(0, 4, 1, 2, 3),
    }
    kv_rows_prompt = kvt_all.reshape(depth, b, 4, N_KV, HEAD_DIM, t).transpose(0, 1, 5, 2, 3, 4)
    outs = [xp, xs.reshape(nb, 1, d)]
    for k in range(5):
        outs.append(kv_rows_prompt if k == 0 else jnp.stack([s[k] for s in st_p]))
        outs.append(sample_states[k] if k in sample_states else jnp.stack([s[k] for s in st_s]))
    return tuple(outs)
```

```python
import functools

import jax
import jax.numpy as jnp
from jax import lax
from jax.experimental import pallas as pl
from jax.experimental.pallas import tpu as pltpu

F32 = jnp.float32
BF16 = jnp.bfloat16

D_MODEL = 1024
DEPTH = 2
HEAD_DIM = 64
N_HEADS = 8
N_KV = 2
GROUP = 4
N_GDN = 8
KV_W = N_KV * HEAD_DIM
GDN_W = N_GDN * HEAD_DIM
GDN_QKV = 3 * GDN_W
CMP_BLOCK = 32
SLC_BLOCK = 64
TOP_N = 16
N_LOCAL = 2
WINDOW = 512
Q_BLOCK = 128
GDN_CHUNK = 64
GDN_CONV = 4
FFN_CONV = 3
D_FF = 2816
PLE_DIM = 256
LN_EPS = 1e-5
RMS_EPS = 1e-6
DN_ALPHA = (2 * DEPTH) ** 0.25
NEG = -1e30
BIG = 1e30

C_KV = 512
C_WIN = 1024
C_GATE = 1280
C_GQKV = 1304
C_GA = 2840
C_GB = 2848
C_GZ = 2856
IN_WIDTH = 3368
SM_W = 128
SM_GA = 24
SM_GB = 32

LANES = 128
SUBLANES = 8
VMEM_CAP = 56 * 1024 * 1024
FF_CHUNK = 2816
SLC_TILE = 512
DEC_SEQS = 4

_NT = (((1,), (1,)), ((), ()))
_TN = (((0,), (0,)), ((), ()))


def _dot(a, b):
    return jnp.dot(a.astype(BF16), b.astype(BF16), preferred_element_type=F32)


def _dot_nt(a, b):
    return lax.dot_general(a.astype(BF16), b.astype(BF16), _NT, preferred_element_type=F32)


def _dot_tn(a, b):
    return lax.dot_general(a.astype(BF16), b.astype(BF16), _TN, preferred_element_type=F32)


def _dot_f32(a, b):
    return jnp.dot(a, b, precision=lax.Precision.HIGHEST, preferred_element_type=F32)


def _split3(a):
    hi = a.astype(BF16)
    r = a - hi.astype(F32)
    mid = r.astype(BF16)
    lo = (r - mid.astype(F32)).astype(BF16)
    return hi, mid, lo


def _dot_sel(a, sel):
    hi, mid, lo = _split3(a)
    sel = sel.astype(BF16)
    d = lambda u: jnp.dot(u, sel, preferred_element_type=F32)
    return d(hi) + d(mid) + d(lo)


def _dot_tn_sel(sel, a):
    hi, mid, lo = _split3(a)
    sel = sel.astype(BF16)
    d = lambda u: lax.dot_general(sel, u, _TN, preferred_element_type=F32)
    return d(hi) + d(mid) + d(lo)


def _sigmoid(x):
    return 1.0 / (1.0 + jnp.exp(-x))


def _silu(x):
    return x * _sigmoid(x)


def _softplus(x):
    return jnp.maximum(x, 0.0) + jnp.log1p(jnp.exp(-jnp.abs(x)))


def _layer_norm(x, g, b):
    mu = jnp.mean(x, axis=-1, keepdims=True)
    xc = x - mu
    var = jnp.mean(xc * xc, axis=-1, keepdims=True)
    return xc * lax.rsqrt(var + LN_EPS) * g + b


def _gelu(x):
    return 0.5 * x * (1.0 + lax.erf(x * (0.5 ** 0.5)))


def _head_indicator(width, heads):
    r = lax.broadcasted_iota(jnp.int32, (width, heads), 0) // HEAD_DIM
    c = lax.broadcasted_iota(jnp.int32, (width, heads), 1)
    return jnp.where(r == c, 1.0, 0.0).astype(BF16)


def _head_indicator_t(heads, width):
    r = lax.broadcasted_iota(jnp.int32, (heads, width), 0)
    c = lax.broadcasted_iota(jnp.int32, (heads, width), 1) // HEAD_DIM
    return jnp.where(r == c, 1.0, 0.0).astype(BF16)


def _params(sem, est_bytes):
    limit = int(min(max(2 * est_bytes, 32 * 1024 * 1024), VMEM_CAP))
    return pltpu.CompilerParams(dimension_semantics=sem, vmem_limit_bytes=limit)


def _resident(shape):
    nd = len(shape)
    return pl.BlockSpec(shape, lambda *_: (0,) * nd, pipeline_mode=pl.Buffered(1))


_PROJ_WIDTHS = (512, 512, 256, GDN_QKV, GDN_W, SM_W)


def _proj_in_body(x_ref, w_ref, *refs, n_extra_in, emit_kv_t):
    out_refs = refs[n_extra_in:]
    xb = x_ref[...].astype(BF16)
    off = 0
    for ref, n in zip(out_refs, _PROJ_WIDTHS):
        ref[...] = jnp.dot(xb, w_ref[:, off:off + n], preferred_element_type=F32)
        off += n
    if emit_kv_t:
        out_refs[len(_PROJ_WIDTHS)][...] = out_refs[1][...].T


def _proj_in(x2, w_cat, tm, kv_t=None):
    m = x2.shape[0]
    tot = sum(_PROJ_WIDTHS)
    est = 2 * tm * D_MODEL * 4 + D_MODEL * tot * 2 + 4 * tm * tot * 4
    in_specs = [pl.BlockSpec((tm, D_MODEL), lambda i: (i, 0)), _resident((D_MODEL, tot))]
    out_specs = [pl.BlockSpec((tm, n), lambda i: (i, 0)) for n in _PROJ_WIDTHS]
    out_shape = [jax.ShapeDtypeStruct((m, n), F32) for n in _PROJ_WIDTHS]
    args = [x2, w_cat]
    aliases = {}
    if kv_t is not None:
        prev, slab_base, rows, n_slabs = kv_t
        per_seq = rows // tm
        out_specs.append(pl.BlockSpec((None, 512, tm), lambda i: (slab_base + i // per_seq, 0, i % per_seq)))
        out_shape.append(jax.ShapeDtypeStruct((n_slabs, 512, rows), F32))
        if prev is not None:
            in_specs.append(pl.BlockSpec(memory_space=pl.ANY))
            args.append(prev)
            aliases = {2: len(_PROJ_WIDTHS)}
    return pl.pallas_call(
        functools.partial(_proj_in_body, n_extra_in=len(args) - 2, emit_kv_t=kv_t is not None),
        grid=(m // tm,),
        in_specs=in_specs,
        out_specs=out_specs,
        out_shape=out_shape,
        input_output_aliases=aliases,
        compiler_params=_params(("parallel",), est),
        name="proj_in",
    )(*args)


def _compress_rows(rows, pe, phi):
    nblk = rows.shape[0] // CMP_BLOCK
    mean = rows.reshape(nblk, CMP_BLOCK, KV_W).sum(axis=1) * (1.0 / CMP_BLOCK)
    pem = jnp.mean(pe, axis=0, keepdims=True)
    outs = []
    for n in range(N_KV):
        outs.append(_dot_f32(mean[:, n * HEAD_DIM:(n + 1) * HEAD_DIM] + pem, phi))
    return jnp.concatenate(outs, axis=1)


def _compress_body(kv_ref, pe_ref, phi_ref, kc_ref, vc_ref):
    kc_ref[...] = _compress_rows(kv_ref[:, 0:KV_W], pe_ref[0], phi_ref[0])
    vc_ref[...] = _compress_rows(kv_ref[:, KV_W:2 * KV_W], pe_ref[1], phi_ref[1])


def _compress(kv3, pe, phi):
    b, t, _ = kv3.shape
    nc = t // CMP_BLOCK
    est = 2 * t * 256 * 4 + 4 * nc * KV_W * 4
    return pl.pallas_call(
        _compress_body,
        grid=(b,),
        in_specs=[pl.BlockSpec((None, t, 2 * KV_W), lambda i: (i, 0, 0)),
                  _resident((2, CMP_BLOCK, HEAD_DIM)), _resident((2, HEAD_DIM, HEAD_DIM))],
        out_specs=[pl.BlockSpec((None, nc, KV_W), lambda i: (i, 0, 0))] * 2,
        out_shape=[jax.ShapeDtypeStruct((b, nc, KV_W), F32)] * 2,
        compiler_params=_params(("parallel",), est),
        name="nsa_compress",
    )(kv3, pe, phi)


def _softmax_rows(s):
    m = jnp.max(s, axis=-1, keepdims=True)
    e = jnp.exp(s - m)
    return e / jnp.sum(e, axis=-1, keepdims=True)


def _key_aux(npos, pos_of_row, with_blocks):
    pos = pos_of_row(lax.broadcasted_iota(jnp.int32, (npos, LANES), 0).astype(F32))
    lane = lax.broadcasted_iota(jnp.int32, (npos, LANES), 1)
    hi = jnp.floor(pos * (1.0 / SLC_BLOCK))
    aux = jnp.where(lane == AUX0, hi, jnp.where(lane == AUX0 + 1, pos - hi * SLC_BLOCK,
                    jnp.where((lane == AUX0 + 2) | (lane == AUX0 + 3), 1.0, 0.0)))
    if with_blocks:
        aux = jnp.where(lane < AUX0, jnp.where(hi == lane.astype(F32), NEG, 0.0), aux)
    return aux.astype(BF16)


AUX0 = 64


def _softmax_cols(s):
    m = jnp.max(s, axis=0, keepdims=True)
    e = jnp.exp(s - m)
    return e / jnp.sum(e, axis=0, keepdims=True)


def _nsa_prompt_body(q_ref, slc_ref, kc_ref, vc_ref, win_ref, sm_ref, kaux_ref, caux_ref, o_ref,
                     imp_ref, kcat, wcat, ccat, vts, vtw, vtc, st_buf, *, seq):
    j = pl.program_id(1)
    nc = seq // CMP_BLOCK
    ns = seq // SLC_BLOCK
    tq = Q_BLOCK
    cols4 = GROUP * tq
    assert ns <= AUX0 and seq <= 256 * SLC_BLOCK

    @pl.when(j == 0)
    def _():
        for c0 in range(0, seq, SLC_TILE):
            rs = slice(c0, c0 + SLC_TILE)
            kcat[rs, 0:KV_W] = slc_ref[rs, 0:KV_W].astype(BF16)
            kcat[rs, KV_W:2 * KV_W] = kaux_ref[rs, :]
            wcat[rs, 0:KV_W] = win_ref[rs, 0:KV_W].astype(BF16)
            wcat[rs, KV_W:2 * KV_W] = kaux_ref[rs, :]
            vts[:, rs] = slc_ref[rs, KV_W:2 * KV_W].T.astype(BF16)
            vtw[:, rs] = win_ref[rs, KV_W:2 * KV_W].T.astype(BF16)
        ccat[:, 0:KV_W] = kc_ref[...].astype(BF16)
        ccat[:, KV_W:2 * KV_W] = caux_ref[...]
        vtc[...] = vc_ref[...].T.astype(BF16)

    q_t = (q_ref[...] * (HEAD_DIM ** -0.5)).T
    gates_t = _sigmoid(sm_ref[...].T[0:3 * N_HEADS, :])
    qpos_r = j * tq + lax.broadcasted_iota(jnp.int32, (1, tq), 1)
    qpos4 = jnp.concatenate([qpos_r] * GROUP, axis=1)
    qa = (qpos4 // SLC_BLOCK).astype(F32)
    qb = (qpos4 % SLC_BLOCK).astype(F32)
    arow = lax.broadcasted_iota(jnp.int32, (SUBLANES, cols4), 0)
    t_q = lax.broadcasted_iota(jnp.int32, (tq, cols4), 1) % tq
    t_k = lax.broadcasted_iota(jnp.int32, (tq, cols4), 0)
    zeros_h = jnp.zeros((HEAD_DIM, cols4), F32)
    outs = [None] * N_HEADS
    rhs_slcs, o_cmps, o_wins = [], [], []
    for n in range(N_KV):
        hs = slice(n * HEAD_DIM, (n + 1) * HEAD_DIM)
        qn_t = jnp.concatenate(
            [q_t[(GROUP * n + g) * HEAD_DIM:(GROUP * n + g + 1) * HEAD_DIM, :] for g in range(GROUP)], axis=1)
        slope = jnp.concatenate(
            [jnp.full((1, tq), 2.0 ** -(GROUP * n + g + 1), F32) for g in range(GROUP)], axis=1)
        q_top = jnp.concatenate([qn_t, zeros_h] if n == 0 else [zeros_h, qn_t], axis=0)
        q_pos = jnp.where(arow == 0, slope * SLC_BLOCK,
                          jnp.where(arow == 1, slope,
                                    jnp.where(arow == 2, -slope * SLC_BLOCK * qa,
                                              jnp.where(arow == 3, -slope * qb, 0.0))))
        q_tail = jnp.zeros((2 * KV_W - KV_W - AUX0 - SUBLANES, cols4), F32)

        def rhs(block_rows):
            return jnp.concatenate([q_top, block_rows, q_pos, q_tail], axis=0).astype(BF16)

        rhs_plain = rhs(jnp.zeros((AUX0, cols4), F32))

        c_end = lax.broadcasted_iota(jnp.int32, (nc, 1), 0) * CMP_BLOCK + (CMP_BLOCK - 1)
        c_vis = c_end <= qpos4
        s = jnp.dot(ccat[...], rhs_plain, preferred_element_type=F32)
        p = jnp.where(c_vis, _softmax_cols(jnp.where(c_vis, s, NEG)), 0.0)
        o_cmp = jnp.dot(vtc[hs, :], p.astype(BF16), preferred_element_type=F32)

        imp_ref[...] = p[:, 0:tq] + p[:, tq:2 * tq] + p[:, 2 * tq:3 * tq] + p[:, 3 * tq:4 * tq]
        impb = imp_ref[pl.ds(0, ns, stride=2), :] + imp_ref[pl.ds(1, ns, stride=2), :]
        blk = lax.broadcasted_iota(jnp.int32, (ns, tq), 0)
        cur = qpos_r // SLC_BLOCK
        future = blk > cur
        forced = (blk == 0) | (((cur - blk) < N_LOCAL) & jnp.logical_not(future))
        score = jnp.where(future, -BIG, jnp.where(forced, BIG, impb))
        rank = jnp.zeros((ns, tq), F32)
        for bp in range(ns):
            row = score[bp:bp + 1, :]
            rank = rank + jnp.where(blk > bp, jnp.where(row >= score, 1.0, 0.0), jnp.where(row > score, 1.0, 0.0))
        unsel_t = jnp.where(rank < float(min(TOP_N, ns)), 0.0, 1.0)
        if ns < AUX0:
            unsel_t = jnp.concatenate([unsel_t, jnp.zeros((AUX0 - ns, tq), F32)], axis=0)
        rhs_slcs.append(rhs(jnp.concatenate([unsel_t] * GROUP, axis=1)))

        nwb = WINDOW // tq + 1
        s_parts, starts = [], []
        for i in range(nwb):
            kb = j - (nwb - 1) + i
            r0 = pl.multiple_of(jnp.maximum(kb, 0) * tq, tq)
            sw = jnp.dot(wcat[pl.ds(r0, tq), :], rhs_plain, preferred_element_type=F32)
            if i == 0:
                sw = jnp.where(t_k > t_q, sw, NEG)
            if i == nwb - 1:
                sw = jnp.where(t_k <= t_q, sw, NEG)
            else:
                sw = jnp.where(kb >= 0, sw, NEG)
            s_parts.append(sw)
            starts.append(r0)
        pw = _softmax_cols(jnp.concatenate(s_parts, axis=0)).astype(BF16)
        o_win = jnp.dot(vtw[hs, pl.ds(starts[0], tq)], pw[0:tq], preferred_element_type=F32)
        for i in range(1, nwb):
            o_win = o_win + jnp.dot(vtw[hs, pl.ds(starts[i], tq)], pw[i * tq:(i + 1) * tq], preferred_element_type=F32)

        o_cmps.append(o_cmp)
        o_wins.append(o_win)

    heads = range(N_KV)

    def slc_scores(k0):
        k_rows = kcat[pl.ds(k0, SLC_TILE), :]
        return tuple(jnp.dot(k_rows, rhs_slcs[n], preferred_element_type=F32) for n in heads)

    def slc_absorb(k0, sts, stats, q_limit=None):
        if q_limit is not None:
            kpos = k0 + lax.broadcasted_iota(jnp.int32, (SLC_TILE, 1), 0)
            sts = [jnp.where(kpos <= q_limit, st, NEG) for st in sts]
        m2s = [jnp.maximum(stats[n][0], jnp.max(sts[n], axis=0, keepdims=True)) for n in heads]
        es = [jnp.exp(sts[n] - m2s[n]) for n in heads]
        pvs = [jnp.dot(vts[n * HEAD_DIM:(n + 1) * HEAD_DIM, pl.ds(k0, SLC_TILE)], es[n].astype(BF16),
                       preferred_element_type=F32) for n in heads]
        new = []
        for n in heads:
            m, l, acc = stats[n]
            a = jnp.exp(m - m2s[n])
            new.append((m2s[n], a * l + jnp.sum(es[n], axis=0, keepdims=True), a * acc + pvs[n]))
        return tuple(new)

    def slc_issue(k0, slot):
        for n, st in enumerate(slc_scores(k0)):
            st_buf[slot, n] = st

    def slc_held(slot):
        return [st_buf[slot, n] for n in heads]

    def slc_pair(i, stats):
        k0 = pl.multiple_of(2 * i * SLC_TILE, SLC_TILE)
        slc_issue(k0 + SLC_TILE, 1)
        stats = slc_absorb(k0, slc_held(0), stats)
        slc_issue(k0 + 2 * SLC_TILE, 0)
        return slc_absorb(k0 + SLC_TILE, slc_held(1), stats)

    n_full = (j * tq) // SLC_TILE
    start = (jnp.full((1, cols4), NEG, F32), jnp.zeros((1, cols4), F32), jnp.zeros((HEAD_DIM, cols4), F32))
    slc_issue(0, 0)
    stats = lax.fori_loop(0, n_full // 2, slc_pair, (start,) * N_KV)
    odd = (n_full % 2) == 1
    k_even = pl.multiple_of((n_full // 2) * 2 * SLC_TILE, SLC_TILE)
    stats = slc_absorb(k_even, slc_held(0), stats, q_limit=jnp.where(odd, jnp.int32(seq), qpos4))

    def last_tile(stats):
        k_diag = pl.multiple_of(n_full * SLC_TILE, SLC_TILE)
        return slc_absorb(k_diag, slc_scores(k_diag), stats, q_limit=qpos4)

    carry = lax.cond(odd, last_tile, lambda s: s, stats)

    for n in range(N_KV):
        _, l_s, acc_s = carry[n]
        o_slc = acc_s / l_s
        for g in range(GROUP):
            h = GROUP * n + g
            cs = slice(g * tq, (g + 1) * tq)
            outs[h] = (gates_t[3 * h:3 * h + 1, :] * o_cmps[n][:, cs] + gates_t[3 * h + 1:3 * h + 2, :] * o_slc[:, cs]
                       + gates_t[3 * h + 2:3 * h + 3, :] * o_wins[n][:, cs])
    o_ref[...] = jnp.concatenate(outs, axis=0).T


def _nsa_prompt(q3, kv3, kc, vc, win3, sm3):
    b, t, _ = q3.shape
    assert t % SLC_TILE == 0 and t % Q_BLOCK == 0
    nc = t // CMP_BLOCK
    est = 2 * (2 * t * 256 * 4) + 4 * nc * KV_W * 4 + 8 * Q_BLOCK * 512 * 4 + 16 * 512 * 640 * 4
    kaux = _key_aux(t, lambda r: r, True)
    caux = _key_aux(nc, lambda r: r * CMP_BLOCK + (CMP_BLOCK - 1) / 2, False)
    return pl.pallas_call(
        functools.partial(_nsa_prompt_body, seq=t),
        grid=(b, t // Q_BLOCK),
        in_specs=[
            pl.BlockSpec((None, Q_BLOCK, 512), lambda i, j: (i, j, 0)),
            pl.BlockSpec((None, t, 2 * KV_W), lambda i, j: (i, 0, 1)),
            pl.BlockSpec((None, nc, KV_W), lambda i, j: (i, 0, 0)),
            pl.BlockSpec((None, nc, KV_W), lambda i, j: (i, 0, 0)),
            pl.BlockSpec((None, t, 2 * KV_W), lambda i, j: (i, 0, 0)),
            pl.BlockSpec((None, Q_BLOCK, SM_W), lambda i, j: (i, j, 0)),
            pl.BlockSpec((t, LANES), lambda i, j: (0, 0)),
            pl.BlockSpec((nc, LANES), lambda i, j: (0, 0)),
        ],
        out_specs=pl.BlockSpec((None, Q_BLOCK, 512), lambda i, j: (i, j, 0)),
        out_shape=jax.ShapeDtypeStruct((b, t, 512), F32),
        scratch_shapes=[pltpu.VMEM((nc, Q_BLOCK), F32),
                        pltpu.VMEM((t, 2 * KV_W), BF16), pltpu.VMEM((t, 2 * KV_W), BF16), pltpu.VMEM((nc, 2 * KV_W), BF16),
                        pltpu.VMEM((KV_W, t), BF16), pltpu.VMEM((KV_W, t), BF16), pltpu.VMEM((KV_W, nc), BF16),
                        pltpu.VMEM((2, N_KV, SLC_TILE, GROUP * Q_BLOCK), F32)],
        compiler_params=_params(("parallel", "arbitrary"), est),
        name="nsa_prompt",
    )(q3, kv3, kc, vc, win3, sm3, kaux, caux)


_BNN = (((2,), (1,)), ((0,), (0,)))
_BNT = (((2,), (2,)), ((0,), (0,)))
_BTN = (((1,), (1,)), ((0,), (0,)))


def _bdot(a, b, dims=_BNN):
    return lax.dot_general(a.astype(BF16), b.astype(BF16), dims, preferred_element_type=F32)


def _solve_unit_lower(nmat, rhs):
    c = nmat.shape[1]
    ii = lax.broadcasted_iota(jnp.int32, (c, c), 0)
    jj = lax.broadcasted_iota(jnp.int32, (c, c), 1)
    tmat = jnp.where(ii == jj, 1.0, 0.0) - nmat
    pw = _bdot(nmat, nmat)
    steps = c.bit_length() - 2
    for _ in range(steps - 1):
        both = _bdot(jnp.concatenate([tmat, pw], axis=1), pw)
        tmat = tmat + both[:, 0:c]
        pw = both[:, c:2 * c]
    tmat = tmat + _bdot(tmat, pw)
    return _bdot(tmat, rhs)


def _l2_scale(x, ind, ind_t):
    ssq = _dot_sel(x * x, ind)
    return _dot_sel(lax.rsqrt(ssq + RMS_EPS), ind_t)


def _gdn_prompt_body(gqkv_ref, gz_ref, sm_ref, cw_ref, alog_r_ref, dtb_r_ref, alog_c_ref, dtb_c_ref, nw_ref,
                     o_ref, s_ref, xbuf):
    t = pl.program_id(1)
    c = GDN_CHUNK
    nt = gqkv_ref.shape[0]
    nchunk = nt // c
    pad = SUBLANES

    @pl.when(t == 0)
    def _():
        xbuf[0:pad, :] = jnp.zeros((pad, GDN_QKV), F32)
        s_ref[...] = jnp.zeros(s_ref.shape, F32)

    xbuf[pad:pad + nt, :] = gqkv_ref[...]
    y = cw_ref[GDN_CONV - 1:GDN_CONV, :] * xbuf[pad:pad + nt, :]
    for tap in range(GDN_CONV - 1):
        y = y + cw_ref[tap:tap + 1, :] * xbuf[pad - (GDN_CONV - 1) + tap:pad - (GDN_CONV - 1) + tap + nt, :]
    xbuf[0:pad, :] = xbuf[nt:nt + pad, :]
    act = _silu(y)
    ind = _head_indicator(GDN_W, N_GDN)
    ind_t = _head_indicator_t(N_GDN, GDN_W)
    qraw = act[:, 0:GDN_W]
    kraw = act[:, GDN_W:2 * GDN_W]
    v = act[:, 2 * GDN_W:3 * GDN_W]
    qn = qraw * _l2_scale(qraw, ind, ind_t) * (HEAD_DIM ** -0.5)
    kn = kraw * _l2_scale(kraw, ind, ind_t)

    sm = sm_ref[...]
    sm_t = sm.T
    g_col = -jnp.exp(alog_r_ref[...]) * _softplus(sm[:, SM_GA:SM_GA + N_GDN] + dtb_r_ref[...])
    g_row = -jnp.exp(alog_c_ref[...]) * _softplus(sm_t[SM_GA:SM_GA + N_GDN, :] + dtb_c_ref[...])
    beta = _sigmoid(sm[:, SM_GB:SM_GB + N_GDN])
    ti = lax.broadcasted_iota(jnp.int32, (nt, nt), 0)
    tj = lax.broadcasted_iota(jnp.int32, (nt, nt), 1)
    same = (ti // c) == (tj // c)
    upto = jnp.where(same & (ti <= tj), 1.0, 0.0)
    gc_col = _dot_tn_sel(upto, g_col)
    gc_row = _dot_sel(g_row, upto)

    items = [(g, h) for g in range(nchunk) for h in range(N_GDN)]
    rows = lambda g: slice(g * c, (g + 1) * c)
    heads = lambda x: jnp.stack([x[rows(g), h * HEAD_DIM:(h + 1) * HEAD_DIM] for g, h in items], axis=0)
    cols = lambda x: jnp.stack([x[rows(g), h:h + 1] for g, h in items], axis=0)
    qb, kb, vb = heads(qn), heads(kn), heads(v)
    gcc = cols(gc_col)
    bcol = cols(beta)
    gcr = jnp.stack([gc_row[h:h + 1, rows(g)] for g, h in items], axis=0)
    gl = jnp.stack([gc_row[h:h + 1, (g + 1) * c - 1:(g + 1) * c] for g, h in items], axis=0)
    ii = lax.broadcasted_iota(jnp.int32, (c, c), 0)
    jj = lax.broadcasted_iota(jnp.int32, (c, c), 1)
    dec = jnp.exp(jnp.where(ii >= jj, gcc - gcr, NEG))
    egc = jnp.exp(gcc)
    kbeta = kb * bcol
    nmat = jnp.where(ii > jj, _bdot(kbeta, kb, _BNT) * dec, 0.0)
    x = _solve_unit_lower(nmat, jnp.concatenate([vb * bcol, kbeta * egc], axis=2))
    val, kcd = x[:, :, 0:HEAD_DIM], x[:, :, HEAD_DIM:2 * HEAD_DIM]
    inner = _bdot(qb, kb, _BNT) * dec
    qg = qb * egc
    kend = kb * jnp.exp(gl - gcc)
    egl = jnp.exp(gl)

    s = s_ref[...]
    outs = []
    for g in range(nchunk):
        it = slice(g * N_GDN, (g + 1) * N_GDN)
        vn = val[it] - _bdot(kcd[it], s)
        og = _bdot(qg[it], s) + _bdot(inner[it], vn)
        s = s * egl[it] + _bdot(kend[it], vn, _BTN)
        outs.append(jnp.concatenate([og[h] for h in range(N_GDN)], axis=1))
    s_ref[...] = s
    o = jnp.concatenate(outs, axis=0)
    rms = _dot_sel(lax.rsqrt(_dot_sel(o * o, ind) * (1.0 / HEAD_DIM) + RMS_EPS), ind_t)
    o_ref[...] = o * rms * nw_ref[...] * _silu(gz_ref[...])


GDN_TOK = 4 * GDN_CHUNK


def _gdn_prompt(gqkv3, gz3, sm3, cw, alog, dtb, nw_t):
    b, t, _ = gqkv3.shape
    assert t % GDN_TOK == 0
    c = GDN_TOK
    est = 4 * c * GDN_QKV * 4 + 64 * c * 512 * 4
    return pl.pallas_call(
        _gdn_prompt_body,
        grid=(b, t // c),
        in_specs=[
            pl.BlockSpec((None, c, GDN_QKV), lambda i, j: (i, j, 0)),
            pl.BlockSpec((None, c, GDN_W), lambda i, j: (i, j, 0)),
            pl.BlockSpec((None, c, SM_W), lambda i, j: (i, j, 0)),
            _resident((GDN_CONV, GDN_QKV)), _resident((1, N_GDN)), _resident((1, N_GDN)),
            _resident((N_GDN, 1)), _resident((N_GDN, 1)), _resident((1, GDN_W)),
        ],
        out_specs=[pl.BlockSpec((None, c, GDN_W), lambda i, j: (i, j, 0)),
                   pl.BlockSpec((None, N_GDN, HEAD_DIM, HEAD_DIM), lambda i, j: (i, 0, 0, 0))],
        out_shape=[jax.ShapeDtypeStruct((b, t, GDN_W), F32),
                   jax.ShapeDtypeStruct((b, N_GDN, HEAD_DIM, HEAD_DIM), F32)],
        scratch_shapes=[pltpu.VMEM((c + SUBLANES, GDN_QKV), F32)],
        compiler_params=_params(("parallel", "arbitrary"), est),
        name="gdn_prompt",
    )(gqkv3, gz3, sm3, cw, alog.reshape(1, N_GDN), dtb.reshape(1, N_GDN),
      alog.reshape(N_GDN, 1), dtb.reshape(N_GDN, 1), nw_t)


def _post_tail(x1, ffn, p, lng_ref, lnb_ref, wg_ref, wp_ref):
    x2 = _layer_norm(DN_ALPHA * x1 + ffn, lng_ref[1:2, :], lnb_ref[1:2, :])
    ple = _sigmoid(jnp.dot(x2.astype(BF16), wg_ref[...], preferred_element_type=F32)) \
        * jnp.dot(p.astype(BF16), wp_ref[...], preferred_element_type=F32)
    return _layer_norm(DN_ALPHA * x2 + ple, lng_ref[2:3, :], lnb_ref[2:3, :])


def _post_head(onsa_ref, og_ref, x_ref, wo_ref, lng_ref, lnb_ref):
    attn = jnp.dot(onsa_ref[...].astype(BF16), wo_ref[0:512, :], preferred_element_type=F32) \
        + jnp.dot(og_ref[...].astype(BF16), wo_ref[512:1024, :], preferred_element_type=F32)
    return _layer_norm(DN_ALPHA * x_ref[...] + attn, lng_ref[0:1, :], lnb_ref[0:1, :])


def _post_prompt_body(onsa_ref, og_ref, x_ref, p_ref, wo_ref, lng_ref, lnb_ref, wu_ref, cw_ref, wd_ref,
                      wp_ref, wg_ref, y_ref, fbuf_ref, upbuf):
    t = pl.program_id(1)
    tm = x_ref.shape[0]
    pad = SUBLANES

    @pl.when(t == 0)
    def _():
        upbuf[0:pad, :] = jnp.zeros((pad, D_FF), F32)

    x1 = _post_head(onsa_ref, og_ref, x_ref, wo_ref, lng_ref, lnb_ref)
    x1b = x1.astype(BF16)
    ffn = jnp.zeros((tm, D_MODEL), F32)
    for cc in range(D_FF // FF_CHUNK):
        cs = slice(cc * FF_CHUNK, (cc + 1) * FF_CHUNK)
        up_g = jnp.dot(x1b, wu_ref[:, cs], preferred_element_type=F32)
        up_v = jnp.dot(x1b, wu_ref[:, D_FF + cc * FF_CHUNK:D_FF + (cc + 1) * FF_CHUNK], preferred_element_type=F32)
        upbuf[pad:pad + tm, cs] = up_g
        hg = cw_ref[FFN_CONV - 1:FFN_CONV, cs] * up_g
        for tap in range(FFN_CONV - 1):
            o0 = pad - (FFN_CONV - 1) + tap
            hg = hg + cw_ref[tap:tap + 1, cs] * upbuf[o0:o0 + tm, cs]
        ffn = ffn + jnp.dot((_gelu(hg) * up_v).astype(BF16), wd_ref[cs, :], preferred_element_type=F32)
    fbuf_ref[...] = upbuf[pad + tm - (FFN_CONV - 1):pad + tm, :]
    upbuf[0:pad, :] = upbuf[tm:tm + pad, :]
    y_ref[...] = _post_tail(x1, ffn, p_ref[...], lng_ref, lnb_ref, wg_ref, wp_ref)


def _post_decode_body(onsa_ref, og_ref, x_ref, p_ref, fst_ref, wo_ref, lng_ref, lnb_ref, wu_ref, cw_ref, wd_ref,
                      wp_ref, wg_ref, y_ref, fnew_ref):
    tm = x_ref.shape[0]
    x1 = _post_head(onsa_ref, og_ref, x_ref, wo_ref, lng_ref, lnb_ref)
    x1b = x1.astype(BF16)
    ffn = jnp.zeros((tm, D_MODEL), F32)
    for cc in range(D_FF // FF_CHUNK):
        cs = slice(cc * FF_CHUNK, (cc + 1) * FF_CHUNK)
        cs1 = slice(D_FF + cc * FF_CHUNK, D_FF + (cc + 1) * FF_CHUNK)
        up_g = jnp.dot(x1b, wu_ref[:, cs], preferred_element_type=F32)
        up_v = jnp.dot(x1b, wu_ref[:, cs1], preferred_element_type=F32)
        hg = cw_ref[0:1, cs] * fst_ref[:, cs] + cw_ref[1:2, cs] * fst_ref[:, cs1] + cw_ref[2:3, cs] * up_g
        fnew_ref[:, cs] = fst_ref[:, cs1]
        fnew_ref[:, cs1] = up_g
        ffn = ffn + jnp.dot((_gelu(hg) * up_v).astype(BF16), wd_ref[cs, :], preferred_element_type=F32)
    y_ref[...] = _post_tail(x1, ffn, p_ref[...], lng_ref, lnb_ref, wg_ref, wp_ref)


def _post_weight_specs():
    return [_resident((D_MODEL, D_MODEL)), _resident((3, D_MODEL)), _resident((3, D_MODEL)),
            _resident((D_MODEL, 2 * D_FF)), _resident((FFN_CONV, D_FF)), _resident((D_FF, D_MODEL)),
            _resident((PLE_DIM, D_MODEL)), _resident((D_MODEL, D_MODEL))]


_POST_WEIGHT_BYTES = 2 * (2 * D_MODEL * D_MODEL + 3 * D_MODEL * D_FF + PLE_DIM * D_MODEL) + 4 * 9 * D_FF


def _post_prompt(onsa3, og3, x3, p3, p_base, wts, tm):
    b, t, _ = x3.shape
    row = lambda n: pl.BlockSpec((None, tm, n), lambda i, j: (i, j, 0))
    p_spec = pl.BlockSpec((None, tm, PLE_DIM), lambda i, j: (i + p_base, j, 0))
    est = _POST_WEIGHT_BYTES + 2 * tm * (512 + 512 + 1024 + 256 + 1024) * 4 + (tm + 8) * D_FF * 4 + 12 * tm * 1024 * 4
    return pl.pallas_call(
        _post_prompt_body,
        grid=(b, t // tm),
        in_specs=[row(512), row(512), row(D_MODEL), p_spec] + _post_weight_specs(),
        out_specs=[row(D_MODEL), pl.BlockSpec((None, FFN_CONV - 1, D_FF), lambda i, j: (i, 0, 0))],
        out_shape=[jax.ShapeDtypeStruct((b, t, D_MODEL), F32),
                   jax.ShapeDtypeStruct((b, FFN_CONV - 1, D_FF), F32)],
        scratch_shapes=[pltpu.VMEM((tm + SUBLANES, D_FF), F32)],
        compiler_params=_params(("parallel", "arbitrary"), est),
        name="post_prompt",
    )(onsa3, og3, x3, p3, *wts)


def _post_decode(onsa2, og2, x2, p2, fst2, wts):
    m = x2.shape[0]
    full = lambda n: pl.BlockSpec((m, n), lambda i: (0, 0))
    est = _POST_WEIGHT_BYTES + 2 * m * (512 + 512 + 1024 + 256 + 1024 + 4 * D_FF) * 4 + 12 * m * 1024 * 4
    return pl.pallas_call(
        _post_decode_body,
        grid=(1,),
        in_specs=[full(512), full(512), full(D_MODEL), full(PLE_DIM), full(2 * D_FF)] + _post_weight_specs(),
        out_specs=[full(D_MODEL), full(2 * D_FF)],
        out_shape=[jax.ShapeDtypeStruct((m, D_MODEL), F32), jax.ShapeDtypeStruct((m, 2 * D_FF), F32)],
        compiler_params=_params(("arbitrary",), est),
        name="post_decode",
    )(onsa2, og2, x2, p2, fst2, *wts)


DEC_AUX0 = 96


def _decode_aux(n_pages, page):
    shape = (n_pages, LANES, page)
    pos = (lax.broadcasted_iota(jnp.int32, shape, 0) * page + lax.broadcasted_iota(jnp.int32, shape, 2)).astype(F32)
    row = lax.broadcasted_iota(jnp.int32, shape, 1)
    hi = jnp.floor(pos * (1.0 / SLC_BLOCK))
    aux = jnp.where(row == DEC_AUX0, hi, jnp.where(row == DEC_AUX0 + 1, pos - hi * SLC_BLOCK,
                    jnp.where((row == DEC_AUX0 + 2) | (row == DEC_AUX0 + 3), 1.0, 0.0)))
    assert 2 * (n_pages * page // SLC_BLOCK) <= DEC_AUX0
    aux = jnp.where((row < DEC_AUX0) & (row % 2 == 0) & (hi == (row // 2).astype(F32)), NEG, aux)
    return aux.astype(BF16)


def _nsa_decode_body(pt_ref, *refs, n_pages, page, chained, seqs):
    del pt_ref
    n_in = seqs * n_pages
    page_refs = refs[:n_in]
    (q_ref, kvn_ref, winn_ref, sm_ref, wbuf_ref, pe_ref, phi_ref, daux_ref) = refs[n_in:n_in + 8]
    o_ref, wout_ref = refs[n_in + 8 + (1 if chained else 0):]
    chains = [_nsa_decode_seq(page_refs[s * n_pages:(s + 1) * n_pages], q_ref.at[s], kvn_ref.at[s], winn_ref.at[s],
                              sm_ref.at[s], wbuf_ref.at[s], pe_ref, phi_ref, daux_ref, o_ref.at[s], wout_ref.at[s], page)
              for s in range(seqs)]
    for _ in zip(*chains):
        pass


def _nsa_decode_seq(page_refs, q_ref, kvn_ref, winn_ref, sm_ref, wbuf_ref, pe_ref, phi_ref, daux_ref, o_ref, wout_ref,
                    page):
    n_pages = len(page_refs)
    past = n_pages * page
    ncv = past // CMP_BLOCK
    nsb = past // SLC_BLOCK
    assert ncv <= LANES and ncv % 2 == 0
    q = q_ref[...] * (HEAD_DIM ** -0.5)
    zero = jnp.zeros((1, HEAD_DIM), F32)
    rows = []
    for h in range(N_HEADS):
        piece = q[:, h * HEAD_DIM:(h + 1) * HEAD_DIM]
        rows.append(jnp.concatenate([piece, zero] if h < GROUP else [zero, piece], axis=1))
    qbd = jnp.concatenate(rows, axis=0)
    qbd_b = qbd.astype(BF16)
    hrow = lax.broadcasted_iota(jnp.int32, (N_HEADS, 1), 0)
    slope = jnp.concatenate([jnp.full((1, 1), 2.0 ** -(h + 1), F32) for h in range(N_HEADS)], axis=0)

    zero_hh = jnp.zeros((HEAD_DIM, HEAD_DIM), F32)
    both_heads = lambda w: jnp.concatenate([jnp.concatenate([w, zero_hh], axis=1),
                                            jnp.concatenate([zero_hh, w], axis=1)], axis=0)
    pe_row = lambda which: jnp.concatenate([jnp.mean(pe_ref[which], axis=0, keepdims=True)] * N_KV, axis=1)
    pool = jnp.where(lax.broadcasted_iota(jnp.int32, (past, ncv), 0) // CMP_BLOCK
                     == lax.broadcasted_iota(jnp.int32, (past, ncv), 1), 1.0, 0.0)
    pool_t = jnp.where(lax.broadcasted_iota(jnp.int32, (ncv, past), 1) // CMP_BLOCK
                       == lax.broadcasted_iota(jnp.int32, (ncv, past), 0), 1.0, 0.0).astype(BF16)
    q_phi = lax.dot_general(qbd, both_heads(phi_ref[0]), _NT, precision=lax.Precision.HIGHEST,
                            preferred_element_type=F32)
    yield
    row_s = jnp.concatenate([_dot(q_phi, pr[0:KV_W, :]) for pr in page_refs], axis=1)
    yield
    cidx = lax.broadcasted_iota(jnp.int32, (1, ncv), 1)
    c_mid = cidx.astype(F32) * CMP_BLOCK + (CMP_BLOCK - 1) / 2
    s = _dot_sel(row_s, pool) * (1.0 / CMP_BLOCK) + jnp.sum(q_phi * pe_row(0), axis=-1, keepdims=True) \
        - slope * (float(past) - c_mid)
    p = _softmax_rows(s)
    yield
    p_rows = jnp.dot(p.astype(BF16), pool_t, preferred_element_type=F32) * (1.0 / CMP_BLOCK)
    u = jnp.sum(p, axis=-1, keepdims=True) * pe_row(1)
    for i, pr in enumerate(page_refs):
        u = u + _dot_nt(p_rows[:, i * page:(i + 1) * page], pr[KV_W:2 * KV_W, :])
    yield
    o_cmp = jnp.dot(u, both_heads(phi_ref[1]), precision=lax.Precision.HIGHEST, preferred_element_type=F32)

    p_l = p if ncv == LANES else jnp.concatenate([p, jnp.zeros((N_HEADS, LANES - ncv), F32)], axis=1)
    pair = p_l + pltpu.roll(p_l, LANES - 1, 1)
    pooled = jnp.concatenate([jnp.sum(pair[n * GROUP:(n + 1) * GROUP], axis=0, keepdims=True) for n in range(N_KV)],
                             axis=0)
    lane = lax.broadcasted_iota(jnp.int32, (N_KV, LANES), 1)
    blk = lane // 2
    is_blk = ((lane % 2) == 0) & (lane < ncv)
    is_cur = lane == ncv
    forced = is_cur | (is_blk & ((blk == 0) | ((nsb - blk) < N_LOCAL)))
    score = jnp.where(forced, BIG, jnp.where(is_blk, pooled, -BIG))
    rank = jnp.zeros((N_KV, LANES), F32)
    cand = [2 * b for b in range(nsb)] + [ncv]
    for lc in cand:
        col = score[:, lc:lc + 1]
        rank = rank + jnp.where(lane > lc, jnp.where(col >= score, 1.0, 0.0), jnp.where(col > score, 1.0, 0.0))
    sel = jnp.where((rank < float(min(TOP_N, nsb + 1))) & (is_blk | is_cur), 1.0, 0.0)
    unsel = jnp.where(is_blk, 1.0 - sel, 0.0)
    unsel8 = jnp.concatenate([jnp.broadcast_to(unsel[n:n + 1], (GROUP, LANES)) for n in range(N_KV)], axis=0)

    lane8 = lax.broadcasted_iota(jnp.int32, (N_HEADS, LANES), 1)
    q_side = unsel8 + jnp.where(lane8 == DEC_AUX0, slope * SLC_BLOCK,
                                jnp.where(lane8 == DEC_AUX0 + 1, slope,
                                          jnp.where(lane8 == DEC_AUX0 + 2, -slope * float(past - past % SLC_BLOCK),
                                                    jnp.where(lane8 == DEC_AUX0 + 3, -slope * float(past % SLC_BLOCK), 0.0))))
    lhs_cat = jnp.concatenate([qbd, q_side], axis=1).astype(BF16)
    kvn = kvn_ref[...]
    s_new = jnp.sum(qbd * kvn[:, 2 * KV_W:3 * KV_W], axis=-1, keepdims=True)
    s_parts = []
    for i, pr in enumerate(page_refs):
        rhs_cat = jnp.concatenate([pr[2 * KV_W:3 * KV_W, :].astype(BF16), daux_ref[i]], axis=0)
        s_parts.append(jnp.dot(lhs_cat, rhs_cat, preferred_element_type=F32))
    yield
    s_all = jnp.concatenate(s_parts, axis=1)
    m = jnp.maximum(jnp.max(s_all, axis=-1, keepdims=True), s_new)
    e = jnp.exp(s_all - m)
    e_new = jnp.exp(s_new - m)
    acc = e_new * kvn[:, 3 * KV_W:4 * KV_W]
    for i, pr in enumerate(page_refs):
        acc = acc + _dot_nt(e[:, i * page:(i + 1) * page], pr[3 * KV_W:4 * KV_W, :])
    yield
    o_slc = acc / (jnp.sum(e, axis=-1, keepdims=True) + e_new)

    wb = wbuf_ref.shape[1]
    winn = winn_ref[...]
    rpos = lax.broadcasted_iota(jnp.int32, (1, wb), 1)
    dist = wb - rpos
    sw = _dot(qbd_b, wbuf_ref[0:KV_W, :]) - slope * dist.astype(F32)
    sw = jnp.where(dist < WINDOW, sw, NEG)
    sw_new = jnp.sum(qbd * winn[:, 0:KV_W], axis=-1, keepdims=True)
    mw = jnp.maximum(jnp.max(sw, axis=-1, keepdims=True), sw_new)
    ew = jnp.exp(sw - mw)
    ew_new = jnp.exp(sw_new - mw)
    o_win = (_dot_nt(ew, wbuf_ref[KV_W:2 * KV_W, :]) + ew_new * winn[:, KV_W:2 * KV_W]) \
        / (jnp.sum(ew, axis=-1, keepdims=True) + ew_new)
    winn_col = jnp.broadcast_to(winn, (SUBLANES, 2 * KV_W)).T[:, 0:1]
    shifted = pltpu.roll(wbuf_ref[...], wb - 1, 1)
    wout_ref[...] = jnp.where(lax.broadcasted_iota(jnp.int32, (2 * KV_W, wb), 1) == wb - 1, winn_col, shifted)

    gates = _sigmoid(sm_ref[:, 0:3 * N_HEADS])
    glane = lax.broadcasted_iota(jnp.int32, (N_HEADS, 3 * N_HEADS), 1)
    gsel = lambda r: jnp.sum(jnp.where(glane == 3 * hrow + r, gates, 0.0), axis=-1, keepdims=True)
    o8 = gsel(0) * o_cmp + gsel(1) * o_slc + gsel(2) * o_win
    pieces = []
    for h in range(N_HEADS):
        n = h // GROUP
        pieces.append(o8[h:h + 1, n * HEAD_DIM:(n + 1) * HEAD_DIM])
    o_ref[...] = jnp.concatenate(pieces, axis=1)
    yield


def _nsa_decode(page_table, cache3, page_base, q3, kvn3, winn3, sm3, wbuf3, seq_base, pe, phi, wout_prev):
    nb, n_pages = page_table.shape
    page = cache3.shape[2]
    wb = wbuf3.shape[2]
    assert wb == WINDOW and n_pages * page >= wb
    chained = wout_prev is not None
    seqs = DEC_SEQS if nb % DEC_SEQS == 0 else 1
    assert seq_base % seqs == 0
    page_specs = [pl.BlockSpec((None, 4 * KV_W, page),
                               functools.partial(lambda i, pt, s, k: (pt[i * seqs + s, k] + page_base, 0, 0), s=s, k=k))
                  for s in range(seqs) for k in range(n_pages)]
    one = lambda n: pl.BlockSpec((seqs, 1, n), lambda i, pt: (i, 0, 0))
    win_spec = pl.BlockSpec((seqs, 2 * KV_W, wb), lambda i, pt: (i + seq_base // seqs, 0, 0))
    in_specs = page_specs + [one(512), one(512), one(256), one(SM_W), win_spec,
                             pl.BlockSpec((2, CMP_BLOCK, HEAD_DIM), lambda i, pt: (0, 0, 0)),
                             pl.BlockSpec((2, HEAD_DIM, HEAD_DIM), lambda i, pt: (0, 0, 0)),
                             pl.BlockSpec((n_pages, LANES, page), lambda i, pt: (0, 0, 0))]
    args = [page_table] + [cache3] * (seqs * n_pages) + [q3, kvn3, winn3, sm3, wbuf3, pe, phi,
                                                          _decode_aux(n_pages, page)]
    aliases = {}
    if chained:
        in_specs.append(pl.BlockSpec(memory_space=pl.ANY))
        args.append(wout_prev)
        aliases = {len(args) - 1: 1}
    grid_spec = pltpu.PrefetchScalarGridSpec(
        num_scalar_prefetch=1, grid=(nb // seqs,), in_specs=in_specs, out_specs=[one(512), win_spec])
    est = seqs * (2 * n_pages * page * 512 * 4 + 4 * wb * 256 * 4) + 8 * 1024 * 1024
    return pl.pallas_call(
        functools.partial(_nsa_decode_body, n_pages=n_pages, page=page, chained=chained, seqs=seqs),
        grid_spec=grid_spec,
        out_shape=[jax.ShapeDtypeStruct((nb, 1, 512), F32), jax.ShapeDtypeStruct(wbuf3.shape, F32)],
        input_output_aliases=aliases,
        compiler_params=_params(("arbitrary",), est),
        name="nsa_decode",
    )(*args)


def _gdn_dec_prep_body(gqkv_ref, cst_ref, sm_ref, cw_ref, alog_ref, dtb_ref, qn_ref, kn_ref, v_ref, db_ref, cnew_ref):
    w = GDN_QKV
    xn = gqkv_ref[...]
    y = cw_ref[GDN_CONV - 1:GDN_CONV, :] * xn
    for tap in range(GDN_CONV - 1):
        y = y + cw_ref[tap:tap + 1, :] * cst_ref[:, tap * w:(tap + 1) * w]
    for tap in range(GDN_CONV - 2):
        cnew_ref[:, tap * w:(tap + 1) * w] = cst_ref[:, (tap + 1) * w:(tap + 2) * w]
    cnew_ref[:, (GDN_CONV - 2) * w:(GDN_CONV - 1) * w] = xn
    act = _silu(y)
    ind = _head_indicator(GDN_W, N_GDN)
    ind_t = _head_indicator_t(N_GDN, GDN_W)
    qraw = act[:, 0:GDN_W]
    kraw = act[:, GDN_W:2 * GDN_W]
    qn_ref[...] = (qraw * _l2_scale(qraw, ind, ind_t) * (HEAD_DIM ** -0.5)).T
    kn_ref[...] = (kraw * _l2_scale(kraw, ind, ind_t)).T
    v_ref[...] = act[:, 2 * GDN_W:3 * GDN_W].T
    sm = sm_ref[...]
    g = -jnp.exp(alog_ref[...]) * _softplus(sm[:, SM_GA:SM_GA + N_GDN] + dtb_ref[...])
    beta = _sigmoid(sm[:, SM_GB:SM_GB + N_GDN])
    m = sm.shape[0]
    db_ref[...] = jnp.concatenate([jnp.exp(g), beta, jnp.zeros((m, SM_W - 2 * N_GDN), F32)], axis=1).T


def _gdn_dec_prep(gqkv2, cst2, sm2, cw, alog, dtb):
    m = gqkv2.shape[0]
    full = lambda n: pl.BlockSpec((m, n), lambda i: (0, 0))
    full_t = lambda n: pl.BlockSpec((n, m), lambda i: (0, 0))
    est = 2 * m * (8 * GDN_QKV + 4 * 512) * 4 + 8 * m * GDN_QKV * 4
    return pl.pallas_call(
        _gdn_dec_prep_body,
        grid=(1,),
        in_specs=[full(GDN_QKV), full((GDN_CONV - 1) * GDN_QKV), full(SM_W),
                  pl.BlockSpec((GDN_CONV, GDN_QKV), lambda i: (0, 0)),
                  pl.BlockSpec((1, N_GDN), lambda i: (0, 0)), pl.BlockSpec((1, N_GDN), lambda i: (0, 0))],
        out_specs=[full_t(GDN_W), full_t(GDN_W), full_t(GDN_W), full_t(SM_W), full((GDN_CONV - 1) * GDN_QKV)],
        out_shape=[jax.ShapeDtypeStruct((GDN_W, m), F32)] * 3 + [jax.ShapeDtypeStruct((SM_W, m), F32),
                   jax.ShapeDtypeStruct((m, (GDN_CONV - 1) * GDN_QKV), F32)],
        compiler_params=_params(("arbitrary",), est),
        name="gdn_decode_prep",
    )(gqkv2, cst2, sm2, cw, alog.reshape(1, N_GDN), dtb.reshape(1, N_GDN))


def _gdn_dec_step_body(qn_ref, kn_ref, v_ref, db_ref, gz_ref, s_ref, nw_ref, *out_refs):
    o_ref, snew_ref = out_refs[-2:]
    h = pl.program_id(0)
    a = db_ref[pl.ds(h, 1), :]
    bta = db_ref[pl.ds(N_GDN + h, 1), :]
    sk = jnp.zeros(v_ref.shape, F32)
    for d in range(HEAD_DIM):
        sk = sk + kn_ref[d:d + 1, :] * s_ref[d]
    vn = bta * (v_ref[...] - a * sk)
    o = jnp.zeros(v_ref.shape, F32)
    for d in range(HEAD_DIM):
        s_d = a * s_ref[d] + kn_ref[d:d + 1, :] * vn
        snew_ref[d] = s_d
        o = o + qn_ref[d:d + 1, :] * s_d
    o = o * lax.rsqrt(jnp.mean(o * o, axis=0, keepdims=True) + RMS_EPS) * nw_ref[...]
    o_ref[...] = o * _silu(gz_ref[...])


def _gdn_dec_step(qn_t, kn_t, v_t, db_t, gz_t, s4, layer_base, nw_col, s_prev):
    nb = qn_t.shape[1]
    head = lambda: pl.BlockSpec((HEAD_DIM, nb), lambda i: (i, 0))
    st = pl.BlockSpec((None, HEAD_DIM, HEAD_DIM, nb), lambda i: (i + layer_base, 0, 0, 0))
    in_specs = [head(), head(), head(), pl.BlockSpec((SM_W, nb), lambda i: (0, 0)), head(), st,
                pl.BlockSpec((HEAD_DIM, 1), lambda i: (0, 0))]
    args = [qn_t, kn_t, v_t, db_t, gz_t, s4, nw_col]
    aliases = {}
    if s_prev is not None:
        in_specs.append(pl.BlockSpec(memory_space=pl.ANY))
        args.append(s_prev)
        aliases = {len(args) - 1: 1}
    est = 4 * HEAD_DIM * HEAD_DIM * nb * 4 + 4 * 1024 * 1024
    return pl.pallas_call(
        _gdn_dec_step_body,
        grid=(N_GDN,),
        in_specs=in_specs,
        out_specs=[head(), st],
        out_shape=[jax.ShapeDtypeStruct((GDN_W, nb), F32), jax.ShapeDtypeStruct(s4.shape, F32)],
        input_output_aliases=aliases,
        compiler_params=_params(("parallel",), est),
        name="gdn_decode_step",
    )(*args)


def _prep_w_in(w):
    small = jnp.concatenate([w[:, C_GATE:C_GQKV], w[:, C_GA:C_GB], w[:, C_GB:C_GZ],
                             jnp.zeros((D_MODEL, SM_W - (C_GQKV - C_GATE) - 2 * N_GDN), w.dtype)], axis=1)
    return jnp.concatenate([w[:, 0:C_KV], w[:, C_KV:C_WIN], w[:, C_WIN:C_GATE], w[:, C_GQKV:C_GA],
                            w[:, C_GZ:IN_WIDTH], small], axis=1).astype(BF16)


def kernel(x_prompt, x_sample, cache_nsa_kv, state_nsa_win, state_gdn, state_gdn_conv, state_ffn_conv, page_table, p_prompt, p_sample, w_in, nsa_pe, nsa_phi, gdn_conv_w, gdn_A_log, gdn_dt_bias, gdn_norm_w, w_out, ln_g, ln_b, ffn_w_up, ffn_conv_w, ffn_w_down, ple_w_proj, ple_w_gate):
    b, t, d = x_prompt.shape
    nb, ts, _ = x_sample.shape
    assert d == D_MODEL and ts == 1 and t % Q_BLOCK == 0 and t >= WINDOW
    depth = w_in.shape[0]
    n_pool, page = cache_nsa_kv.shape[1], cache_nsa_kv.shape[2]
    wb = state_nsa_win.shape[2]
    tm_p = 512
    xp, xs = x_prompt, x_sample.reshape(nb, d)
    cache_all = cache_nsa_kv.transpose(0, 1, 3, 4, 5, 2).reshape(depth * n_pool, 4 * KV_W, page)
    wbuf_all = state_nsa_win.transpose(0, 1, 3, 4, 5, 2).reshape(depth * nb, 2 * KV_W, wb)
    gstate_all = state_gdn.transpose(0, 2, 3, 4, 1).reshape(depth * N_GDN, HEAD_DIM, HEAD_DIM, nb)
    p_all = p_prompt.reshape(depth * b, t, PLE_DIM)
    win_all = None
    gs_all = None
    kvt_all = None
    st_p, st_s = [], []
    for l in range(depth):
        w_cat = _prep_w_in(w_in[l])
        post_w = (w_out[l].astype(BF16), ln_g[l], ln_b[l], ffn_w_up[l].astype(BF16), ffn_conv_w[l],
                  ffn_w_down[l].astype(BF16), ple_w_proj[l].astype(BF16), ple_w_gate[l].astype(BF16))
        nw_t = jnp.tile(gdn_norm_w[l].reshape(1, HEAD_DIM), (1, N_GDN))

        q, kv, win, gqkv, gz, sm, kvt_all = _proj_in(xp.reshape(b * t, d), w_cat, 512, (kvt_all, l * b, t, depth * b))
        kv3, win3 = kv.reshape(b, t, 512), win.reshape(b, t, 256)
        sm3, gqkv3 = sm.reshape(b, t, SM_W), gqkv.reshape(b, t, GDN_QKV)
        kc, vc = _compress(kv3, nsa_pe[l], nsa_phi[l])
        o_nsa = _nsa_prompt(q.reshape(b, t, 512), kv3, kc, vc, win3, sm3)
        o_g, s_p = _gdn_prompt(gqkv3, gz.reshape(b, t, GDN_W), sm3, gdn_conv_w[l], gdn_A_log[l], gdn_dt_bias[l], nw_t)
        xp, fbuf_p = _post_prompt(o_nsa, o_g, xp, p_all, l * b, post_w, tm_p)
        st_p.append((None, win3[:, t - wb:].reshape(b, wb, 2, N_KV, HEAD_DIM),
                     s_p, gqkv3[:, t - (GDN_CONV - 1):], fbuf_p))

        q, kv, win, gqkv, gz, sm = _proj_in(xs, w_cat, nb)
        o_nsa, win_all = _nsa_decode(
            page_table, cache_all, l * n_pool, q.reshape(nb, 1, 512), kv.reshape(nb, 1, 512),
            win.reshape(nb, 1, 256), sm.reshape(nb, 1, SM_W), wbuf_all, l * nb, nsa_pe[l], nsa_phi[l], win_all)
        qn_t, kn_t, v_t, db_t, cnew = _gdn_dec_prep(gqkv, state_gdn_conv[l].reshape(nb, (GDN_CONV - 1) * GDN_QKV), sm,
                                                    gdn_conv_w[l], gdn_A_log[l], gdn_dt_bias[l])
        o_g_t, gs_all = _gdn_dec_step(qn_t, kn_t, v_t, db_t, gz.T, gstate_all, l * N_GDN,
                                      gdn_norm_w[l].reshape(HEAD_DIM, 1), gs_all)
        xs, fnew = _post_decode(o_nsa.reshape(nb, 512), o_g_t.T, xs, p_sample[l].reshape(nb, PLE_DIM),
                                state_ffn_conv[l].reshape(nb, (FFN_CONV - 1) * D_FF), post_w)
        st_s.append((kv.reshape(nb, 1, 4, N_KV, HEAD_DIM), None, None, cnew.reshape(nb, GDN_CONV - 1, GDN_QKV),
                     fnew.reshape(nb, FFN_CONV - 1, D_FF)))

    sample_states = {
        1: win_all.reshape(depth, nb, 2, N_KV, HEAD_DIM, wb).transpose(0, 1, 5, 2, 3, 4),
        2: gs_all.reshape(depth, N_GDN, HEAD_DIM, HEAD_DIM, nb).transpose(0, 4, 1, 2, 3),
    }
    kv_rows_prompt = kvt_all.reshape(depth, b, 4, N_KV, HEAD_DIM, t).transpose(0, 1, 5, 2, 3, 4)
    outs = [xp, xs.reshape(nb, 1, d)]
    for k in range(5):
        outs.append(kv_rows_prompt if k == 0 else jnp.stack([s[k] for s in st_p]))
        outs.append(sample_states[k] if k in sample_states else jnp.stack([s[k] for s in st_s]))
    return tuple(outs)
```
